```python
import math
import jax
import jax.numpy as jnp
from jax import lax
import numpy as np

D_MODEL = 1024
BATCH = 32
SEQ = 256
DEPTH = 4
DEC_BATCH = 2
DEC_SEQ = 1024
PAST_LEN = 512

GRID_W = 64
N_EVEN = (DEPTH + 1) // 2
N_ODD = DEPTH // 2
BRANCH = D_MODEL
CHUNK = 64
EPS = 1e-6
S5_GROUP = 16
S5_GROUPS = BRANCH // S5_GROUP
S5_P = 64
ML_HEADS = 4
ML_DV = BRANCH // ML_HEADS
ML_DK = ML_DV // 2
RET_HEADS = 4
RET_DV = BRANCH // RET_HEADS
RET_DK = RET_DV // 2
ROPE_BASE = 10000.0
DN_HEADS = 8
DN_DK = BRANCH // DN_HEADS
DN_DV = BRANCH // DN_HEADS
DN_CONV = 3
DN_QKV = 2 * DN_HEADS * DN_DK + DN_HEADS * DN_DV
AB_SIZES = (BRANCH, BRANCH, ML_HEADS * ML_DK, ML_HEADS * ML_DK, BRANCH, BRANCH, 2 * ML_HEADS, 2 * ML_HEADS, BRANCH)
CD_SIZES = (RET_HEADS * RET_DK, RET_HEADS * RET_DK, RET_HEADS * RET_DV, BRANCH, DN_QKV, 2 * DN_HEADS, 2 * DN_HEADS, BRANCH)
W_AB = sum(AB_SIZES)
W_CD = sum(CD_SIZES)
F32 = jnp.float32

kernel_name = 'bidir_hybrid_ssm_diffusion_step'


def _split_cols(x, sizes):
    return jnp.split(x, np.cumsum(sizes)[:-1].tolist(), axis=-1)


def _rms(x, gain=None):
    x32 = x.astype(F32)
    y = x32 * lax.rsqrt(jnp.mean(x32 * x32, axis=-1, keepdims=True) + EPS)
    if gain is not None:
        y = y * gain.astype(F32)
    return y


def _flip(x, rev, axis):
    return jnp.flip(x, axis) if rev else x


def _to_chunks(x):
    b, h, t = x.shape[:3]
    return jnp.moveaxis(x.reshape(b, h, t // CHUNK, CHUNK, *x.shape[3:]), 2, 0)


def _from_chunks(x):
    n, b, h, l = x.shape[:4]
    return jnp.moveaxis(x, 0, 2).reshape(b, h, n * l, *x.shape[4:])


def _heads(x, n_heads, d):
    b, t, _ = x.shape
    return jnp.swapaxes(x.astype(F32).reshape(b, t, n_heads, d), 1, 2)


def _lin_combine(left, right):
    a_l, b_l = left
    a_r, b_r = right
    return a_l * a_r, a_r * b_l + b_r


def _s5_scan(u, lam, step, b_mat, c_mat, x0):
    lam_dt = lam * step[:, None]
    lam_bar = jnp.exp(lam_dt)
    b_bar = ((lam_bar - 1.0) / lam)[..., None] * b_mat
    bu = jnp.einsum('gps,btgs->btgp', b_bar, u.astype(jnp.complex64))
    a = jnp.broadcast_to(lam_bar, bu.shape)
    _, xs = lax.associative_scan(_lin_combine, (a, bu), axis=1)
    t = u.shape[1]
    powers = jnp.exp(lam_dt[None] * jnp.arange(1, t + 1, dtype=F32)[:, None, None])
    xs = xs + powers[None] * x0[:, None]
    y = jnp.einsum('gsp,btgp->btgs', c_mat, xs).real
    return y, xs[:, -1]


def _s5_branch(u, z, lam_re, lam_im, log_step, b_re, b_im, c_re, c_im, d_skip, w_glu, st):
    bsz, t, _ = u.shape
    u32 = u.astype(F32)
    ug = u32.reshape(bsz, t, S5_GROUPS, S5_GROUP)
    ys, fins = [], []
    for d in range(2):
        lam = lax.complex(lam_re[d].astype(F32), lam_im[d].astype(F32))
        b_mat = lax.complex(b_re[d].astype(F32), b_im[d].astype(F32))
        c_mat = lax.complex(c_re[d].astype(F32), c_im[d].astype(F32))
        x0 = lax.complex(st[:, d, 0].astype(F32), st[:, d, 1].astype(F32))
        y, xf = _s5_scan(_flip(ug, d == 1, 1), lam, jnp.exp(log_step[d].astype(F32)), b_mat, c_mat, x0)
        ys.append(_flip(y, d == 1, 1))
        fins.append(jnp.stack([xf.real, xf.imag], axis=1))
    y = (ys[0] + ys[1]).reshape(bsz, t, BRANCH) + d_skip.astype(F32) * u32
    g = jax.nn.gelu(y)
    out = g * jax.nn.sigmoid(g @ w_glu.astype(F32)) * jax.nn.silu(z.astype(F32))
    return out, jnp.stack(fins, axis=1)


def _mlstm_chunked(q, k, v, i_pre, logf, c0, n0, m0):
    causal = jnp.tril(jnp.ones((CHUNK, CHUNK), bool))

    def step(carry, xs):
        c_st, n_st, m_st = carry
        q_, k_, v_, i_, lf = xs
        b = jnp.cumsum(lf, axis=-1)
        log_d = jnp.where(causal, b[..., :, None] - b[..., None, :] + i_[..., None, :], -jnp.inf)
        log_inter = b + m_st[..., None]
        m_row = jnp.maximum(log_inter, jnp.max(log_d, axis=-1))
        s = jnp.einsum('bhld,bhsd->bhls', q_, k_) * jnp.exp(log_d - m_row[..., None])
        inter = jnp.exp(log_inter - m_row)
        num = jnp.einsum('bhls,bhsv->bhlv', s, v_) + inter[..., None] * jnp.einsum('bhld,bhdv->bhlv', q_, c_st)
        den = jnp.sum(s, axis=-1) + inter * jnp.einsum('bhld,bhd->bhl', q_, n_st)
        h = num / jnp.maximum(jnp.abs(den), jnp.exp(-m_row))[..., None]
        b_last = b[..., -1]
        log_w = b_last[..., None] - b + i_
        m_new = jnp.maximum(b_last + m_st, jnp.max(log_w, axis=-1))
        w = jnp.exp(log_w - m_new[..., None])
        dec = jnp.exp(b_last + m_st - m_new)
        c_new = dec[..., None, None] * c_st + jnp.einsum('bhl,bhld,bhlv->bhdv', w, k_, v_)
        n_new = dec[..., None] * n_st + jnp.einsum('bhl,bhld->bhd', w, k_)
        return (c_new, n_new, m_new), h

    xs = tuple(_to_chunks(a) for a in (q, k, v, i_pre, logf))
    (c_f, n_f, m_f), hs = lax.scan(step, (c0.astype(F32), n0.astype(F32), m0.astype(F32)), xs)
    return _from_chunks(hs), c_f, n_f, m_f


def _mlstm_branch(q, k, v, o, ig, fg, z, i_bias, f_bias, norm_g, st_c, st_n, st_m):
    bsz, t, _ = q.shape
    qh = _heads(q, ML_HEADS, ML_DK)
    kh = _heads(k, ML_HEADS, ML_DK) * (ML_DK ** -0.5)
    vh = _heads(v, ML_HEADS, ML_DV)
    i_pre = ig.astype(F32).reshape(bsz, t, 2, ML_HEADS) + i_bias.astype(F32)
    logf = jax.nn.log_sigmoid(fg.astype(F32).reshape(bsz, t, 2, ML_HEADS) + f_bias.astype(F32))
    hs, cs, ns, ms = [], [], [], []
    for d in range(2):
        r = d == 1
        h, c_f, n_f, m_f = _mlstm_chunked(
            _flip(qh, r, 2), _flip(kh, r, 2), _flip(vh, r, 2),
            _flip(jnp.swapaxes(i_pre[:, :, d], 1, 2), r, 2), _flip(jnp.swapaxes(logf[:, :, d], 1, 2), r, 2),
            st_c[:, d], st_n[:, d], st_m[:, d])
        hs.append(_flip(h, r, 2))
        cs.append(c_f)
        ns.append(n_f)
        ms.append(m_f)
    h = _rms(jnp.swapaxes(hs[0] + hs[1], 1, 2)).reshape(bsz, t, BRANCH) * norm_g.astype(F32)
    out = jax.nn.sigmoid(o.astype(F32)) * h * jax.nn.silu(z.astype(F32))
    return out, (jnp.stack(cs, 1), jnp.stack(ns, 1), jnp.stack(ms, 1))


def _rope_2d(x, row, col):
    half = x.shape[-1] // 2
    quarter = half // 2
    freq = ROPE_BASE ** (-jnp.arange(quarter, dtype=F32) / quarter)

    def rot(xp, pos):
        ang = pos[:, None] * freq[None]
        cos = jnp.cos(ang)[None, :, None, :]
        sin = jnp.sin(ang)[None, :, None, :]
        x1, x2 = xp[..., :quarter], xp[..., quarter:]
        return jnp.concatenate([x1 * cos - x2 * sin, x1 * sin + x2 * cos], axis=-1)

    return jnp.concatenate([rot(x[..., :half], row), rot(x[..., half:], col)], axis=-1)


def _retention_chunked(q, k, v, log_gamma, r0):
    idx = jnp.arange(CHUNK, dtype=F32)
    lg = log_gamma[:, None]
    diff = idx[:, None] - idx[None, :]
    dmask = jnp.where(diff >= 0, jnp.exp(jnp.maximum(diff, 0.0)[None] * lg[..., None]), 0.0)
    q_dec = jnp.exp((idx + 1.0)[None] * lg)
    k_dec = jnp.exp((CHUNK - 1.0 - idx)[None] * lg)
    c_dec = jnp.exp(CHUNK * log_gamma)

    def step(r_st, xs):
        q_, k_, v_ = xs
        s = jnp.einsum('bhld,bhsd->bhls', q_, k_) * dmask
        o = jnp.einsum('bhls,bhsv->bhlv', s, v_) + jnp.einsum('bhld,bhdv->bhlv', q_ * q_dec[..., None], r_st)
        r_new = c_dec[:, None, None] * r_st + jnp.einsum('bhld,bhlv->bhdv', k_ * k_dec[..., None], v_)
        return r_new, o

    r_f, os_ = lax.scan(step, r0.astype(F32), (_to_chunks(q), _to_chunks(k), _to_chunks(v)))
    return _from_chunks(os_), r_f


def _retention_branch(q, k, v, z, decay_logit, st, pos):
    bsz, t, _ = q.shape
    qh = q.astype(F32).reshape(bsz, t, RET_HEADS, RET_DK)
    kh = k.astype(F32).reshape(bsz, t, RET_HEADS, RET_DK)
    if pos is not None:
        qh = _rope_2d(qh, pos[0], pos[1])
        kh = _rope_2d(kh, pos[0], pos[1])
    qh = jnp.swapaxes(qh, 1, 2)
    kh = jnp.swapaxes(kh, 1, 2) * (RET_DK ** -0.5)
    vh = _heads(v, RET_HEADS, RET_DV)
    log_gamma = jax.nn.log_sigmoid(decay_logit.astype(F32))
    os_, fins = [], []
    for d in range(2):
        r = d == 1
        o, r_f = _retention_chunked(_flip(qh, r, 2), _flip(kh, r, 2), _flip(vh, r, 2), log_gamma[d], st[:, d])
        os_.append(_flip(o, r, 2))
        fins.append(r_f)
    o = _rms(jnp.swapaxes(os_[0] + os_[1], 1, 2)).reshape(bsz, t, BRANCH)
    return o * jax.nn.silu(z.astype(F32)), jnp.stack(fins, 1)


def _centred_conv(x, w):
    kw, ch = w.shape
    return lax.conv_general_dilated(x, w[:, None, :], window_strides=(1,), padding=[(kw // 2, kw // 2)],
                                    dimension_numbers=('NWC', 'WIO', 'NWC'), feature_group_count=ch)


def _gated_delta_chunked(q, k, v, g, beta, s0):
    dv = v.shape[-1]
    qc, kc, vc, gc, bc = (_to_chunks(a) for a in (q, k, v, g, beta))
    tril = jnp.tril(jnp.ones((CHUNK, CHUNK), bool))
    strict = jnp.tril(jnp.ones((CHUNK, CHUNK), bool), -1)
    gcum = jnp.cumsum(gc, axis=-1)
    decay = jnp.exp(jnp.where(tril, gcum[..., :, None] - gcum[..., None, :], -jnp.inf))
    kb = kc * bc[..., None]
    a_mat = jnp.where(strict, jnp.einsum('nbhid,nbhjd->nbhij', kb, kc) * decay, 0.0)
    rhs = jnp.concatenate([vc * bc[..., None], kb * jnp.exp(gcum)[..., None]], axis=-1)
    sol = lax.linalg.triangular_solve(a_mat, rhs, left_side=True, lower=True, unit_diagonal=True)
    u, w = sol[..., :dv], sol[..., dv:]

    def step(s_st, xs):
        q_, k_, u_, w_, gcum_, decay_ = xs
        v_new = u_ - jnp.einsum('bhld,bhdv->bhlv', w_, s_st)
        attn = jnp.einsum('bhld,bhsd->bhls', q_, k_) * decay_
        o = jnp.einsum('bhld,bhdv->bhlv', q_ * jnp.exp(gcum_)[..., None], s_st) + jnp.einsum('bhls,bhsv->bhlv', attn, v_new)
        g_last = gcum_[..., -1]
        s_new = s_st * jnp.exp(g_last)[..., None, None] + jnp.einsum(
            'bhld,bhlv->bhdv', k_ * jnp.exp(g_last[..., None] - gcum_)[..., None], v_new)
        return s_new, o

    s_f, os_ = lax.scan(step, s0.astype(F32), (qc, kc, u, w, gcum, decay))
    return _from_chunks(os_), s_f


def _l2n(x):
    return x * lax.rsqrt(jnp.sum(x * x, axis=-1, keepdims=True) + EPS)


def _deltanet_branch(qkv, a, b, z, conv_w, a_log, dt_bias, norm_g, st):
    bsz, t, _ = qkv.shape
    qkv = jax.nn.silu(_centred_conv(qkv.astype(F32), conv_w.astype(F32)))
    q, k, v = _split_cols(qkv, (DN_HEADS * DN_DK, DN_HEADS * DN_DK, DN_HEADS * DN_DV))
    qh = _l2n(_heads(q, DN_HEADS, DN_DK)) * (DN_DK ** -0.5)
    kh = _l2n(_heads(k, DN_HEADS, DN_DK))
    vh = _heads(v, DN_HEADS, DN_DV)
    beta = jax.nn.sigmoid(b.astype(F32).reshape(bsz, t, 2, DN_HEADS))
    g = -jnp.exp(a_log.astype(F32)) * jax.nn.softplus(a.astype(F32).reshape(bsz, t, 2, DN_HEADS) + dt_bias.astype(F32))
    os_, fins = [], []
    for d in range(2):
        r = d == 1
        o, s_f = _gated_delta_chunked(_flip(qh, r, 2), _flip(kh, r, 2), _flip(vh, r, 2),
                                      _flip(jnp.swapaxes(g[:, :, d], 1, 2), r, 2),
                                      _flip(jnp.swapaxes(beta[:, :, d], 1, 2), r, 2), st[:, d])
        os_.append(_flip(o, r, 2))
        fins.append(s_f)
    o = _rms(jnp.swapaxes(os_[0] + os_[1], 1, 2), norm_g).reshape(bsz, t, BRANCH)
    return o * jax.nn.silu(z.astype(F32)), jnp.stack(fins, 1)


def _ab_mixer(h, w_in, w_out, s5p, mlp, st):
    u, za, q, k, v, o, ig, fg, zb = _split_cols(h @ w_in, AB_SIZES)
    ya, s5_f = _s5_branch(u, za, *s5p, st[0])
    yb, ml_f = _mlstm_branch(q, k, v, o, ig, fg, zb, *mlp, st[1], st[2], st[3])
    out = jnp.concatenate([ya, yb], axis=-1).astype(h.dtype) @ w_out
    return out, (s5_f, ml_f[0], ml_f[1], ml_f[2])


def _cd_mixer(h, w_in, w_out, ret_logit, dnp, st, pos):
    rq, rk, rv, zc, qkv, a, b, zd = _split_cols(h @ w_in, CD_SIZES)
    yc, ret_f = _retention_branch(rq, rk, rv, zc, ret_logit, st[0], pos)
    yd, dn_f = _deltanet_branch(qkv, a, b, zd, *dnp, st[1])
    out = jnp.concatenate([yc, yd], axis=-1).astype(h.dtype) @ w_out
    return out, (ret_f, dn_f)


def _modulation(cond, w, b):
    m = jax.nn.silu(cond) @ w + b
    return jnp.split(m, 3, axis=-1)


def setup_inputs(seed: int = 0) -> dict:
    key = jax.random.key(seed)
    it = iter(jax.random.split(key, 48))

    def nrm(shape, scale):
        return jax.random.normal(next(it), shape, F32) * scale

    def unif(shape, lo, hi):
        return jax.random.uniform(next(it), shape, F32, lo, hi)

    dt = jnp.exp(unif((N_ODD, 2, DN_HEADS), math.log(1e-3), math.log(1e-1)))
    ret_base = jnp.log(2.0 ** (5.0 + jnp.arange(RET_HEADS, dtype=F32)) - 1.0)
    return {
        'x_prompt': nrm((BATCH, SEQ, D_MODEL), 1.0),
        'x_sample': nrm((DEC_BATCH, DEC_SEQ, D_MODEL), 1.0),
        'state_s5': nrm((DEC_BATCH, N_EVEN, 2, 2, S5_GROUPS, S5_P), 0.3),
        'state_ml_c': nrm((DEC_BATCH, N_EVEN, 2, ML_HEADS, ML_DK, ML_DV), 0.3),
        'state_ml_n': nrm((DEC_BATCH, N_EVEN, 2, ML_HEADS, ML_DK), 0.3),
        'state_ml_m': unif((DEC_BATCH, N_EVEN, 2, ML_HEADS), 0.0, 3.0),
        'state_ret': nrm((DEC_BATCH, N_ODD, 2, RET_HEADS, RET_DK, RET_DV), 0.3),
        'state_dn': nrm((DEC_BATCH, N_ODD, 2, DN_HEADS, DN_DK, DN_DV), 0.1),
        'c': nrm((DEC_BATCH, D_MODEL), 1.0),
        'c_ctx': nrm((D_MODEL,), 1.0),
        'mod_w': nrm((DEPTH, D_MODEL, 3 * D_MODEL), D_MODEL ** -0.5),
        'mod_b': nrm((DEPTH, 3 * D_MODEL), 0.02),
        'norm_pre': 1.0 + nrm((DEPTH, D_MODEL), 0.05),
        'norm_post': 1.0 + nrm((DEPTH, D_MODEL), 0.05),
        'ab_w_in': nrm((N_EVEN, D_MODEL, W_AB), D_MODEL ** -0.5),
        'ab_w_out': nrm((N_EVEN, 2 * BRANCH, D_MODEL), (2 * BRANCH) ** -0.5),
        's5_lambda_re': -0.5 + nrm((N_EVEN, 2, S5_GROUPS, S5_P), 0.01),
        's5_lambda_im': math.pi * jnp.arange(S5_P, dtype=F32) + nrm((N_EVEN, 2, S5_GROUPS, S5_P), 0.01),
        's5_log_step': unif((N_EVEN, 2, S5_GROUPS), math.log(1e-3), math.log(1e-1)),
        's5_b_re': nrm((N_EVEN, 2, S5_GROUPS, S5_P, S5_GROUP), (2 * S5_GROUP) ** -0.5),
        's5_b_im': nrm((N_EVEN, 2, S5_GROUPS, S5_P, S5_GROUP), (2 * S5_GROUP) ** -0.5),
        's5_c_re': nrm((N_EVEN, 2, S5_GROUPS, S5_GROUP, S5_P), (2 * S5_P) ** -0.5),
        's5_c_im': nrm((N_EVEN, 2, S5_GROUPS, S5_GROUP, S5_P), (2 * S5_P) ** -0.5),
        's5_d': nrm((N_EVEN, BRANCH), 1.0),
        's5_w_glu': nrm((N_EVEN, BRANCH, BRANCH), BRANCH ** -0.5),
        'ml_i_bias': nrm((N_EVEN, 2, ML_HEADS), 0.1),
        'ml_f_bias': jnp.linspace(3.0, 6.0, ML_HEADS, dtype=F32) + nrm((N_EVEN, 2, ML_HEADS), 0.1),
        'ml_norm': 1.0 + nrm((N_EVEN, BRANCH), 0.05),
        'cd_w_in': nrm((N_ODD, D_MODEL, W_CD), D_MODEL ** -0.5),
        'cd_w_out': nrm((N_ODD, 2 * BRANCH, D_MODEL), (2 * BRANCH) ** -0.5),
        'ret_decay_logit': ret_base + nrm((N_ODD, 2, RET_HEADS), 0.1),
        'dn_conv': nrm((N_ODD, DN_CONV, DN_QKV), DN_CONV ** -0.5),
        'dn_a_log': jnp.log(unif((N_ODD, 2, DN_HEADS), 1.0, 16.0)),
        'dn_dt_bias': dt + jnp.log(-jnp.expm1(-dt)),
        'dn_norm': 1.0 + nrm((N_ODD, DN_DV), 0.05),
    }


def reference(x_prompt, x_sample, state_s5, state_ml_c, state_ml_n, state_ml_m, state_ret, state_dn,
              c, c_ctx, mod_w, mod_b, norm_pre, norm_post,
              ab_w_in, ab_w_out, s5_lambda_re, s5_lambda_im, s5_log_step, s5_b_re, s5_b_im,
              s5_c_re, s5_c_im, s5_d, s5_w_glu, ml_i_bias, ml_f_bias, ml_norm,
              cd_w_in, cd_w_out, ret_decay_logit, dn_conv, dn_a_log, dn_dt_bias, dn_norm):
    bp = x_prompt.shape[0]
    rows = x_sample.shape[1] // GRID_W
    t_idx = jnp.arange(rows * GRID_W)
    pos = ((t_idx // GRID_W).astype(F32), (t_idx % GRID_W).astype(F32))

    zero_ab = (jnp.zeros((bp, 2, 2, S5_GROUPS, S5_P), F32),
               jnp.zeros((bp, 2, ML_HEADS, ML_DK, ML_DV), F32),
               jnp.zeros((bp, 2, ML_HEADS, ML_DK), F32),
               jnp.zeros((bp, 2, ML_HEADS), F32))
    zero_cd = (jnp.zeros((bp, 2, RET_HEADS, RET_DK, RET_DV), F32),
               jnp.zeros((bp, 2, DN_HEADS, DN_DK, DN_DV), F32))

    xp, xs = x_prompt, x_sample
    new_s5, new_mc, new_mn, new_mm, new_ret, new_dn = [], [], [], [], [], []
    for l in range(DEPTH):
        j = l // 2
        sh_c, sc_c, g_c = _modulation(c_ctx, mod_w[l], mod_b[l])
        sh_s, sc_s, g_s = _modulation(c, mod_w[l], mod_b[l])
        hp = (_rms(xp, norm_pre[l]).astype(xp.dtype) * (1.0 + sc_c) + sh_c)
        hs = (_rms(xs, norm_pre[l]).astype(xs.dtype) * (1.0 + sc_s[:, None]) + sh_s[:, None])
        if l % 2 == 0:
            s5p = (s5_lambda_re[j], s5_lambda_im[j], s5_log_step[j], s5_b_re[j], s5_b_im[j],
                   s5_c_re[j], s5_c_im[j], s5_d[j], s5_w_glu[j])
            mlp = (ml_i_bias[j], ml_f_bias[j], ml_norm[j])
            op, st = _ab_mixer(hp, ab_w_in[j], ab_w_out[j], s5p, mlp, zero_ab)
            os_, _ = _ab_mixer(hs, ab_w_in[j], ab_w_out[j], s5p, mlp,
                               (state_s5[:, j], state_ml_c[:, j], state_ml_n[:, j], state_ml_m[:, j]))
            new_s5.append(st[0])
            new_mc.append(st[1])
            new_mn.append(st[2])
            new_mm.append(st[3])
        else:
            dnp = (dn_conv[j], dn_a_log[j], dn_dt_bias[j], dn_norm[j])
            op, st = _cd_mixer(hp, cd_w_in[j], cd_w_out[j], ret_decay_logit[j], dnp, zero_cd, None)
            os_, _ = _cd_mixer(hs, cd_w_in[j], cd_w_out[j], ret_decay_logit[j], dnp,
                               (state_ret[:, j], state_dn[:, j]), pos)
            new_ret.append(st[0])
            new_dn.append(st[1])
        xp = xp + g_c * _rms(op, norm_post[l]).astype(xp.dtype)
        xs = xs + g_s[:, None] * _rms(os_, norm_post[l]).astype(xs.dtype)
    return (xp, xs, jnp.stack(new_s5, 1), jnp.stack(new_mc, 1), jnp.stack(new_mn, 1),
            jnp.stack(new_mm, 1), jnp.stack(new_ret, 1), jnp.stack(new_dn, 1))
```

```python
import functools
import math

import jax
import jax.numpy as jnp
from jax import lax
from jax.experimental import pallas as pl
from jax.experimental.pallas import tpu as pltpu

F32 = jnp.float32
BF16 = jnp.bfloat16

EPS = 1e-6
GRID_W = 64
ROPE_BASE = 10000.0
S5_GROUP = 16
S5_P = 64
ML_HEADS = 4
RET_HEADS = 4
DN_HEADS = 8
LANES = 128
SUBLANES = 8
SEG = 256
ML_CHUNK = 128
DN_CHUNK = 64
VMEM_LIMIT = 56 * 1024 * 1024


def _cparams(sem):
    return pltpu.CompilerParams(dimension_semantics=sem, vmem_limit_bytes=VMEM_LIMIT)


def _bdot(a, b):
    return jnp.dot(a.astype(BF16), b.astype(BF16), preferred_element_type=F32)


def _bdot_nt(a, b):
    return lax.dot_general(a.astype(BF16), b.astype(BF16), (((1,), (1,)), ((), ())),
                           preferred_element_type=F32)


def _bdot_tn(a, b):
    return lax.dot_general(a.astype(BF16), b.astype(BF16), (((0,), (0,)), ((), ())),
                           preferred_element_type=F32)


def _sigmoid(x):
    return 1.0 / (1.0 + jnp.exp(-x))


def _silu(x):
    return x * _sigmoid(x)


def _softplus(x):
    return jnp.maximum(x, 0.0) + jnp.log(1.0 + jnp.exp(-jnp.abs(x)))


def _log_sigmoid(x):
    return -_softplus(-x)


def _gelu_tanh(x):
    return 0.5 * x * (1.0 + jnp.tanh(math.sqrt(2.0 / math.pi) * (x + 0.044715 * (x * x * x))))


def _rms_rows(x):
    return x * lax.rsqrt(jnp.mean(x * x, axis=-1, keepdims=True) + EPS)


def _mod_kernel(cond_ref, w_ref, b_ref, o_ref):
    o_ref[0] = _bdot(_silu(cond_ref[...]), w_ref[0]) + b_ref[0]


def _modulation(cond, mod_w, mod_b):
    depth, d, n3 = mod_w.shape
    tn = 1024
    return pl.pallas_call(
        _mod_kernel, name="modulation",
        grid=(depth, n3 // tn),
        in_specs=[pl.BlockSpec((SUBLANES, d), lambda l, n: (0, 0)),
                  pl.BlockSpec((1, d, tn), lambda l, n: (l, 0, n)),
                  pl.BlockSpec((1, 1, tn), lambda l, n: (l, 0, n))],
        out_specs=pl.BlockSpec((1, SUBLANES, tn), lambda l, n: (l, 0, n)),
        out_shape=jax.ShapeDtypeStruct((depth, SUBLANES, n3), F32),
        compiler_params=_cparams(("parallel", "parallel")),
    )(cond, mod_w, mod_b.reshape(depth, 1, n3))


def _inproj_kernel(n_tm, x_ref, mod_ref, g_ref, w_ref, wgt_ref, *out_refs):
    x = x_ref[...]
    h = _rms_rows(x) * g_ref[...]
    h = (h * mod_ref[0, 0:1, :] + mod_ref[0, 1:2, :]).astype(BF16)
    tm_ref = out_refs[0] if n_tm else None
    main_ref, gr_ref = out_refs[-2], out_refs[-1]
    step = 1024
    n_main = main_ref.shape[1]
    for n0 in range(0, n_tm, step):
        tm_ref[:, n0:n0 + step] = jnp.dot(h, w_ref[:, n0:n0 + step], preferred_element_type=F32)
    for n0 in range(0, n_main, step):
        main_ref[:, n0:n0 + step] = jnp.dot(h, w_ref[:, n_tm + n0:n_tm + n0 + step],
                                            preferred_element_type=F32)
    gr_ref[...] = lax.dot_general(wgt_ref[...], h, (((1,), (1,)), ((), ())),
                                  preferred_element_type=F32)


def _inproj(x2, mod, rows_per_mod, gain, w, wgt, n_tm):
    m, d = x2.shape
    n = w.shape[1]
    ng = wgt.shape[0]
    nt = m // SEG
    tiles_per_mod = rows_per_mod // SEG
    out_shape, out_specs = [], []
    if n_tm:
        out_shape.append(jax.ShapeDtypeStruct((SEG, nt * n_tm), F32))
        out_specs.append(pl.BlockSpec((SEG, n_tm), lambda i: (0, i)))
    out_shape += [jax.ShapeDtypeStruct((m, n - n_tm), F32), jax.ShapeDtypeStruct((ng, m), F32)]
    out_specs += [pl.BlockSpec((SEG, n - n_tm), lambda i: (i, 0)),
                  pl.BlockSpec((ng, SEG), lambda i: (0, i))]
    return pl.pallas_call(
        functools.partial(_inproj_kernel, n_tm), name="inproj",
        grid=(nt,),
        in_specs=[pl.BlockSpec((SEG, d), lambda i: (i, 0)),
                  pl.BlockSpec((1, 2, d), lambda i: (i // tiles_per_mod, 0, 0)),
                  pl.BlockSpec((1, d), lambda i: (0, 0)),
                  pl.BlockSpec((d, n), lambda i: (0, 0), pipeline_mode=pl.Buffered(1)),
                  pl.BlockSpec((ng, d), lambda i: (0, 0))],
        out_specs=out_specs,
        out_shape=out_shape,
        compiler_params=_cparams(("parallel",)),
    )(x2, mod, gain, w, wgt)


def _outproj_kernel(ya_ref, yb_ref, w_ref, x_ref, mod_ref, g_ref, o_ref):
    half = ya_ref.shape[1]
    acc = (jnp.dot(ya_ref[...].astype(BF16), w_ref[0:half, :], preferred_element_type=F32)
           + jnp.dot(yb_ref[...].astype(BF16), w_ref[half:2 * half, :], preferred_element_type=F32))
    o_ref[...] = x_ref[...] + mod_ref[0, 2:3, :] * (_rms_rows(acc) * g_ref[...])


def _outproj(ya, yb, w, x2, mod, rows_per_mod, gain):
    m, d = x2.shape
    br = ya.shape[1]
    tiles_per_mod = rows_per_mod // SEG
    return pl.pallas_call(
        _outproj_kernel, name="outproj",
        grid=(m // SEG,),
        in_specs=[pl.BlockSpec((SEG, br), lambda i: (i, 0)),
                  pl.BlockSpec((SEG, br), lambda i: (i, 0)),
                  pl.BlockSpec((2 * br, d), lambda i: (0, 0), pipeline_mode=pl.Buffered(1)),
                  pl.BlockSpec((SEG, d), lambda i: (i, 0)),
                  pl.BlockSpec((1, 3, d), lambda i: (i // tiles_per_mod, 0, 0)),
                  pl.BlockSpec((1, d), lambda i: (0, 0))],
        out_specs=pl.BlockSpec((SEG, d), lambda i: (i, 0)),
        out_shape=jax.ShapeDtypeStruct((m, d), F32),
        compiler_params=_cparams(("parallel",)),
    )(ya, yb, w, x2, mod, gain)


def _cmul(ar, ai, xr, xi):
    return ar * xr - ai * xi, ar * xi + ai * xr


def _s5_kernel(t_len, nseg, u_ref, bw_ref, cw_ref, a_ref, x0_ref, y_ref, xf_ref, bu_scr):
    hp = xf_ref.shape[1] // 4
    u = u_ref[...].reshape(t_len * SUBLANES, LANES)
    bu_scr[...] = _bdot(u, bw_ref[0])
    a = a_ref[0]
    coef = [jnp.broadcast_to(a[r:r + 1, :], (SUBLANES, hp)) for r in range(4)]

    def scan(init, store):
        def step(t, carry):
            xfr, xfi, xbr, xbi = carry
            rf = pl.multiple_of(t * SUBLANES, SUBLANES)
            rb = pl.multiple_of((t_len - 1 - t) * SUBLANES, SUBLANES)
            pr, pi = _cmul(coef[0], coef[1], xfr, xfi)
            nfr = pr + bu_scr[pl.ds(rf, SUBLANES), 0:hp]
            nfi = pi + bu_scr[pl.ds(rf, SUBLANES), hp:2 * hp]
            pr, pi = _cmul(coef[2], coef[3], xbr, xbi)
            nbr = pr + bu_scr[pl.ds(rb, SUBLANES), 2 * hp:3 * hp]
            nbi = pi + bu_scr[pl.ds(rb, SUBLANES), 3 * hp:4 * hp]
            if store:
                bu_scr[pl.ds(rf, SUBLANES), 0:hp] = nfr
                bu_scr[pl.ds(rf, SUBLANES), hp:2 * hp] = nfi
                bu_scr[pl.ds(rb, SUBLANES), 2 * hp:3 * hp] = nbr
                bu_scr[pl.ds(rb, SUBLANES), 3 * hp:4 * hp] = nbi
            return nfr, nfi, nbr, nbi
        return lax.fori_loop(0, t_len, step, init, unroll=4)

    x0 = tuple(x0_ref[:, r * hp:(r + 1) * hp] for r in range(4))
    if nseg > 1:
        zero = jnp.zeros((SUBLANES, hp), F32)
        ffr, ffi, fbr, fbi = scan((zero, zero, zero, zero), False)
        pfr, pfi, pbr, pbi = coef
        for _ in range(int(math.log2(t_len))):
            pfr, pfi = _cmul(pfr, pfi, pfr, pfi)
            pbr, pbi = _cmul(pbr, pbi, pbr, pbi)
        seg = lax.broadcasted_iota(jnp.int32, (SUBLANES, hp), 0) % nseg
        sfr, sfi, sbr, sbi = x0
        for _ in range(nseg - 1):
            tr, ti = _cmul(pfr, pfi, sfr, sfi)
            sfr = x0[0] + jnp.where(seg >= 1, pltpu.roll(tr + ffr, 1, 0), 0.0)
            sfi = x0[1] + jnp.where(seg >= 1, pltpu.roll(ti + ffi, 1, 0), 0.0)
            tr, ti = _cmul(pbr, pbi, sbr, sbi)
            sbr = x0[2] + jnp.where(seg <= nseg - 2, pltpu.roll(tr + fbr, SUBLANES - 1, 0), 0.0)
            sbi = x0[3] + jnp.where(seg <= nseg - 2, pltpu.roll(ti + fbi, SUBLANES - 1, 0), 0.0)
        x0 = (sfr, sfi, sbr, sbi)
    fin = scan(x0, True)
    for r in range(4):
        xf_ref[:, r * hp:(r + 1) * hp] = fin[r]
    y = _bdot(bu_scr[...], cw_ref[0])
    y_ref[...] = y.reshape(t_len, SUBLANES, LANES)


def _s5_scan(u_tm, bw, cw, acoef, x0, nseg):
    t_len, nb, br = u_tm.shape
    nj = br // LANES
    sw = bw.shape[2]
    assert t_len & (t_len - 1) == 0
    return pl.pallas_call(
        functools.partial(_s5_kernel, t_len, nseg), name="s5_scan",
        grid=(nb // SUBLANES, nj),
        in_specs=[pl.BlockSpec((t_len, SUBLANES, LANES), lambda i, j: (0, i, j)),
                  pl.BlockSpec((1, LANES, sw), lambda i, j: (j, 0, 0)),
                  pl.BlockSpec((1, sw, LANES), lambda i, j: (j, 0, 0)),
                  pl.BlockSpec((1, 4, sw // 4), lambda i, j: (j, 0, 0)),
                  pl.BlockSpec((SUBLANES, sw), lambda i, j: (i, j))],
        out_specs=[pl.BlockSpec((t_len, SUBLANES, LANES), lambda i, j: (0, i, j)),
                   pl.BlockSpec((SUBLANES, sw), lambda i, j: (i, j))],
        out_shape=[jax.ShapeDtypeStruct((t_len, nb, br), F32),
                   jax.ShapeDtypeStruct((nb, nj * sw), F32)],
        scratch_shapes=[pltpu.VMEM((t_len * SUBLANES, sw), F32)],
        compiler_params=_cparams(("parallel", "parallel")),
    )(u_tm, bw, cw, acoef, x0)


def _s5_glu_kernel(y_ref, u_ref, z_ref, d_ref, w_ref, o_ref):
    y = y_ref[...] + d_ref[...] * u_ref[...]
    g = _gelu_tanh(y)
    o_ref[...] = g * _sigmoid(_bdot(g, w_ref[...])) * _silu(z_ref[...])


def _s5_glu(y_tm2, u_tm2, main, d_skip, w_glu):
    br = d_skip.shape[1]
    nt = y_tm2.shape[1] // br
    return pl.pallas_call(
        _s5_glu_kernel, name="s5_glu",
        grid=(nt,),
        in_specs=[pl.BlockSpec((SEG, br), lambda i: (0, i)),
                  pl.BlockSpec((SEG, br), lambda i: (0, i)),
                  pl.BlockSpec((SEG, br), lambda i: (i, 0)),
                  pl.BlockSpec((1, br), lambda i: (0, 0)),
                  pl.BlockSpec((br, br), lambda i: (0, 0))],
        out_specs=pl.BlockSpec((SEG, br), lambda i: (i, 0)),
        out_shape=jax.ShapeDtypeStruct((nt * SEG, br), F32),
        compiler_params=_cparams(("parallel",)),
    )(y_tm2, u_tm2, main, d_skip, w_glu)


def _s5_weights(lam_re, lam_im, log_step, b_re, b_im, c_re, c_im):
    lam = lax.complex(lam_re.astype(F32), lam_im.astype(F32))
    step = jnp.exp(log_step.astype(F32))[..., None]
    lam_bar = jnp.exp(lam * step)
    b_bar = ((lam_bar - 1.0) / lam)[..., None] * lax.complex(b_re.astype(F32), b_im.astype(F32))
    g = lam.shape[1]
    gl = LANES // S5_GROUP
    nj = g // gl
    eye = jnp.eye(gl, dtype=F32)

    def bw_part(x):
        x = x.reshape(nj, gl, S5_P, S5_GROUP)
        return jnp.einsum('ab,jbps->jasbp', eye, x).reshape(nj, LANES, gl * S5_P)

    def cw_part(x):
        x = x.reshape(nj, gl, S5_GROUP, S5_P)
        return jnp.einsum('ab,jbsp->jbpas', eye, x).reshape(nj, gl * S5_P, LANES)

    bw = jnp.concatenate([bw_part(jnp.real(b_bar[0])), bw_part(jnp.imag(b_bar[0])),
                          bw_part(jnp.real(b_bar[1])), bw_part(jnp.imag(b_bar[1]))], axis=2)
    cw = jnp.concatenate([cw_part(c_re[0].astype(F32)), cw_part(-c_im[0].astype(F32)),
                          cw_part(c_re[1].astype(F32)), cw_part(-c_im[1].astype(F32))], axis=1)
    acoef = jnp.stack([jnp.real(lam_bar[0]), jnp.imag(lam_bar[0]),
                       jnp.real(lam_bar[1]), jnp.imag(lam_bar[1])], axis=0)
    acoef = acoef.reshape(4, nj, gl * S5_P).transpose(1, 0, 2)
    return bw.astype(BF16), cw.astype(BF16), acoef


def _s5_state_to_rows(st):
    nb, _, _, g, p = st.shape
    gl = LANES // S5_GROUP
    nj = g // gl
    return st.reshape(nb, 4, nj, gl * p).transpose(0, 2, 1, 3).reshape(nb, nj * 4 * gl * p)


def _s5_rows_to_state(rows, g, p):
    nb = rows.shape[0]
    gl = LANES // S5_GROUP
    nj = g // gl
    return rows.reshape(nb, nj, 4, gl * p).transpose(0, 2, 1, 3).reshape(nb, 2, 2, g, p)


def _chunk_masks(length):
    row = lax.broadcasted_iota(jnp.int32, (length, length), 0)
    col = lax.broadcasted_iota(jnp.int32, (length, length), 1)
    return row, col


def _to_col(eye, x_row):
    return jnp.sum(jnp.where(eye, x_row, 0.0), axis=1, keepdims=True)


def _cumsum_forms(x_row, eye, incl, incl_t):
    x_col = _to_col(eye, x_row)
    c_col = jnp.sum(jnp.where(incl, x_row, 0.0), axis=1, keepdims=True)
    c_row = jnp.sum(jnp.where(incl_t, x_col, 0.0), axis=0, keepdims=True)
    return c_col, c_row


def _dir_masks(row, col, d):
    if d == 0:
        return col <= row, row <= col, col < row
    return col >= row, row >= col, col > row


def _mlstm_kernel(t_len, has_init, emit_state, *refs):
    (q_ref, k_ref, v_ref, o_ref, z_ref, ig0_ref, ig1_ref, fg0_ref, fg1_ref,
     ib_ref, fb_ref, ng_ref) = refs[:12]
    pos = 12
    if has_init:
        c0_ref, n0_ref, m0_ref = refs[pos:pos + 3]
        pos += 3
    y_ref = refs[pos]
    pos += 1
    if emit_state:
        cf_ref, nf_ref, mf_ref = refs[pos:pos + 3]
        pos += 3
    hf_scr, hb_scr = refs[pos:pos + 2]

    ln = ML_CHUNK
    nch = t_len // ln
    dk = q_ref.shape[1]
    dv = v_ref.shape[1]
    b_id = pl.program_id(0)
    h_id = pl.program_id(1)
    nh = pl.num_programs(1)
    row, col = _chunk_masks(ln)
    eye = row == col
    scale = dk ** -0.5
    ig_refs = (ig0_ref, ig1_ref)
    fg_refs = (fg0_ref, fg1_ref)
    h_scrs = (hf_scr, hb_scr)
    masks = [_dir_masks(row, col, d) for d in range(2)]

    def chunk_step(d, n, carry):
        c_st, n_st, m_st = carry
        incl, incl_t, _ = masks[d]
        r0 = pl.multiple_of(n * ln, ln)
        q = q_ref[pl.ds(r0, ln), :]
        k = k_ref[pl.ds(r0, ln), :] * scale
        v = v_ref[pl.ds(r0, ln), :]
        i_row = ig_refs[d][0, 0, pl.ds(n, 1), :] + ib_ref[d * nh + h_id]
        lf_row = _log_sigmoid(fg_refs[d][0, 0, pl.ds(n, 1), :] + fb_ref[d * nh + h_id])
        i_col = _to_col(eye, i_row)
        b_col, b_row = _cumsum_forms(lf_row, eye, incl, incl_t)
        c_row = i_row - b_row
        c_col = i_col - b_col
        cm = jnp.where(incl, c_row, -jnp.inf)
        mu = jnp.maximum(m_st, jnp.max(cm, axis=1, keepdims=True))
        s = _bdot_nt(q, k) * jnp.exp(cm - mu)
        inter = jnp.exp(m_st - mu)
        num = _bdot(s, v) + inter * _bdot(q, c_st)
        den = jnp.sum(s, axis=1, keepdims=True) + inter * jnp.sum(q * n_st, axis=1, keepdims=True)
        h = num / jnp.maximum(jnp.abs(den), jnp.exp(-b_col - mu))
        h_scrs[d][pl.ds(r0, ln), :] = h
        mu_last = jnp.maximum(m_st, jnp.max(c_row, axis=1, keepdims=True))
        b_last = jnp.sum(lf_row, axis=1, keepdims=True)
        kw = k * jnp.exp(c_col - mu_last)
        dec = jnp.exp(m_st - mu_last)
        c_new = dec * c_st + _bdot_tn(kw, v)
        n_new = dec * n_st + jnp.sum(kw, axis=0, keepdims=True)
        return c_new, n_new, b_last + mu_last

    def body(i, carry):
        cf = chunk_step(0, i, carry[0])
        cb = chunk_step(1, nch - 1 - i, carry[1])
        return cf, cb

    init = []
    for d in range(2):
        if has_init:
            m0 = jnp.zeros((1, 1), F32) + m0_ref[(b_id * 2 + d) * nh + h_id]
            init.append((c0_ref[0, d, 0], n0_ref[0, d, 0], m0))
        else:
            init.append((jnp.zeros((dk, dv), F32), jnp.zeros((1, dk), F32), jnp.zeros((1, 1), F32)))
    fin = lax.fori_loop(0, nch, body, tuple(init))
    if emit_state:
        for d in range(2):
            cf_ref[0, d, 0] = fin[d][0]
            nf_ref[0, d, 0] = fin[d][1]
            mf_ref[0, d, 0] = jnp.broadcast_to(fin[d][2], (1, LANES))
    h = _rms_rows(hf_scr[...] + hb_scr[...]) * ng_ref[...]
    y_ref[...] = _sigmoid(o_ref[...]) * h * _silu(z_ref[...])


def _mlstm(main, gr, cols, i_bias, f_bias, norm_g, nb, t_len, init, emit_state):
    m = main.shape[0]
    nh = ML_HEADS
    dv = norm_g.shape[1] // nh
    dk = dv // 2
    nch = t_len // ML_CHUNK
    gr3 = gr.reshape(gr.shape[0], nb, nch, ML_CHUNK)

    def colspec(width, off):
        return pl.BlockSpec((t_len, width), lambda b, h: (b, off // width + h))

    def gspec(base):
        return pl.BlockSpec((1, 1, nch, ML_CHUNK), lambda b, h: (base + h, b, 0, 0))

    smem = pl.BlockSpec(memory_space=pltpu.SMEM)
    in_specs = [colspec(dk, cols['q']), colspec(dk, cols['k']), colspec(dv, cols['v']),
                colspec(dv, cols['o']), colspec(dv, cols['z']),
                gspec(0), gspec(nh), gspec(2 * nh), gspec(3 * nh), smem, smem,
                pl.BlockSpec((1, dv), lambda b, h: (0, h))]
    args = [main, main, main, main, main, gr3, gr3, gr3, gr3,
            i_bias.reshape(-1), f_bias.reshape(-1), norm_g]
    if init is not None:
        c0, n0, m0 = init
        in_specs += [pl.BlockSpec((1, 2, 1, dk, dv), lambda b, h: (b, 0, h, 0, 0)),
                     pl.BlockSpec((1, 2, 1, 1, dk), lambda b, h: (b, 0, h, 0, 0)), smem]
        args += [c0, n0.reshape(nb, 2, nh, 1, dk), m0.reshape(-1)]
    out_shape = [jax.ShapeDtypeStruct((m, nh * dv), F32)]
    out_specs = [pl.BlockSpec((t_len, dv), lambda b, h: (b, h))]
    if emit_state:
        out_shape += [jax.ShapeDtypeStruct((nb, 2, nh, dk, dv), F32),
                      jax.ShapeDtypeStruct((nb, 2, nh, 1, dk), F32),
                      jax.ShapeDtypeStruct((nb, 2, nh, 1, LANES), F32)]
        out_specs += [pl.BlockSpec((1, 2, 1, dk, dv), lambda b, h: (b, 0, h, 0, 0)),
                      pl.BlockSpec((1, 2, 1, 1, dk), lambda b, h: (b, 0, h, 0, 0)),
                      pl.BlockSpec((1, 2, 1, 1, LANES), lambda b, h: (b, 0, h, 0, 0))]
    return pl.pallas_call(
        functools.partial(_mlstm_kernel, t_len, init is not None, emit_state), name="mlstm",
        grid=(nb, nh),
        in_specs=in_specs,
        out_specs=out_specs,
        out_shape=out_shape,
        scratch_shapes=[pltpu.VMEM((t_len, dv), F32), pltpu.VMEM((t_len, dv), F32)],
        compiler_params=_cparams(("parallel", "parallel")),
    )(*args)


def _rope_swap(x):
    quarter = x.shape[1] // 4
    lane = lax.broadcasted_iota(jnp.int32, x.shape, 1)
    up = pltpu.roll(x, x.shape[1] - quarter, 1)
    down = pltpu.roll(x, quarter, 1)
    return jnp.where((lane // quarter) % 2 == 0, up, down)


def _retention_kernel(t_len, has_pos, has_init, emit_state, *refs):
    q_ref, k_ref, v_ref, z_ref, lg_ref = refs[:5]
    pos = 5
    if has_pos:
        cos_ref, sin_ref = refs[pos:pos + 2]
        pos += 2
    if has_init:
        r0_ref = refs[pos]
        pos += 1
    y_ref = refs[pos]
    pos += 1
    if emit_state:
        rf_ref = refs[pos]
        pos += 1
    of_scr, ob_scr, q_scr, k_scr = refs[pos:pos + 4]

    ln = ML_CHUNK
    nch = t_len // ln
    dk = q_ref.shape[1]
    dv = v_ref.shape[1]
    h_id = pl.program_id(1)
    nh = pl.num_programs(1)
    row, col = _chunk_masks(ln)
    scale = dk ** -0.5
    o_scrs = (of_scr, ob_scr)

    q = q_ref[...]
    k = k_ref[...]
    if has_pos:
        q = q * cos_ref[...] + _rope_swap(q) * sin_ref[...]
        k = k * cos_ref[...] + _rope_swap(k) * sin_ref[...]
    q_scr[...] = q
    k_scr[...] = k * scale

    dist = jnp.abs(row - col).astype(F32)
    iota_col = lax.broadcasted_iota(jnp.int32, (ln, 1), 0).astype(F32)
    consts = []
    for d in range(2):
        incl = _dir_masks(row, col, d)[0]
        lg = _log_sigmoid(jnp.zeros((1, 1), F32) + lg_ref[d * nh + h_id])
        order = iota_col if d == 0 else (ln - 1.0) - iota_col
        dmask = jnp.where(incl, jnp.exp(dist * lg), 0.0)
        consts.append((dmask, jnp.exp((order + 1.0) * lg), jnp.exp((ln - 1.0 - order) * lg),
                       jnp.exp(float(ln) * lg)))

    def chunk_step(d, n, r_st):
        dmask, q_dec, k_dec, c_dec = consts[d]
        r0 = pl.multiple_of(n * ln, ln)
        qc = q_scr[pl.ds(r0, ln), :]
        kc = k_scr[pl.ds(r0, ln), :]
        vc = v_ref[pl.ds(r0, ln), :]
        s = _bdot_nt(qc, kc) * dmask
        o_scrs[d][pl.ds(r0, ln), :] = _bdot(s, vc) + _bdot(qc * q_dec, r_st)
        return c_dec * r_st + _bdot_tn(kc * k_dec, vc)

    def body(i, carry):
        return chunk_step(0, i, carry[0]), chunk_step(1, nch - 1 - i, carry[1])

    if has_init:
        init = (r0_ref[0, 0, 0], r0_ref[0, 1, 0])
    else:
        init = (jnp.zeros((dk, dv), F32), jnp.zeros((dk, dv), F32))
    fin = lax.fori_loop(0, nch, body, init)
    if emit_state:
        rf_ref[0, 0, 0] = fin[0]
        rf_ref[0, 1, 0] = fin[1]
    y_ref[...] = _rms_rows(of_scr[...] + ob_scr[...]) * _silu(z_ref[...])


def _retention(main, cols, decay_logit, nb, t_len, rope, init, emit_state):
    m = main.shape[0]
    nh = RET_HEADS
    dk = (cols['k'] - cols['q']) // nh
    dv = 2 * dk

    def colspec(width, off):
        return pl.BlockSpec((t_len, width), lambda b, h: (b, off // width + h))

    smem = pl.BlockSpec(memory_space=pltpu.SMEM)
    in_specs = [colspec(dk, cols['q']), colspec(dk, cols['k']), colspec(dv, cols['v']),
                colspec(dv, cols['z']), smem]
    args = [main, main, main, main, decay_logit.reshape(-1)]
    if rope is not None:
        in_specs += [pl.BlockSpec((t_len, dk), lambda b, h: (0, 0))] * 2
        args += list(rope)
    if init is not None:
        in_specs.append(pl.BlockSpec((1, 2, 1, dk, dv), lambda b, h: (b, 0, h, 0, 0)))
        args.append(init)
    out_shape = [jax.ShapeDtypeStruct((m, nh * dv), F32)]
    out_specs = [pl.BlockSpec((t_len, dv), lambda b, h: (b, h))]
    if emit_state:
        out_shape.append(jax.ShapeDtypeStruct((nb, 2, nh, dk, dv), F32))
        out_specs.append(pl.BlockSpec((1, 2, 1, dk, dv), lambda b, h: (b, 0, h, 0, 0)))
    return pl.pallas_call(
        functools.partial(_retention_kernel, t_len, rope is not None, init is not None, emit_state),
        name="retention",
        grid=(nb, nh),
        in_specs=in_specs,
        out_specs=out_specs,
        out_shape=out_shape,
        scratch_shapes=[pltpu.VMEM((t_len, dv), F32), pltpu.VMEM((t_len, dv), F32),
                        pltpu.VMEM((t_len, dk), F32), pltpu.VMEM((t_len, dk), F32)],
        compiler_params=_cparams(("parallel", "parallel")),
    )(*args)


def _rope_tables(t_len, dk):
    quarter = dk // 4
    t_idx = jnp.arange(t_len)
    row = (t_idx // GRID_W).astype(F32)
    colp = (t_idx % GRID_W).astype(F32)
    freq = ROPE_BASE ** (-jnp.arange(quarter, dtype=F32) / quarter)
    ar = row[:, None] * freq[None]
    ac = colp[:, None] * freq[None]
    cos = jnp.concatenate([jnp.cos(ar), jnp.cos(ar), jnp.cos(ac), jnp.cos(ac)], axis=1)
    sin = jnp.concatenate([-jnp.sin(ar), jnp.sin(ar), -jnp.sin(ac), jnp.sin(ac)], axis=1)
    return cos, sin


def _l2n(x):
    return x * lax.rsqrt(jnp.sum(x * x, axis=-1, keepdims=True) + EPS)


def _split_bf16(x):
    hi = x.astype(BF16)
    return hi, (x - hi.astype(F32)).astype(BF16)


def _dot3(a, b):
    a_hi, a_lo = _split_bf16(a)
    b_hi, b_lo = _split_bf16(b)
    dot = functools.partial(jnp.dot, preferred_element_type=F32)
    return dot(a_hi, b_hi) + (dot(a_hi, b_lo) + dot(a_lo, b_hi))


_INV_BASE = 8


def _unit_triangular_inverse(neg_a, row, col):
    ln = neg_a.shape[0]

    def same_block(size):
        return (row // size) == (col // size)

    p = jnp.where(same_block(_INV_BASE), neg_a, 0.0)
    inv = jnp.where(row == col, 1.0, 0.0) + p
    for _ in range(int(math.log2(_INV_BASE)) - 1):
        p = _dot3(p, p)
        inv = inv + _dot3(inv, p)
    size = _INV_BASE
    while size < ln:
        off = jnp.where(same_block(2 * size) & jnp.logical_not(same_block(size)), neg_a, 0.0)
        inv = inv + _dot3(_dot3(inv, off), inv)
        size *= 2
    return inv


def _deltanet_kernel(t_len, has_init, emit_state, *refs):
    (q_ref, k_ref, v_ref, z_ref, a0_ref, a1_ref, b0_ref, b1_ref, cw_ref,
     alog_ref, dtb_ref, ng_ref) = refs[:12]
    pos = 12
    if has_init:
        s0_ref = refs[pos]
        pos += 1
    y_ref = refs[pos]
    pos += 1
    if emit_state:
        sf_ref = refs[pos]
        pos += 1
    of_scr, ob_scr, q_scr, k_scr, v_scr = refs[pos:pos + 5]

    ln = DN_CHUNK
    nch = t_len // ln
    dk = q_ref.shape[1]
    dv = v_ref.shape[1]
    h_id = pl.program_id(1)
    nh = pl.num_programs(1)
    row, col = _chunk_masks(ln)
    eye = row == col
    o_scrs = (of_scr, ob_scr)
    a_refs = (a0_ref, a1_ref)
    b_refs = (b0_ref, b1_ref)
    masks = [_dir_masks(row, col, d) for d in range(2)]

    trow = lax.broadcasted_iota(jnp.int32, (t_len, 1), 0)

    def conv_silu(x_ref, wi):
        x = x_ref[...]
        w = cw_ref[0, wi]
        prev = jnp.where(trow == 0, 0.0, pltpu.roll(x, 1, 0))
        nxt = jnp.where(trow == t_len - 1, 0.0, pltpu.roll(x, t_len - 1, 0))
        return _silu(prev * w[0:1, :] + x * w[1:2, :] + nxt * w[2:3, :])

    q_scr[...] = _l2n(conv_silu(q_ref, 0)) * (dk ** -0.5)
    k_scr[...] = _l2n(conv_silu(k_ref, 1))
    v_scr[...] = conv_silu(v_ref, 2)

    def chunk_step(d, n, s_st):
        incl, incl_t, strict = masks[d]
        r0 = pl.multiple_of(n * ln, ln)
        q = q_scr[pl.ds(r0, ln), :]
        k = k_scr[pl.ds(r0, ln), :]
        v = v_scr[pl.ds(r0, ln), :]
        a_row = a_refs[d][0, 0, pl.ds(n, 1), :]
        beta_row = _sigmoid(b_refs[d][0, 0, pl.ds(n, 1), :])
        neg_rate = -jnp.exp(jnp.zeros((1, 1), F32) + alog_ref[d * nh + h_id])
        g_row = neg_rate * _softplus(a_row + dtb_ref[d * nh + h_id])
        beta = _to_col(eye, beta_row)
        gc_col, gc_row = _cumsum_forms(g_row, eye, incl, incl_t)
        decay = jnp.exp(jnp.where(incl, gc_col - gc_row, -jnp.inf))
        kb = k * beta
        neg_a = jnp.where(strict, -(_bdot_nt(kb, k) * decay), 0.0)
        t_inv = _unit_triangular_inverse(neg_a, row, col)
        r = _dot3(t_inv, jnp.concatenate([v * beta, kb * jnp.exp(gc_col)], axis=1))
        u = r[:, :dv]
        w = r[:, dv:]
        v_new = u - _bdot(w, s_st)
        attn = _bdot_nt(q, k) * decay
        o_scrs[d][pl.ds(r0, ln), :] = _bdot(q * jnp.exp(gc_col), s_st) + _bdot(attn, v_new)
        g_last = jnp.sum(g_row, axis=1, keepdims=True)
        return s_st * jnp.exp(g_last) + _bdot_tn(k * jnp.exp(g_last - gc_col), v_new)

    def body(i, carry):
        return chunk_step(0, i, carry[0]), chunk_step(1, nch - 1 - i, carry[1])

    if has_init:
        init = (s0_ref[0, 0, 0], s0_ref[0, 1, 0])
    else:
        init = (jnp.zeros((dk, dv), F32), jnp.zeros((dk, dv), F32))
    fin = lax.fori_loop(0, nch, body, init)
    if emit_state:
        sf_ref[0, 0, 0] = fin[0]
        sf_ref[0, 1, 0] = fin[1]
    o = _rms_rows(of_scr[...] + ob_scr[...]) * ng_ref[...]
    y_ref[...] = o * _silu(z_ref[...])


def _deltanet(main, gr, cols, conv_w, a_log, dt_bias, norm_g, nb, t_len, init, emit_state):
    m = main.shape[0]
    nh = DN_HEADS
    dk = norm_g.shape[1]
    dv = dk
    nch = t_len // DN_CHUNK
    gr3 = gr.reshape(gr.shape[0], nb, nch, DN_CHUNK)
    cw = conv_w.reshape(conv_w.shape[0], 3, nh, dk).transpose(2, 1, 0, 3)

    def colspec(off):
        return pl.BlockSpec((t_len, dk), lambda b, h: (b, off // dk + h))

    def gspec(base):
        return pl.BlockSpec((1, 1, nch, DN_CHUNK), lambda b, h: (base + h, b, 0, 0))

    smem = pl.BlockSpec(memory_space=pltpu.SMEM)
    in_specs = [colspec(cols['q']), colspec(cols['k']), colspec(cols['v']), colspec(cols['z']),
                gspec(0), gspec(nh), gspec(2 * nh), gspec(3 * nh),
                pl.BlockSpec((1, 3, 3, dk), lambda b, h: (h, 0, 0, 0)), smem, smem,
                pl.BlockSpec((1, dv), lambda b, h: (0, 0))]
    args = [main, main, main, main, gr3, gr3, gr3, gr3, cw,
            a_log.reshape(-1), dt_bias.reshape(-1), norm_g]
    if init is not None:
        in_specs.append(pl.BlockSpec((1, 2, 1, dk, dv), lambda b, h: (b, 0, h, 0, 0)))
        args.append(init)
    out_shape = [jax.ShapeDtypeStruct((m, nh * dv), F32)]
    out_specs = [pl.BlockSpec((t_len, dv), lambda b, h: (b, h))]
    if emit_state:
        out_shape.append(jax.ShapeDtypeStruct((nb, 2, nh, dk, dv), F32))
        out_specs.append(pl.BlockSpec((1, 2, 1, dk, dv), lambda b, h: (b, 0, h, 0, 0)))
    return pl.pallas_call(
        functools.partial(_deltanet_kernel, t_len, init is not None, emit_state), name="deltanet",
        grid=(nb, nh),
        in_specs=in_specs,
        out_specs=out_specs,
        out_shape=out_shape,
        scratch_shapes=[pltpu.VMEM((t_len, dv), F32), pltpu.VMEM((t_len, dv), F32),
                        pltpu.VMEM((t_len, dk), F32), pltpu.VMEM((t_len, dk), F32),
                        pltpu.VMEM((t_len, dv), F32)],
        compiler_params=_cparams(("parallel", "parallel")),
    )(*args)


def _pack_cols(w, sizes, order, gate_names):
    offs = {}
    o = 0
    for name, sz in sizes:
        offs[name] = (o, sz)
        o += sz
    main = jnp.concatenate([w[:, offs[n][0]:offs[n][0] + offs[n][1]] for n in order], axis=1)
    gates = jnp.concatenate([w[:, offs[n][0]:offs[n][0] + offs[n][1]] for n in gate_names], axis=1)
    return main.astype(BF16), gates.T.astype(BF16)


def kernel(x_prompt, x_sample, state_s5, state_ml_c, state_ml_n, state_ml_m, state_ret, state_dn,
           c, c_ctx, mod_w, mod_b, norm_pre, norm_post,
           ab_w_in, ab_w_out, s5_lambda_re, s5_lambda_im, s5_log_step, s5_b_re, s5_b_im,
           s5_c_re, s5_c_im, s5_d, s5_w_glu, ml_i_bias, ml_f_bias, ml_norm,
           cd_w_in, cd_w_out, ret_decay_logit, dn_conv, dn_a_log, dn_dt_bias, dn_norm):
    bp, tp, d = x_prompt.shape
    bs, ts, _ = x_sample.shape
    depth = mod_w.shape[0]
    br = s5_d.shape[1]
    g_s5 = s5_lambda_re.shape[2]
    ml_dv = br // ML_HEADS
    ml_dk = ml_dv // 2
    ret_dv = br // RET_HEADS
    ret_dk = ret_dv // 2
    dn_dk = br // DN_HEADS
    assert tp == SEG and ts % SEG == 0 and (bs * ts // SEG) == SUBLANES and bp % SUBLANES == 0
    nseg = ts // SEG

    cond = jnp.zeros((SUBLANES, d), F32).at[0].set(c_ctx).at[1:1 + bs].set(c)
    mod = _modulation(cond, mod_w, mod_b)
    mod = mod.reshape(depth, SUBLANES, 3, d)
    mod = mod.at[:, :, 1].add(1.0)
    mod = mod[:, :, jnp.array([1, 0, 2])]

    ab_sizes = (('u', br), ('za', br), ('q', ML_HEADS * ml_dk), ('k', ML_HEADS * ml_dk), ('v', br),
                ('o', br), ('ig', 2 * ML_HEADS), ('fg', 2 * ML_HEADS), ('zb', br))
    ab_order = ('u', 'za', 'q', 'k', 'v', 'o', 'zb')
    ab_cols = {'za': 0, 'q': br, 'k': br + ML_HEADS * ml_dk, 'v': br + 2 * ML_HEADS * ml_dk,
               'o': 2 * br + 2 * ML_HEADS * ml_dk, 'z': 3 * br + 2 * ML_HEADS * ml_dk}
    dn_qkv = 3 * br
    cd_sizes = (('rq', RET_HEADS * ret_dk), ('rk', RET_HEADS * ret_dk), ('rv', br), ('zc', br),
                ('qkv', dn_qkv), ('a', 2 * DN_HEADS), ('b', 2 * DN_HEADS), ('zd', br))
    cd_order = ('rq', 'rk', 'rv', 'zc', 'qkv', 'zd')
    ret_cols = {'q': 0, 'k': RET_HEADS * ret_dk, 'v': 2 * RET_HEADS * ret_dk,
                'z': 2 * RET_HEADS * ret_dk + br}
    dn_base = 2 * RET_HEADS * ret_dk + 2 * br
    dn_cols = {'q': dn_base, 'k': dn_base + br, 'v': dn_base + 2 * br, 'z': dn_base + 3 * br}

    rope = _rope_tables(ts, ret_dk)

    xp = x_prompt.reshape(bp * tp, d)
    xs = x_sample.reshape(bs * ts, d)
    new_s5, new_mc, new_mn, new_mm, new_ret, new_dn = [], [], [], [], [], []
    for l in range(depth):
        j = l // 2
        mod_p = mod[l, 0:1]
        mod_s = mod[l, 1:1 + bs]
        g_pre = norm_pre[l][None]
        g_post = norm_post[l][None]
        if l % 2 == 0:
            w_main, w_gt = _pack_cols(ab_w_in[j], ab_sizes, ab_order, ('ig', 'fg'))
            bw, cw, acoef = _s5_weights(s5_lambda_re[j], s5_lambda_im[j], s5_log_step[j],
                                        s5_b_re[j], s5_b_im[j], s5_c_re[j], s5_c_im[j])
            w_glu = s5_w_glu[j].astype(BF16)
            w_out = ab_w_out[j].astype(BF16)
            outs = []
            for (x2, nb, t_len, md, is_sample) in ((xp, bp, tp, mod_p, False), (xs, bs, ts, mod_s, True)):
                rpm = x2.shape[0] // md.shape[0]
                u_tm, main, gr = _inproj(x2, md[:, 0:2], rpm, g_pre, w_main, w_gt, br)
                nrow = x2.shape[0] // SEG
                if is_sample:
                    st = state_s5[:, j].astype(F32)
                    x0 = jnp.zeros((bs, nseg, 2, 2, g_s5, S5_P), F32)
                    x0 = x0.at[:, 0, 0].set(st[:, 0]).at[:, nseg - 1, 1].set(st[:, 1])
                    x0 = _s5_state_to_rows(x0.reshape(bs * nseg, 2, 2, g_s5, S5_P))
                else:
                    x0 = jnp.zeros((nrow, 4 * g_s5 * S5_P), F32)
                y_tm, xf = _s5_scan(u_tm.reshape(SEG, nrow, br), bw, cw, acoef, x0,
                                    nseg if is_sample else 1)
                ya = _s5_glu(y_tm.reshape(SEG, nrow * br), u_tm, main, s5_d[j][None], w_glu)
                init = None
                if is_sample:
                    init = (state_ml_c[:, j].astype(F32), state_ml_n[:, j].astype(F32),
                            state_ml_m[:, j].astype(F32))
                res = _mlstm(main, gr, ab_cols, ml_i_bias[j], ml_f_bias[j], ml_norm[j][None],
                             nb, t_len, init, not is_sample)
                yb = res[0]
                if not is_sample:
                    new_s5.append(_s5_rows_to_state(xf, g_s5, S5_P))
                    new_mc.append(res[1])
                    new_mn.append(res[2][:, :, :, 0, :])
                    new_mm.append(res[3][:, :, :, 0, 0])
                outs.append(_outproj(ya, yb, w_out, x2, md, rpm, g_post))
            xp, xs = outs
        else:
            w_main, w_gt = _pack_cols(cd_w_in[j], cd_sizes, cd_order, ('a', 'b'))
            w_out = cd_w_out[j].astype(BF16)
            outs = []
            for (x2, nb, t_len, md, is_sample) in ((xp, bp, tp, mod_p, False), (xs, bs, ts, mod_s, True)):
                rpm = x2.shape[0] // md.shape[0]
                main, gr = _inproj(x2, md[:, 0:2], rpm, g_pre, w_main, w_gt, 0)
                res_c = _retention(main, ret_cols, ret_decay_logit[j], nb, t_len,
                                   rope if is_sample else None,
                                   state_ret[:, j].astype(F32) if is_sample else None, not is_sample)
                res_d = _deltanet(main, gr, dn_cols, dn_conv[j], dn_a_log[j], dn_dt_bias[j],
                                  dn_norm[j][None], nb, t_len,
                                  state_dn[:, j].astype(F32) if is_sample else None, not is_sample)
                if not is_sample:
                    new_ret.append(res_c[1])
                    new_dn.append(res_d[1])
                outs.append(_outproj(res_c[0], res_d[0], w_out, x2, md, rpm, g_post))
            xp, xs = outs
    return (xp.reshape(bp, tp, d), xs.reshape(bs, ts, d), jnp.stack(new_s5, 1), jnp.stack(new_mc, 1),
            jnp.stack(new_mn, 1), jnp.stack(new_mm, 1), jnp.stack(new_ret, 1), jnp.stack(new_dn, 1))
```

```python
import functools
import math

import jax
import jax.numpy as jnp
from jax import lax
from jax.experimental import pallas as pl
from jax.experimental.pallas import tpu as pltpu

F32 = jnp.float32
BF16 = jnp.bfloat16

EPS = 1e-6
GRID_W = 64
ROPE_BASE = 10000.0
S5_GROUP = 16
S5_P = 64
ML_HEADS = 4
RET_HEADS = 4
DN_HEADS = 8
LANES = 128
SUBLANES = 8
SEG = 256
ML_CHUNK = 128
DN_CHUNK = 256
VMEM_LIMIT = 56 * 1024 * 1024


def _cparams(sem):
    return pltpu.CompilerParams(dimension_semantics=sem, vmem_limit_bytes=VMEM_LIMIT)


def _bdot(a, b):
    return jnp.dot(a.astype(BF16), b.astype(BF16), preferred_element_type=F32)


def _bdot_nt(a, b):
    return lax.dot_general(a.astype(BF16), b.astype(BF16), (((1,), (1,)), ((), ())),
                           preferred_element_type=F32)


def _bdot_tn(a, b):
    return lax.dot_general(a.astype(BF16), b.astype(BF16), (((0,), (0,)), ((), ())),
                           preferred_element_type=F32)


def _sigmoid(x):
    return 1.0 / (1.0 + jnp.exp(-x))


def _silu(x):
    return x * _sigmoid(x)


def _softplus(x):
    return jnp.maximum(x, 0.0) + jnp.log(1.0 + jnp.exp(-jnp.abs(x)))


def _log_sigmoid(x):
    return -_softplus(-x)


def _gelu_tanh(x):
    return 0.5 * x * (1.0 + jnp.tanh(math.sqrt(2.0 / math.pi) * (x + 0.044715 * (x * x * x))))


def _rms_rows(x):
    return x * lax.rsqrt(jnp.mean(x * x, axis=-1, keepdims=True) + EPS)


def _mod_kernel(cond_ref, w_ref, b_ref, o_ref):
    o_ref[0] = _bdot(_silu(cond_ref[...]), w_ref[0]) + b_ref[0]


def _modulation(cond, mod_w, mod_b):
    depth, d, n3 = mod_w.shape
    tn = 1024
    return pl.pallas_call(
        _mod_kernel, name="modulation",
        grid=(depth, n3 // tn),
        in_specs=[pl.BlockSpec((SUBLANES, d), lambda l, n: (0, 0)),
                  pl.BlockSpec((1, d, tn), lambda l, n: (l, 0, n)),
                  pl.BlockSpec((1, 1, tn), lambda l, n: (l, 0, n))],
        out_specs=pl.BlockSpec((1, SUBLANES, tn), lambda l, n: (l, 0, n)),
        out_shape=jax.ShapeDtypeStruct((depth, SUBLANES, n3), F32),
        compiler_params=_cparams(("parallel", "parallel")),
    )(cond, mod_w, mod_b.reshape(depth, 1, n3))


def _inproj_kernel(n_tm, x_ref, mod_ref, g_ref, w_ref, wgt_ref, *out_refs):
    x = x_ref[...]
    h = _rms_rows(x) * g_ref[...]
    h = (h * mod_ref[0, 0:1, :] + mod_ref[0, 1:2, :]).astype(BF16)
    tm_ref = out_refs[0] if n_tm else None
    main_ref, gr_ref = out_refs[-2], out_refs[-1]
    step = 1024
    n_main = main_ref.shape[1]
    for n0 in range(0, n_tm, step):
        tm_ref[:, n0:n0 + step] = jnp.dot(h, w_ref[:, n0:n0 + step], preferred_element_type=F32)
    for n0 in range(0, n_main, step):
        main_ref[:, n0:n0 + step] = jnp.dot(h, w_ref[:, n_tm + n0:n_tm + n0 + step],
                                            preferred_element_type=F32)
    gr_ref[...] = lax.dot_general(wgt_ref[...], h, (((1,), (1,)), ((), ())),
                                  preferred_element_type=F32)


def _inproj(x2, mod, rows_per_mod, gain, w, wgt, n_tm):
    m, d = x2.shape
    n = w.shape[1]
    ng = wgt.shape[0]
    nt = m // SEG
    tiles_per_mod = rows_per_mod // SEG
    out_shape, out_specs = [], []
    if n_tm:
        out_shape.append(jax.ShapeDtypeStruct((SEG, nt * n_tm), F32))
        out_specs.append(pl.BlockSpec((SEG, n_tm), lambda i: (0, i)))
    out_shape += [jax.ShapeDtypeStruct((m, n - n_tm), F32), jax.ShapeDtypeStruct((ng, m), F32)]
    out_specs += [pl.BlockSpec((SEG, n - n_tm), lambda i: (i, 0)),
                  pl.BlockSpec((ng, SEG), lambda i: (0, i))]
    return pl.pallas_call(
        functools.partial(_inproj_kernel, n_tm), name="inproj",
        grid=(nt,),
        in_specs=[pl.BlockSpec((SEG, d), lambda i: (i, 0)),
                  pl.BlockSpec((1, 2, d), lambda i: (i // tiles_per_mod, 0, 0)),
                  pl.BlockSpec((1, d), lambda i: (0, 0)),
                  pl.BlockSpec((d, n), lambda i: (0, 0), pipeline_mode=pl.Buffered(1)),
                  pl.BlockSpec((ng, d), lambda i: (0, 0))],
        out_specs=out_specs,
        out_shape=out_shape,
        compiler_params=_cparams(("parallel",)),
    )(x2, mod, gain, w, wgt)


def _outproj_kernel(ya_ref, yb_ref, w_ref, x_ref, mod_ref, g_ref, o_ref):
    half = ya_ref.shape[1]
    acc = (jnp.dot(ya_ref[...].astype(BF16), w_ref[0:half, :], preferred_element_type=F32)
           + jnp.dot(yb_ref[...].astype(BF16), w_ref[half:2 * half, :], preferred_element_type=F32))
    o_ref[...] = x_ref[...] + mod_ref[0, 2:3, :] * (_rms_rows(acc) * g_ref[...])


def _outproj(ya, yb, w, x2, mod, rows_per_mod, gain):
    m, d = x2.shape
    br = ya.shape[1]
    tiles_per_mod = rows_per_mod // SEG
    return pl.pallas_call(
        _outproj_kernel, name="outproj",
        grid=(m // SEG,),
        in_specs=[pl.BlockSpec((SEG, br), lambda i: (i, 0)),
                  pl.BlockSpec((SEG, br), lambda i: (i, 0)),
                  pl.BlockSpec((2 * br, d), lambda i: (0, 0), pipeline_mode=pl.Buffered(1)),
                  pl.BlockSpec((SEG, d), lambda i: (i, 0)),
                  pl.BlockSpec((1, 3, d), lambda i: (i // tiles_per_mod, 0, 0)),
                  pl.BlockSpec((1, d), lambda i: (0, 0))],
        out_specs=pl.BlockSpec((SEG, d), lambda i: (i, 0)),
        out_shape=jax.ShapeDtypeStruct((m, d), F32),
        compiler_params=_cparams(("parallel",)),
    )(ya, yb, w, x2, mod, gain)


def _cmul(ar, ai, xr, xi):
    return ar * xr - ai * xi, ar * xi + ai * xr


def _s5_kernel(t_len, nseg, u_ref, bw_ref, cw_ref, a_ref, x0_ref, y_ref, xf_ref, bu_scr):
    hp = xf_ref.shape[1] // 4
    u = u_ref[...].reshape(t_len * SUBLANES, LANES)
    bu_scr[...] = _bdot(u, bw_ref[0])
    a = a_ref[0]
    coef = [jnp.broadcast_to(a[r:r + 1, :], (SUBLANES, hp)) for r in range(4)]

    def scan(init, store):
        def step(t, carry):
            xfr, xfi, xbr, xbi = carry
            rf = pl.multiple_of(t * SUBLANES, SUBLANES)
            rb = pl.multiple_of((t_len - 1 - t) * SUBLANES, SUBLANES)
            pr, pi = _cmul(coef[0], coef[1], xfr, xfi)
            nfr = pr + bu_scr[pl.ds(rf, SUBLANES), 0:hp]
            nfi = pi + bu_scr[pl.ds(rf, SUBLANES), hp:2 * hp]
            pr, pi = _cmul(coef[2], coef[3], xbr, xbi)
            nbr = pr + bu_scr[pl.ds(rb, SUBLANES), 2 * hp:3 * hp]
            nbi = pi + bu_scr[pl.ds(rb, SUBLANES), 3 * hp:4 * hp]
            if store:
                bu_scr[pl.ds(rf, SUBLANES), 0:hp] = nfr
                bu_scr[pl.ds(rf, SUBLANES), hp:2 * hp] = nfi
                bu_scr[pl.ds(rb, SUBLANES), 2 * hp:3 * hp] = nbr
                bu_scr[pl.ds(rb, SUBLANES), 3 * hp:4 * hp] = nbi
            return nfr, nfi, nbr, nbi
        return lax.fori_loop(0, t_len, step, init, unroll=4)

    x0 = tuple(x0_ref[:, r * hp:(r + 1) * hp] for r in range(4))
    if nseg > 1:
        zero = jnp.zeros((SUBLANES, hp), F32)
        ffr, ffi, fbr, fbi = scan((zero, zero, zero, zero), False)
        pfr, pfi, pbr, pbi = coef
        for _ in range(int(math.log2(t_len))):
            pfr, pfi = _cmul(pfr, pfi, pfr, pfi)
            pbr, pbi = _cmul(pbr, pbi, pbr, pbi)
        seg = lax.broadcasted_iota(jnp.int32, (SUBLANES, hp), 0) % nseg
        sfr, sfi, sbr, sbi = x0
        for _ in range(nseg - 1):
            tr, ti = _cmul(pfr, pfi, sfr, sfi)
            sfr = x0[0] + jnp.where(seg >= 1, pltpu.roll(tr + ffr, 1, 0), 0.0)
            sfi = x0[1] + jnp.where(seg >= 1, pltpu.roll(ti + ffi, 1, 0), 0.0)
            tr, ti = _cmul(pbr, pbi, sbr, sbi)
            sbr = x0[2] + jnp.where(seg <= nseg - 2, pltpu.roll(tr + fbr, SUBLANES - 1, 0), 0.0)
            sbi = x0[3] + jnp.where(seg <= nseg - 2, pltpu.roll(ti + fbi, SUBLANES - 1, 0), 0.0)
        x0 = (sfr, sfi, sbr, sbi)
    fin = scan(x0, True)
    for r in range(4):
        xf_ref[:, r * hp:(r + 1) * hp] = fin[r]
    y = _bdot(bu_scr[...], cw_ref[0])
    y_ref[...] = y.reshape(t_len, SUBLANES, LANES)


def _s5_scan(u_tm, bw, cw, acoef, x0, nseg):
    t_len, nb, br = u_tm.shape
    nj = br // LANES
    sw = bw.shape[2]
    assert t_len & (t_len - 1) == 0
    return pl.pallas_call(
        functools.partial(_s5_kernel, t_len, nseg), name="s5_scan",
        grid=(nb // SUBLANES, nj),
        in_specs=[pl.BlockSpec((t_len, SUBLANES, LANES), lambda i, j: (0, i, j)),
                  pl.BlockSpec((1, LANES, sw), lambda i, j: (j, 0, 0)),
                  pl.BlockSpec((1, sw, LANES), lambda i, j: (j, 0, 0)),
                  pl.BlockSpec((1, 4, sw // 4), lambda i, j: (j, 0, 0)),
                  pl.BlockSpec((SUBLANES, sw), lambda i, j: (i, j))],
        out_specs=[pl.BlockSpec((t_len, SUBLANES, LANES), lambda i, j: (0, i, j)),
                   pl.BlockSpec((SUBLANES, sw), lambda i, j: (i, j))],
        out_shape=[jax.ShapeDtypeStruct((t_len, nb, br), F32),
                   jax.ShapeDtypeStruct((nb, nj * sw), F32)],
        scratch_shapes=[pltpu.VMEM((t_len * SUBLANES, sw), F32)],
        compiler_params=_cparams(("parallel", "parallel")),
    )(u_tm, bw, cw, acoef, x0)


def _s5_glu_kernel(y_ref, u_ref, z_ref, d_ref, w_ref, o_ref):
    y = y_ref[...] + d_ref[...] * u_ref[...]
    g = _gelu_tanh(y)
    o_ref[...] = g * _sigmoid(_bdot(g, w_ref[...])) * _silu(z_ref[...])


def _s5_glu(y_tm2, u_tm2, main, d_skip, w_glu):
    br = d_skip.shape[1]
    nt = y_tm2.shape[1] // br
    return pl.pallas_call(
        _s5_glu_kernel, name="s5_glu",
        grid=(nt,),
        in_specs=[pl.BlockSpec((SEG, br), lambda i: (0, i)),
                  pl.BlockSpec((SEG, br), lambda i: (0, i)),
                  pl.BlockSpec((SEG, br), lambda i: (i, 0)),
                  pl.BlockSpec((1, br), lambda i: (0, 0)),
                  pl.BlockSpec((br, br), lambda i: (0, 0))],
        out_specs=pl.BlockSpec((SEG, br), lambda i: (i, 0)),
        out_shape=jax.ShapeDtypeStruct((nt * SEG, br), F32),
        compiler_params=_cparams(("parallel",)),
    )(y_tm2, u_tm2, main, d_skip, w_glu)


def _s5_weights(lam_re, lam_im, log_step, b_re, b_im, c_re, c_im):
    lr, li = lam_re.astype(F32), lam_im.astype(F32)
    step = jnp.exp(log_step.astype(F32))[..., None]
    mag = jnp.exp(lr * step)
    a_r, a_i = mag * jnp.cos(li * step), mag * jnp.sin(li * step)
    den = lr * lr + li * li
    c_r = ((a_r - 1.0) * lr + a_i * li) / den
    c_i = (a_i * lr - (a_r - 1.0) * li) / den
    br_, bi_ = b_re.astype(F32), b_im.astype(F32)
    bbar_r = c_r[..., None] * br_ - c_i[..., None] * bi_
    bbar_i = c_r[..., None] * bi_ + c_i[..., None] * br_
    g = lr.shape[1]
    gl = LANES // S5_GROUP
    nj = g // gl
    eye = jnp.eye(gl, dtype=F32)

    def bw_part(x):
        x = x.reshape(nj, gl, S5_P, S5_GROUP)
        return jnp.einsum('ab,jbps->jasbp', eye, x).reshape(nj, LANES, gl * S5_P)

    def cw_part(x):
        x = x.reshape(nj, gl, S5_GROUP, S5_P)
        return jnp.einsum('ab,jbsp->jbpas', eye, x).reshape(nj, gl * S5_P, LANES)

    bw = jnp.concatenate([bw_part(bbar_r[0]), bw_part(bbar_i[0]),
                          bw_part(bbar_r[1]), bw_part(bbar_i[1])], axis=2)
    cw = jnp.concatenate([cw_part(c_re[0].astype(F32)), cw_part(-c_im[0].astype(F32)),
                          cw_part(c_re[1].astype(F32)), cw_part(-c_im[1].astype(F32))], axis=1)
    acoef = jnp.stack([a_r[0], a_i[0], a_r[1], a_i[1]], axis=0)
    acoef = acoef.reshape(4, nj, gl * S5_P).transpose(1, 0, 2)
    return bw.astype(BF16), cw.astype(BF16), acoef


def _s5_state_to_rows(st):
    nb, _, _, g, p = st.shape
    gl = LANES // S5_GROUP
    nj = g // gl
    return st.reshape(nb, 4, nj, gl * p).transpose(0, 2, 1, 3).reshape(nb, nj * 4 * gl * p)


def _s5_rows_to_state(rows, g, p):
    nb = rows.shape[0]
    gl = LANES // S5_GROUP
    nj = g // gl
    return rows.reshape(nb, nj, 4, gl * p).transpose(0, 2, 1, 3).reshape(nb, 2, 2, g, p)


def _chunk_masks(length):
    row = lax.broadcasted_iota(jnp.int32, (length, length), 0)
    col = lax.broadcasted_iota(jnp.int32, (length, length), 1)
    return row, col


def _to_col(eye, x_row):
    return jnp.sum(jnp.where(eye, x_row, 0.0), axis=1, keepdims=True)


def _cumsum_forms(x_row, eye, incl, incl_t):
    x_col = _to_col(eye, x_row)
    c_col = jnp.sum(jnp.where(incl, x_row, 0.0), axis=1, keepdims=True)
    c_row = jnp.sum(jnp.where(incl_t, x_col, 0.0), axis=0, keepdims=True)
    return c_col, c_row


def _dir_masks(row, col, d):
    if d == 0:
        return col <= row, row <= col, col < row
    return col >= row, row >= col, col > row


def _mlstm_kernel(t_len, has_init, emit_state, *refs):
    (q_ref, k_ref, v_ref, o_ref, z_ref, ig0_ref, ig1_ref, fg0_ref, fg1_ref,
     ib_ref, fb_ref, ng_ref) = refs[:12]
    pos = 12
    if has_init:
        c0_ref, n0_ref, m0_ref = refs[pos:pos + 3]
        pos += 3
    y_ref = refs[pos]
    pos += 1
    if emit_state:
        cf_ref, nf_ref, mf_ref = refs[pos:pos + 3]
        pos += 3
    hf_scr, hb_scr = refs[pos:pos + 2]

    ln = ML_CHUNK
    nch = t_len // ln
    dk = q_ref.shape[1]
    dv = v_ref.shape[1]
    b_id = pl.program_id(0)
    h_id = pl.program_id(1)
    nh = pl.num_programs(1)
    row, col = _chunk_masks(ln)
    eye = row == col
    scale = dk ** -0.5
    ig_refs = (ig0_ref, ig1_ref)
    fg_refs = (fg0_ref, fg1_ref)
    h_scrs = (hf_scr, hb_scr)
    masks = [_dir_masks(row, col, d) for d in range(2)]

    def chunk_step(d, n, carry):
        c_st, n_st, m_st = carry
        incl, incl_t, _ = masks[d]
        r0 = pl.multiple_of(n * ln, ln)
        q = q_ref[pl.ds(r0, ln), :]
        k = k_ref[pl.ds(r0, ln), :] * scale
        v = v_ref[pl.ds(r0, ln), :]
        i_row = ig_refs[d][0, 0, pl.ds(n, 1), :] + ib_ref[d * nh + h_id]
        lf_row = _log_sigmoid(fg_refs[d][0, 0, pl.ds(n, 1), :] + fb_ref[d * nh + h_id])
        i_col = _to_col(eye, i_row)
        b_col, b_row = _cumsum_forms(lf_row, eye, incl, incl_t)
        c_row = i_row - b_row
        c_col = i_col - b_col
        cm = jnp.where(incl, c_row, -jnp.inf)
        mu = jnp.maximum(m_st, jnp.max(cm, axis=1, keepdims=True))
        s = _bdot_nt(q, k) * jnp.exp(cm - mu)
        inter = jnp.exp(m_st - mu)
        num = _bdot(s, v) + inter * _bdot(q, c_st)
        den = jnp.sum(s, axis=1, keepdims=True) + inter * jnp.sum(q * n_st, axis=1, keepdims=True)
        h = num / jnp.maximum(jnp.abs(den), jnp.exp(-b_col - mu))
        h_scrs[d][pl.ds(r0, ln), :] = h
        mu_last = jnp.maximum(m_st, jnp.max(c_row, axis=1, keepdims=True))
        b_last = jnp.sum(lf_row, axis=1, keepdims=True)
        kw = k * jnp.exp(c_col - mu_last)
        dec = jnp.exp(m_st - mu_last)
        c_new = dec * c_st + _bdot_tn(kw, v)
        n_new = dec * n_st + jnp.sum(kw, axis=0, keepdims=True)
        return c_new, n_new, b_last + mu_last

    def body(i, carry):
        cf = chunk_step(0, i, carry[0])
        cb = chunk_step(1, nch - 1 - i, carry[1])
        return cf, cb

    init = []
    for d in range(2):
        if has_init:
            m0 = jnp.zeros((1, 1), F32) + m0_ref[(b_id * 2 + d) * nh + h_id]
            init.append((c0_ref[0, d, 0], n0_ref[0, d, 0], m0))
        else:
            init.append((jnp.zeros((dk, dv), F32), jnp.zeros((1, dk), F32), jnp.zeros((1, 1), F32)))
    fin = lax.fori_loop(0, nch, body, tuple(init))
    if emit_state:
        for d in range(2):
            cf_ref[0, d, 0] = fin[d][0]
            nf_ref[0, d, 0] = fin[d][1]
            mf_ref[0, d, 0] = jnp.broadcast_to(fin[d][2], (1, LANES))
    h = _rms_rows(hf_scr[...] + hb_scr[...]) * ng_ref[...]
    y_ref[...] = _sigmoid(o_ref[...]) * h * _silu(z_ref[...])


def _mlstm(main, gr, cols, i_bias, f_bias, norm_g, nb, t_len, init, emit_state):
    m = main.shape[0]
    nh = ML_HEADS
    dv = norm_g.shape[1] // nh
    dk = dv // 2
    nch = t_len // ML_CHUNK
    gr3 = gr.reshape(gr.shape[0], nb, nch, ML_CHUNK)

    def colspec(width, off):
        return pl.BlockSpec((t_len, width), lambda b, h: (b, off // width + h))

    def gspec(base):
        return pl.BlockSpec((1, 1, nch, ML_CHUNK), lambda b, h: (base + h, b, 0, 0))

    smem = pl.BlockSpec(memory_space=pltpu.SMEM)
    in_specs = [colspec(dk, cols['q']), colspec(dk, cols['k']), colspec(dv, cols['v']),
                colspec(dv, cols['o']), colspec(dv, cols['z']),
                gspec(0), gspec(nh), gspec(2 * nh), gspec(3 * nh), smem, smem,
                pl.BlockSpec((1, dv), lambda b, h: (0, h))]
    args = [main, main, main, main, main, gr3, gr3, gr3, gr3,
            i_bias.reshape(-1), f_bias.reshape(-1), norm_g]
    if init is not None:
        c0, n0, m0 = init
        in_specs += [pl.BlockSpec((1, 2, 1, dk, dv), lambda b, h: (b, 0, h, 0, 0)),
                     pl.BlockSpec((1, 2, 1, 1, dk), lambda b, h: (b, 0, h, 0, 0)), smem]
        args += [c0, n0.reshape(nb, 2, nh, 1, dk), m0.reshape(-1)]
    out_shape = [jax.ShapeDtypeStruct((m, nh * dv), F32)]
    out_specs = [pl.BlockSpec((t_len, dv), lambda b, h: (b, h))]
    if emit_state:
        out_shape += [jax.ShapeDtypeStruct((nb, 2, nh, dk, dv), F32),
                      jax.ShapeDtypeStruct((nb, 2, nh, 1, dk), F32),
                      jax.ShapeDtypeStruct((nb, 2, nh, 1, LANES), F32)]
        out_specs += [pl.BlockSpec((1, 2, 1, dk, dv), lambda b, h: (b, 0, h, 0, 0)),
                      pl.BlockSpec((1, 2, 1, 1, dk), lambda b, h: (b, 0, h, 0, 0)),
                      pl.BlockSpec((1, 2, 1, 1, LANES), lambda b, h: (b, 0, h, 0, 0))]
    return pl.pallas_call(
        functools.partial(_mlstm_kernel, t_len, init is not None, emit_state), name="mlstm",
        grid=(nb, nh),
        in_specs=in_specs,
        out_specs=out_specs,
        out_shape=out_shape,
        scratch_shapes=[pltpu.VMEM((t_len, dv), F32), pltpu.VMEM((t_len, dv), F32)],
        compiler_params=_cparams(("parallel", "parallel")),
    )(*args)


def _rope_swap(x):
    quarter = x.shape[1] // 4
    lane = lax.broadcasted_iota(jnp.int32, x.shape, 1)
    up = pltpu.roll(x, x.shape[1] - quarter, 1)
    down = pltpu.roll(x, quarter, 1)
    return jnp.where((lane // quarter) % 2 == 0, up, down)


def _retention_kernel(t_len, has_pos, has_init, emit_state, *refs):
    q_ref, k_ref, v_ref, z_ref, lg_ref = refs[:5]
    pos = 5
    if has_pos:
        cos_ref, sin_ref = refs[pos:pos + 2]
        pos += 2
    if has_init:
        r0_ref = refs[pos]
        pos += 1
    y_ref = refs[pos]
    pos += 1
    if emit_state:
        rf_ref = refs[pos]
        pos += 1
    of_scr, ob_scr, q_scr, k_scr = refs[pos:pos + 4]

    ln = ML_CHUNK
    nch = t_len // ln
    dk = q_ref.shape[1]
    dv = v_ref.shape[1]
    h_id = pl.program_id(1)
    nh = pl.num_programs(1)
    row, col = _chunk_masks(ln)
    scale = dk ** -0.5
    o_scrs = (of_scr, ob_scr)

    q = q_ref[...]
    k = k_ref[...]
    if has_pos:
        q = q * cos_ref[...] + _rope_swap(q) * sin_ref[...]
        k = k * cos_ref[...] + _rope_swap(k) * sin_ref[...]
    q_scr[...] = q
    k_scr[...] = k * scale

    dist = jnp.abs(row - col).astype(F32)
    iota_col = lax.broadcasted_iota(jnp.int32, (ln, 1), 0).astype(F32)
    consts = []
    for d in range(2):
        incl = _dir_masks(row, col, d)[0]
        lg = _log_sigmoid(jnp.zeros((1, 1), F32) + lg_ref[d * nh + h_id])
        order = iota_col if d == 0 else (ln - 1.0) - iota_col
        dmask = jnp.where(incl, jnp.exp(dist * lg), 0.0)
        consts.append((dmask, jnp.exp((order + 1.0) * lg), jnp.exp((ln - 1.0 - order) * lg),
                       jnp.exp(float(ln) * lg)))

    def chunk_step(d, n, r_st):
        dmask, q_dec, k_dec, c_dec = consts[d]
        r0 = pl.multiple_of(n * ln, ln)
        qc = q_scr[pl.ds(r0, ln), :]
        kc = k_scr[pl.ds(r0, ln), :]
        vc = v_ref[pl.ds(r0, ln), :]
        s = _bdot_nt(qc, kc) * dmask
        o_scrs[d][pl.ds(r0, ln), :] = _bdot(s, vc) + _bdot(qc * q_dec, r_st)
        return c_dec * r_st + _bdot_tn(kc * k_dec, vc)

    def body(i, carry):
        return chunk_step(0, i, carry[0]), chunk_step(1, nch - 1 - i, carry[1])

    if has_init:
        init = (r0_ref[0, 0, 0], r0_ref[0, 1, 0])
    else:
        init = (jnp.zeros((dk, dv), F32), jnp.zeros((dk, dv), F32))
    fin = lax.fori_loop(0, nch, body, init)
    if emit_state:
        rf_ref[0, 0, 0] = fin[0]
        rf_ref[0, 1, 0] = fin[1]
    y_ref[...] = _rms_rows(of_scr[...] + ob_scr[...]) * _silu(z_ref[...])


def _retention(main, cols, decay_logit, nb, t_len, rope, init, emit_state):
    m = main.shape[0]
    nh = RET_HEADS
    dk = (cols['k'] - cols['q']) // nh
    dv = 2 * dk

    def colspec(width, off):
        return pl.BlockSpec((t_len, width), lambda b, h: (b, off // width + h))

    smem = pl.BlockSpec(memory_space=pltpu.SMEM)
    in_specs = [colspec(dk, cols['q']), colspec(dk, cols['k']), colspec(dv, cols['v']),
                colspec(dv, cols['z']), smem]
    args = [main, main, main, main, decay_logit.reshape(-1)]
    if rope is not None:
        in_specs += [pl.BlockSpec((t_len, dk), lambda b, h: (0, 0))] * 2
        args += list(rope)
    if init is not None:
        in_specs.append(pl.BlockSpec((1, 2, 1, dk, dv), lambda b, h: (b, 0, h, 0, 0)))
        args.append(init)
    out_shape = [jax.ShapeDtypeStruct((m, nh * dv), F32)]
    out_specs = [pl.BlockSpec((t_len, dv), lambda b, h: (b, h))]
    if emit_state:
        out_shape.append(jax.ShapeDtypeStruct((nb, 2, nh, dk, dv), F32))
        out_specs.append(pl.BlockSpec((1, 2, 1, dk, dv), lambda b, h: (b, 0, h, 0, 0)))
    return pl.pallas_call(
        functools.partial(_retention_kernel, t_len, rope is not None, init is not None, emit_state),
        name="retention",
        grid=(nb, nh),
        in_specs=in_specs,
        out_specs=out_specs,
        out_shape=out_shape,
        scratch_shapes=[pltpu.VMEM((t_len, dv), F32), pltpu.VMEM((t_len, dv), F32),
                        pltpu.VMEM((t_len, dk), F32), pltpu.VMEM((t_len, dk), F32)],
        compiler_params=_cparams(("parallel", "parallel")),
    )(*args)


def _rope_tables(t_len, dk):
    quarter = dk // 4
    t_idx = jnp.arange(t_len)
    row = (t_idx // GRID_W).astype(F32)
    colp = (t_idx % GRID_W).astype(F32)
    freq = ROPE_BASE ** (-jnp.arange(quarter, dtype=F32) / quarter)
    ar = row[:, None] * freq[None]
    ac = colp[:, None] * freq[None]
    cos = jnp.concatenate([jnp.cos(ar), jnp.cos(ar), jnp.cos(ac), jnp.cos(ac)], axis=1)
    sin = jnp.concatenate([-jnp.sin(ar), jnp.sin(ar), -jnp.sin(ac), jnp.sin(ac)], axis=1)
    return cos, sin


def _l2n(x):
    return x * lax.rsqrt(jnp.sum(x * x, axis=-1, keepdims=True) + EPS)


_INV_BASE = 8


def _unit_triangular_inverse(neg_a, row, col):
    ln = neg_a.shape[0]

    def same_block(size):
        return (row // size) == (col // size)

    p = jnp.where(same_block(_INV_BASE), neg_a, 0.0)
    inv = jnp.where(row == col, 1.0, 0.0) + p
    for _ in range(int(math.log2(_INV_BASE)) - 1):
        p = _bdot(p, p)
        inv = inv + _bdot(inv, p)
    size = _INV_BASE
    while size < ln:
        off = jnp.where(same_block(2 * size) & jnp.logical_not(same_block(size)), neg_a, 0.0)
        inv = inv + _bdot(_bdot(inv, off), inv)
        size *= 2
    return inv


def _deltanet_kernel(t_len, has_init, emit_state, *refs):
    (q_ref, k_ref, v_ref, z_ref, a0_ref, a1_ref, b0_ref, b1_ref, cw_ref,
     alog_ref, dtb_ref, ng_ref) = refs[:12]
    pos = 12
    if has_init:
        s0_ref = refs[pos]
        pos += 1
    y_ref = refs[pos]
    pos += 1
    if emit_state:
        sf_ref = refs[pos]
        pos += 1
    of_scr, ob_scr, q_scr, k_scr, v_scr = refs[pos:pos + 5]

    ln = DN_CHUNK
    nch = t_len // ln
    dk = q_ref.shape[1]
    dv = v_ref.shape[1]
    h_id = pl.program_id(1)
    nh = pl.num_programs(1)
    row, col = _chunk_masks(ln)
    eye = row == col
    o_scrs = (of_scr, ob_scr)
    a_refs = (a0_ref, a1_ref)
    b_refs = (b0_ref, b1_ref)
    masks = [_dir_masks(row, col, d) for d in range(2)]

    trow = lax.broadcasted_iota(jnp.int32, (t_len, 1), 0)

    def conv_silu(x_ref, wi):
        x = x_ref[...]
        w = cw_ref[0, wi]
        prev = jnp.where(trow == 0, 0.0, pltpu.roll(x, 1, 0))
        nxt = jnp.where(trow == t_len - 1, 0.0, pltpu.roll(x, t_len - 1, 0))
        return _silu(prev * w[0:1, :] + x * w[1:2, :] + nxt * w[2:3, :])

    q_scr[...] = _l2n(conv_silu(q_ref, 0)) * (dk ** -0.5)
    k_scr[...] = _l2n(conv_silu(k_ref, 1))
    v_scr[...] = conv_silu(v_ref, 2)

    def chunk_step(d, n, s_st):
        incl, incl_t, strict = masks[d]
        r0 = n * ln
        q = q_scr[r0:r0 + ln, :]
        k = k_scr[r0:r0 + ln, :]
        v = v_scr[r0:r0 + ln, :]
        a_row = a_refs[d][0, 0, n:n + 1, :]
        beta_row = _sigmoid(b_refs[d][0, 0, n:n + 1, :])
        neg_rate = -jnp.exp(jnp.zeros((1, 1), F32) + alog_ref[d * nh + h_id])
        g_row = neg_rate * _softplus(a_row + dtb_ref[d * nh + h_id])
        beta = _to_col(eye, beta_row)
        gc_col, gc_row = _cumsum_forms(g_row, eye, incl, incl_t)
        decay = jnp.exp(jnp.where(incl, gc_col - gc_row, -jnp.inf))
        kb = k * beta
        neg_a = jnp.where(strict, -(_bdot_nt(kb, k) * decay), 0.0)
        t_inv = _unit_triangular_inverse(neg_a, row, col)
        r = _bdot(t_inv, jnp.concatenate([v * beta, kb * jnp.exp(gc_col)], axis=1))
        u = r[:, :dv]
        w = r[:, dv:]
        v_new = u - _bdot(w, s_st)
        attn = _bdot_nt(q, k) * decay
        o_scrs[d][r0:r0 + ln, :] = _bdot(q * jnp.exp(gc_col), s_st) + _bdot(attn, v_new)
        g_last = jnp.sum(g_row, axis=1, keepdims=True)
        return s_st * jnp.exp(g_last) + _bdot_tn(k * jnp.exp(g_last - gc_col), v_new)

    if has_init:
        fin = [s0_ref[0, 0, 0], s0_ref[0, 1, 0]]
    else:
        fin = [jnp.zeros((dk, dv), F32), jnp.zeros((dk, dv), F32)]
    for i in range(nch):
        fin[0] = chunk_step(0, i, fin[0])
        fin[1] = chunk_step(1, nch - 1 - i, fin[1])
    if emit_state:
        sf_ref[0, 0, 0] = fin[0]
        sf_ref[0, 1, 0] = fin[1]
    o = _rms_rows(of_scr[...] + ob_scr[...]) * ng_ref[...]
    y_ref[...] = o * _silu(z_ref[...])


def _deltanet(main, gr, cols, conv_w, a_log, dt_bias, norm_g, nb, t_len, init, emit_state):
    m = main.shape[0]
    nh = DN_HEADS
    dk = norm_g.shape[1]
    dv = dk
    nch = t_len // DN_CHUNK
    gr3 = gr.reshape(gr.shape[0], nb, nch, DN_CHUNK)
    cw = conv_w.reshape(conv_w.shape[0], 3, nh, dk).transpose(2, 1, 0, 3)

    def colspec(off):
        return pl.BlockSpec((t_len, dk), lambda b, h: (b, off // dk + h))

    def gspec(base):
        return pl.BlockSpec((1, 1, nch, DN_CHUNK), lambda b, h: (base + h, b, 0, 0))

    smem = pl.BlockSpec(memory_space=pltpu.SMEM)
    in_specs = [colspec(cols['q']), colspec(cols['k']), colspec(cols['v']), colspec(cols['z']),
                gspec(0), gspec(nh), gspec(2 * nh), gspec(3 * nh),
                pl.BlockSpec((1, 3, 3, dk), lambda b, h: (h, 0, 0, 0)), smem, smem,
                pl.BlockSpec((1, dv), lambda b, h: (0, 0))]
    args = [main, main, main, main, gr3, gr3, gr3, gr3, cw,
            a_log.reshape(-1), dt_bias.reshape(-1), norm_g]
    if init is not None:
        in_specs.append(pl.BlockSpec((1, 2, 1, dk, dv), lambda b, h: (b, 0, h, 0, 0)))
        args.append(init)
    out_shape = [jax.ShapeDtypeStruct((m, nh * dv), F32)]
    out_specs = [pl.BlockSpec((t_len, dv), lambda b, h: (b, h))]
    if emit_state:
        out_shape.append(jax.ShapeDtypeStruct((nb, 2, nh, dk, dv), F32))
        out_specs.append(pl.BlockSpec((1, 2, 1, dk, dv), lambda b, h: (b, 0, h, 0, 0)))
    return pl.pallas_call(
        functools.partial(_deltanet_kernel, t_len, init is not None, emit_state), name="deltanet",
        grid=(nb, nh),
        in_specs=in_specs,
        out_specs=out_specs,
        out_shape=out_shape,
        scratch_shapes=[pltpu.VMEM((t_len, dv), F32), pltpu.VMEM((t_len, dv), F32),
                        pltpu.VMEM((t_len, dk), F32), pltpu.VMEM((t_len, dk), F32),
                        pltpu.VMEM((t_len, dv), F32)],
        compiler_params=_cparams(("parallel", "parallel")),
    )(*args)


def _pack_cols(w, sizes, order, gate_names):
    offs = {}
    o = 0
    for name, sz in sizes:
        offs[name] = (o, sz)
        o += sz
    main = jnp.concatenate([w[:, offs[n][0]:offs[n][0] + offs[n][1]] for n in order], axis=1)
    gates = jnp.concatenate([w[:, offs[n][0]:offs[n][0] + offs[n][1]] for n in gate_names], axis=1)
    return main.astype(BF16), gates.T.astype(BF16)


def kernel(x_prompt, x_sample, state_s5, state_ml_c, state_ml_n, state_ml_m, state_ret, state_dn,
           c, c_ctx, mod_w, mod_b, norm_pre, norm_post,
           ab_w_in, ab_w_out, s5_lambda_re, s5_lambda_im, s5_log_step, s5_b_re, s5_b_im,
           s5_c_re, s5_c_im, s5_d, s5_w_glu, ml_i_bias, ml_f_bias, ml_norm,
           cd_w_in, cd_w_out, ret_decay_logit, dn_conv, dn_a_log, dn_dt_bias, dn_norm):
    bp, tp, d = x_prompt.shape
    bs, ts, _ = x_sample.shape
    depth = mod_w.shape[0]
    br = s5_d.shape[1]
    g_s5 = s5_lambda_re.shape[2]
    ml_dv = br // ML_HEADS
    ml_dk = ml_dv // 2
    ret_dv = br // RET_HEADS
    ret_dk = ret_dv // 2
    dn_dk = br // DN_HEADS
    assert tp == SEG and ts % SEG == 0 and (bs * ts // SEG) == SUBLANES and bp % SUBLANES == 0
    nseg = ts // SEG

    cond = jnp.zeros((SUBLANES, d), F32).at[0].set(c_ctx).at[1:1 + bs].set(c)
    mod = _modulation(cond, mod_w, mod_b)
    mod = mod.reshape(depth, SUBLANES, 3, d)
    mod = mod.at[:, :, 1].add(1.0)
    mod = mod[:, :, jnp.array([1, 0, 2])]

    ab_sizes = (('u', br), ('za', br), ('q', ML_HEADS * ml_dk), ('k', ML_HEADS * ml_dk), ('v', br),
                ('o', br), ('ig', 2 * ML_HEADS), ('fg', 2 * ML_HEADS), ('zb', br))
    ab_order = ('u', 'za', 'q', 'k', 'v', 'o', 'zb')
    ab_cols = {'za': 0, 'q': br, 'k': br + ML_HEADS * ml_dk, 'v': br + 2 * ML_HEADS * ml_dk,
               'o': 2 * br + 2 * ML_HEADS * ml_dk, 'z': 3 * br + 2 * ML_HEADS * ml_dk}
    dn_qkv = 3 * br
    cd_sizes = (('rq', RET_HEADS * ret_dk), ('rk', RET_HEADS * ret_dk), ('rv', br), ('zc', br),
                ('qkv', dn_qkv), ('a', 2 * DN_HEADS), ('b', 2 * DN_HEADS), ('zd', br))
    cd_order = ('rq', 'rk', 'rv', 'zc', 'qkv', 'zd')
    ret_cols = {'q': 0, 'k': RET_HEADS * ret_dk, 'v': 2 * RET_HEADS * ret_dk,
                'z': 2 * RET_HEADS * ret_dk + br}
    dn_base = 2 * RET_HEADS * ret_dk + 2 * br
    dn_cols = {'q': dn_base, 'k': dn_base + br, 'v': dn_base + 2 * br, 'z': dn_base + 3 * br}

    rope = _rope_tables(ts, ret_dk)

    xp = x_prompt.reshape(bp * tp, d)
    xs = x_sample.reshape(bs * ts, d)
    new_s5, new_mc, new_mn, new_mm, new_ret, new_dn = [], [], [], [], [], []
    for l in range(depth):
        j = l // 2
        mod_p = mod[l, 0:1]
        mod_s = mod[l, 1:1 + bs]
        g_pre = norm_pre[l][None]
        g_post = norm_post[l][None]
        if l % 2 == 0:
            w_main, w_gt = _pack_cols(ab_w_in[j], ab_sizes, ab_order, ('ig', 'fg'))
            bw, cw, acoef = _s5_weights(s5_lambda_re[j], s5_lambda_im[j], s5_log_step[j],
                                        s5_b_re[j], s5_b_im[j], s5_c_re[j], s5_c_im[j])
            w_glu = s5_w_glu[j].astype(BF16)
            w_out = ab_w_out[j].astype(BF16)
            outs = []
            for (x2, nb, t_len, md, is_sample) in ((xp, bp, tp, mod_p, False), (xs, bs, ts, mod_s, True)):
                rpm = x2.shape[0] // md.shape[0]
                u_tm, main, gr = _inproj(x2, md[:, 0:2], rpm, g_pre, w_main, w_gt, br)
                nrow = x2.shape[0] // SEG
                if is_sample:
                    st = state_s5[:, j].astype(F32)
                    x0 = jnp.zeros((bs, nseg, 2, 2, g_s5, S5_P), F32)
                    x0 = x0.at[:, 0, 0].set(st[:, 0]).at[:, nseg - 1, 1].set(st[:, 1])
                    x0 = _s5_state_to_rows(x0.reshape(bs * nseg, 2, 2, g_s5, S5_P))
                else:
                    x0 = jnp.zeros((nrow, 4 * g_s5 * S5_P), F32)
                y_tm, xf = _s5_scan(u_tm.reshape(SEG, nrow, br), bw, cw, acoef, x0,
                                    nseg if is_sample else 1)
                ya = _s5_glu(y_tm.reshape(SEG, nrow * br), u_tm, main, s5_d[j][None], w_glu)
                init = None
                if is_sample:
                    init = (state_ml_c[:, j].astype(F32), state_ml_n[:, j].astype(F32),
                            state_ml_m[:, j].astype(F32))
                res = _mlstm(main, gr, ab_cols, ml_i_bias[j], ml_f_bias[j], ml_norm[j][None],
                             nb, t_len, init, not is_sample)
                yb = res[0]
                if not is_sample:
                    new_s5.append(_s5_rows_to_state(xf, g_s5, S5_P))
                    new_mc.append(res[1])
                    new_mn.append(res[2][:, :, :, 0, :])
                    new_mm.append(res[3][:, :, :, 0, 0])
                outs.append(_outproj(ya, yb, w_out, x2, md, rpm, g_post))
            xp, xs = outs
        else:
            w_main, w_gt = _pack_cols(cd_w_in[j], cd_sizes, cd_order, ('a', 'b'))
            w_out = cd_w_out[j].astype(BF16)
            outs = []
            for (x2, nb, t_len, md, is_sample) in ((xp, bp, tp, mod_p, False), (xs, bs, ts, mod_s, True)):
                rpm = x2.shape[0] // md.shape[0]
                main, gr = _inproj(x2, md[:, 0:2], rpm, g_pre, w_main, w_gt, 0)
                res_c = _retention(main, ret_cols, ret_decay_logit[j], nb, t_len,
                                   rope if is_sample else None,
                                   state_ret[:, j].astype(F32) if is_sample else None, not is_sample)
                res_d = _deltanet(main, gr, dn_cols, dn_conv[j], dn_a_log[j], dn_dt_bias[j],
                                  dn_norm[j][None], nb, t_len,
                                  state_dn[:, j].astype(F32) if is_sample else None, not is_sample)
                if not is_sample:
                    new_ret.append(res_c[1])
                    new_dn.append(res_d[1])
                outs.append(_outproj(res_c[0], res_d[0], w_out, x2, md, rpm, g_post))
            xp, xs = outs
    return (xp.reshape(bp, tp, d), xs.reshape(bs, ts, d), jnp.stack(new_s5, 1), jnp.stack(new_mc, 1),
            jnp.stack(new_mn, 1), jnp.stack(new_mm, 1), jnp.stack(new_ret, 1), jnp.stack(new_dn, 1))
```

```python
import functools
import math

import jax
import jax.numpy as jnp
from jax import lax
from jax.experimental import pallas as pl
from jax.experimental.pallas import tpu as pltpu

F32 = jnp.float32
BF16 = jnp.bfloat16

EPS = 1e-6
GRID_W = 64
ROPE_BASE = 10000.0
S5_GROUP = 16
S5_P = 64
ML_HEADS = 4
RET_HEADS = 4
DN_HEADS = 8
LANES = 128
SUBLANES = 8
SEG = 256
ML_CHUNK = 256
DN_CHUNK = 256
VMEM_LIMIT = 56 * 1024 * 1024


def _cparams(sem):
    return pltpu.CompilerParams(dimension_semantics=sem, vmem_limit_bytes=VMEM_LIMIT)


def _bdot(a, b):
    return jnp.dot(a.astype(BF16), b.astype(BF16), preferred_element_type=F32)


def _bdot_nt(a, b):
    return lax.dot_general(a.astype(BF16), b.astype(BF16), (((1,), (1,)), ((), ())),
                           preferred_element_type=F32)


def _bdot_tn(a, b):
    return lax.dot_general(a.astype(BF16), b.astype(BF16), (((0,), (0,)), ((), ())),
                           preferred_element_type=F32)


def _sigmoid(x):
    return 1.0 / (1.0 + jnp.exp(-x))


def _silu(x):
    return x * _sigmoid(x)


def _softplus(x):
    return jnp.maximum(x, 0.0) + jnp.log(1.0 + jnp.exp(-jnp.abs(x)))


def _log_sigmoid(x):
    return -_softplus(-x)


def _gelu_tanh(x):
    return 0.5 * x * (1.0 + jnp.tanh(math.sqrt(2.0 / math.pi) * (x + 0.044715 * (x * x * x))))


def _rms_rows(x):
    return x * lax.rsqrt(jnp.mean(x * x, axis=-1, keepdims=True) + EPS)


def _mod_kernel(cond_ref, w_ref, b_ref, o_ref):
    o_ref[0] = _bdot(_silu(cond_ref[...]), w_ref[0]) + b_ref[0]


def _modulation(cond, mod_w, mod_b):
    depth, d, n3 = mod_w.shape
    tn = 1024
    return pl.pallas_call(
        _mod_kernel, name="modulation",
        grid=(depth, n3 // tn),
        in_specs=[pl.BlockSpec((SUBLANES, d), lambda l, n: (0, 0)),
                  pl.BlockSpec((1, d, tn), lambda l, n: (l, 0, n)),
                  pl.BlockSpec((1, 1, tn), lambda l, n: (l, 0, n))],
        out_specs=pl.BlockSpec((1, SUBLANES, tn), lambda l, n: (l, 0, n)),
        out_shape=jax.ShapeDtypeStruct((depth, SUBLANES, n3), F32),
        compiler_params=_cparams(("parallel", "parallel")),
    )(cond, mod_w, mod_b.reshape(depth, 1, n3))


def _inproj_kernel(n_tm, x_ref, mod_ref, g_ref, w_ref, wgt_ref, *out_refs):
    x = x_ref[...]
    h = _rms_rows(x) * g_ref[...]
    h = (h * mod_ref[0, 0:1, :] + mod_ref[0, 1:2, :]).astype(BF16)
    tm_ref = out_refs[0] if n_tm else None
    main_ref, gr_ref = out_refs[-2], out_refs[-1]
    step = 1024
    n_main = main_ref.shape[1]
    for n0 in range(0, n_tm, step):
        tm_ref[:, n0:n0 + step] = jnp.dot(h, w_ref[:, n0:n0 + step], preferred_element_type=F32)
    for n0 in range(0, n_main, step):
        main_ref[:, n0:n0 + step] = jnp.dot(h, w_ref[:, n_tm + n0:n_tm + n0 + step],
                                            preferred_element_type=F32)
    gr_ref[...] = lax.dot_general(wgt_ref[...], h, (((1,), (1,)), ((), ())),
                                  preferred_element_type=F32)


def _inproj(x2, mod, rows_per_mod, gain, w, wgt, n_tm):
    m, d = x2.shape
    n = w.shape[1]
    ng = wgt.shape[0]
    nt = m // SEG
    tiles_per_mod = rows_per_mod // SEG
    out_shape, out_specs = [], []
    if n_tm:
        out_shape.append(jax.ShapeDtypeStruct((SEG, nt * n_tm), F32))
        out_specs.append(pl.BlockSpec((SEG, n_tm), lambda i: (0, i)))
    out_shape += [jax.ShapeDtypeStruct((m, n - n_tm), F32), jax.ShapeDtypeStruct((ng, m), F32)]
    out_specs += [pl.BlockSpec((SEG, n - n_tm), lambda i: (i, 0)),
                  pl.BlockSpec((ng, SEG), lambda i: (0, i))]
    return pl.pallas_call(
        functools.partial(_inproj_kernel, n_tm), name="inproj",
        grid=(nt,),
        in_specs=[pl.BlockSpec((SEG, d), lambda i: (i, 0)),
                  pl.BlockSpec((1, 2, d), lambda i: (i // tiles_per_mod, 0, 0)),
                  pl.BlockSpec((1, d), lambda i: (0, 0)),
                  pl.BlockSpec((d, n), lambda i: (0, 0), pipeline_mode=pl.Buffered(1)),
                  pl.BlockSpec((ng, d), lambda i: (0, 0))],
        out_specs=out_specs,
        out_shape=out_shape,
        compiler_params=_cparams(("parallel",)),
    )(x2, mod, gain, w, wgt)


def _outproj_kernel(ya_ref, yb_ref, w_ref, x_ref, mod_ref, g_ref, o_ref):
    half = ya_ref.shape[1]
    acc = (jnp.dot(ya_ref[...].astype(BF16), w_ref[0:half, :], preferred_element_type=F32)
           + jnp.dot(yb_ref[...].astype(BF16), w_ref[half:2 * half, :], preferred_element_type=F32))
    o_ref[...] = x_ref[...] + mod_ref[0, 2:3, :] * (_rms_rows(acc) * g_ref[...])


def _outproj(ya, yb, w, x2, mod, rows_per_mod, gain):
    m, d = x2.shape
    br = ya.shape[1]
    tiles_per_mod = rows_per_mod // SEG
    return pl.pallas_call(
        _outproj_kernel, name="outproj",
        grid=(m // SEG,),
        in_specs=[pl.BlockSpec((SEG, br), lambda i: (i, 0)),
                  pl.BlockSpec((SEG, br), lambda i: (i, 0)),
                  pl.BlockSpec((2 * br, d), lambda i: (0, 0), pipeline_mode=pl.Buffered(1)),
                  pl.BlockSpec((SEG, d), lambda i: (i, 0)),
                  pl.BlockSpec((1, 3, d), lambda i: (i // tiles_per_mod, 0, 0)),
                  pl.BlockSpec((1, d), lambda i: (0, 0))],
        out_specs=pl.BlockSpec((SEG, d), lambda i: (i, 0)),
        out_shape=jax.ShapeDtypeStruct((m, d), F32),
        compiler_params=_cparams(("parallel",)),
    )(ya, yb, w, x2, mod, gain)


def _cmul(ar, ai, xr, xi):
    return ar * xr - ai * xi, ar * xi + ai * xr


def _s5_kernel(t_len, nseg, u_ref, bw_ref, cw_ref, a_ref, x0_ref, y_ref, xf_ref, bu_scr):
    hp = xf_ref.shape[1] // 4
    u = u_ref[...].reshape(t_len * SUBLANES, LANES)
    bu_scr[...] = _bdot(u, bw_ref[0])
    a = a_ref[0]
    coef = [jnp.broadcast_to(a[r:r + 1, :], (SUBLANES, hp)) for r in range(4)]

    def scan(init, store):
        def step(t, carry):
            xfr, xfi, xbr, xbi = carry
            rf = pl.multiple_of(t * SUBLANES, SUBLANES)
            rb = pl.multiple_of((t_len - 1 - t) * SUBLANES, SUBLANES)
            pr, pi = _cmul(coef[0], coef[1], xfr, xfi)
            nfr = pr + bu_scr[pl.ds(rf, SUBLANES), 0:hp]
            nfi = pi + bu_scr[pl.ds(rf, SUBLANES), hp:2 * hp]
            pr, pi = _cmul(coef[2], coef[3], xbr, xbi)
            nbr = pr + bu_scr[pl.ds(rb, SUBLANES), 2 * hp:3 * hp]
            nbi = pi + bu_scr[pl.ds(rb, SUBLANES), 3 * hp:4 * hp]
            if store:
                bu_scr[pl.ds(rf, SUBLANES), 0:hp] = nfr
                bu_scr[pl.ds(rf, SUBLANES), hp:2 * hp] = nfi
                bu_scr[pl.ds(rb, SUBLANES), 2 * hp:3 * hp] = nbr
                bu_scr[pl.ds(rb, SUBLANES), 3 * hp:4 * hp] = nbi
            return nfr, nfi, nbr, nbi
        return lax.fori_loop(0, t_len, step, init, unroll=4)

    x0 = tuple(x0_ref[:, r * hp:(r + 1) * hp] for r in range(4))
    if nseg > 1:
        zero = jnp.zeros((SUBLANES, hp), F32)
        ffr, ffi, fbr, fbi = scan((zero, zero, zero, zero), False)
        pfr, pfi, pbr, pbi = coef
        for _ in range(int(math.log2(t_len))):
            pfr, pfi = _cmul(pfr, pfi, pfr, pfi)
            pbr, pbi = _cmul(pbr, pbi, pbr, pbi)
        seg = lax.broadcasted_iota(jnp.int32, (SUBLANES, hp), 0) % nseg
        sfr, sfi, sbr, sbi = x0
        for _ in range(nseg - 1):
            tr, ti = _cmul(pfr, pfi, sfr, sfi)
            sfr = x0[0] + jnp.where(seg >= 1, pltpu.roll(tr + ffr, 1, 0), 0.0)
            sfi = x0[1] + jnp.where(seg >= 1, pltpu.roll(ti + ffi, 1, 0), 0.0)
            tr, ti = _cmul(pbr, pbi, sbr, sbi)
            sbr = x0[2] + jnp.where(seg <= nseg - 2, pltpu.roll(tr + fbr, SUBLANES - 1, 0), 0.0)
            sbi = x0[3] + jnp.where(seg <= nseg - 2, pltpu.roll(ti + fbi, SUBLANES - 1, 0), 0.0)
        x0 = (sfr, sfi, sbr, sbi)
    fin = scan(x0, True)
    for r in range(4):
        xf_ref[:, r * hp:(r + 1) * hp] = fin[r]
    y = _bdot(bu_scr[...], cw_ref[0])
    y_ref[...] = y.reshape(t_len, SUBLANES, LANES)


def _s5_scan(u_tm, bw, cw, acoef, x0, nseg):
    t_len, nb, br = u_tm.shape
    nj = br // LANES
    sw = bw.shape[2]
    assert t_len & (t_len - 1) == 0
    return pl.pallas_call(
        functools.partial(_s5_kernel, t_len, nseg), name="s5_scan",
        grid=(nb // SUBLANES, nj),
        in_specs=[pl.BlockSpec((t_len, SUBLANES, LANES), lambda i, j: (0, i, j)),
                  pl.BlockSpec((1, LANES, sw), lambda i, j: (j, 0, 0)),
                  pl.BlockSpec((1, sw, LANES), lambda i, j: (j, 0, 0)),
                  pl.BlockSpec((1, 4, sw // 4), lambda i, j: (j, 0, 0)),
                  pl.BlockSpec((SUBLANES, sw), lambda i, j: (i, j))],
        out_specs=[pl.BlockSpec((t_len, SUBLANES, LANES), lambda i, j: (0, i, j)),
                   pl.BlockSpec((SUBLANES, sw), lambda i, j: (i, j))],
        out_shape=[jax.ShapeDtypeStruct((t_len, nb, br), F32),
                   jax.ShapeDtypeStruct((nb, nj * sw), F32)],
        scratch_shapes=[pltpu.VMEM((t_len * SUBLANES, sw), F32)],
        compiler_params=_cparams(("parallel", "parallel")),
    )(u_tm, bw, cw, acoef, x0)


def _s5_glu_kernel(y_ref, u_ref, z_ref, d_ref, w_ref, o_ref):
    y = y_ref[...] + d_ref[...] * u_ref[...]
    g = _gelu_tanh(y)
    o_ref[...] = g * _sigmoid(_bdot(g, w_ref[...])) * _silu(z_ref[...])


def _s5_glu(y_tm2, u_tm2, main, d_skip, w_glu):
    br = d_skip.shape[1]
    nt = y_tm2.shape[1] // br
    return pl.pallas_call(
        _s5_glu_kernel, name="s5_glu",
        grid=(nt,),
        in_specs=[pl.BlockSpec((SEG, br), lambda i: (0, i)),
                  pl.BlockSpec((SEG, br), lambda i: (0, i)),
                  pl.BlockSpec((SEG, br), lambda i: (i, 0)),
                  pl.BlockSpec((1, br), lambda i: (0, 0)),
                  pl.BlockSpec((br, br), lambda i: (0, 0))],
        out_specs=pl.BlockSpec((SEG, br), lambda i: (i, 0)),
        out_shape=jax.ShapeDtypeStruct((nt * SEG, br), F32),
        compiler_params=_cparams(("parallel",)),
    )(y_tm2, u_tm2, main, d_skip, w_glu)


def _s5_weights(lam_re, lam_im, log_step, b_re, b_im, c_re, c_im):
    lr, li = lam_re.astype(F32), lam_im.astype(F32)
    step = jnp.exp(log_step.astype(F32))[..., None]
    mag = jnp.exp(lr * step)
    a_r, a_i = mag * jnp.cos(li * step), mag * jnp.sin(li * step)
    den = lr * lr + li * li
    c_r = ((a_r - 1.0) * lr + a_i * li) / den
    c_i = (a_i * lr - (a_r - 1.0) * li) / den
    br_, bi_ = b_re.astype(F32), b_im.astype(F32)
    bbar_r = c_r[..., None] * br_ - c_i[..., None] * bi_
    bbar_i = c_r[..., None] * bi_ + c_i[..., None] * br_
    g = lr.shape[1]
    gl = LANES // S5_GROUP
    nj = g // gl
    eye = jnp.eye(gl, dtype=F32)

    def bw_part(x):
        x = x.reshape(nj, gl, S5_P, S5_GROUP)
        return jnp.einsum('ab,jbps->jasbp', eye, x).reshape(nj, LANES, gl * S5_P)

    def cw_part(x):
        x = x.reshape(nj, gl, S5_GROUP, S5_P)
        return jnp.einsum('ab,jbsp->jbpas', eye, x).reshape(nj, gl * S5_P, LANES)

    bw = jnp.concatenate([bw_part(bbar_r[0]), bw_part(bbar_i[0]),
                          bw_part(bbar_r[1]), bw_part(bbar_i[1])], axis=2)
    cw = jnp.concatenate([cw_part(c_re[0].astype(F32)), cw_part(-c_im[0].astype(F32)),
                          cw_part(c_re[1].astype(F32)), cw_part(-c_im[1].astype(F32))], axis=1)
    acoef = jnp.stack([a_r[0], a_i[0], a_r[1], a_i[1]], axis=0)
    acoef = acoef.reshape(4, nj, gl * S5_P).transpose(1, 0, 2)
    return bw.astype(BF16), cw.astype(BF16), acoef


def _s5_state_to_rows(st):
    nb, _, _, g, p = st.shape
    gl = LANES // S5_GROUP
    nj = g // gl
    return st.reshape(nb, 4, nj, gl * p).transpose(0, 2, 1, 3).reshape(nb, nj * 4 * gl * p)


def _s5_rows_to_state(rows, g, p):
    nb = rows.shape[0]
    gl = LANES // S5_GROUP
    nj = g // gl
    return rows.reshape(nb, nj, 4, gl * p).transpose(0, 2, 1, 3).reshape(nb, 2, 2, g, p)


def _chunk_masks(length):
    row = lax.broadcasted_iota(jnp.int32, (length, length), 0)
    col = lax.broadcasted_iota(jnp.int32, (length, length), 1)
    return row, col


def _to_col(eye, x_row):
    return jnp.sum(jnp.where(eye, x_row, 0.0), axis=1, keepdims=True)


def _cumsum_forms(x_row, eye, incl, incl_t):
    x_col = _to_col(eye, x_row)
    c_col = jnp.sum(jnp.where(incl, x_row, 0.0), axis=1, keepdims=True)
    c_row = jnp.sum(jnp.where(incl_t, x_col, 0.0), axis=0, keepdims=True)
    return c_col, c_row


def _dir_masks(row, col, d):
    if d == 0:
        return col <= row, row <= col, col < row
    return col >= row, row >= col, col > row


def _chunk_loop(nch, body, init):
    return lax.fori_loop(0, nch, body, init, unroll=nch <= 2)


def _mlstm_kernel(t_len, hpb, has_init, emit_state, *refs):
    (q_ref, k_ref, v_ref, o_ref, z_ref, ig0_ref, ig1_ref, fg0_ref, fg1_ref,
     ib_ref, fb_ref, ng_ref) = refs[:12]
    pos = 12
    if has_init:
        c0_ref, n0_ref, m0_ref = refs[pos:pos + 3]
        pos += 3
    y_ref = refs[pos]
    pos += 1
    if emit_state:
        cf_ref, nf_ref, mf_ref = refs[pos:pos + 3]
        pos += 3
    hf_scr, hb_scr = refs[pos:pos + 2]

    ln = ML_CHUNK
    nch = t_len // ln
    dk = q_ref.shape[1] // hpb
    dv = v_ref.shape[1] // hpb
    b_id = pl.program_id(0)
    nh = pl.num_programs(1) * hpb
    h_base = pl.program_id(1) * hpb
    row, col = _chunk_masks(ln)
    eye = row == col
    scale = dk ** -0.5
    ig_refs = (ig0_ref, ig1_ref)
    fg_refs = (fg0_ref, fg1_ref)
    h_scrs = (hf_scr, hb_scr)
    masks = [_dir_masks(row, col, d) for d in range(2)]

    def gate_terms(hh, d, n, m_st):
        incl, incl_t, _ = masks[d]
        i_row = ig_refs[d][hh, 0, pl.ds(n, 1), :] + ib_ref[d * nh + h_base + hh]
        lf_row = _log_sigmoid(fg_refs[d][hh, 0, pl.ds(n, 1), :] + fb_ref[d * nh + h_base + hh])
        i_col = _to_col(eye, i_row)
        b_col, b_row = _cumsum_forms(lf_row, eye, incl, incl_t)
        c_row = i_row - b_row
        c_col = i_col - b_col
        cm = jnp.where(incl, c_row, -jnp.inf)
        mu = jnp.maximum(m_st, jnp.max(cm, axis=1, keepdims=True))
        mu_last = jnp.maximum(m_st, jnp.max(c_row, axis=1, keepdims=True))
        b_last = jnp.sum(lf_row, axis=1, keepdims=True)
        return dict(e=jnp.exp(cm - mu), inter=jnp.exp(m_st - mu), floor=jnp.exp(-b_col - mu),
                    w=jnp.exp(c_col - mu_last), dec=jnp.exp(m_st - mu_last), m_new=b_last + mu_last)

    def load_qkv(hh, n):
        r0 = pl.multiple_of(n * ln, ln)
        return (r0, q_ref[pl.ds(r0, ln), hh * dk:(hh + 1) * dk],
                k_ref[pl.ds(r0, ln), hh * dk:(hh + 1) * dk] * scale,
                v_ref[pl.ds(r0, ln), hh * dv:(hh + 1) * dv])

    def finish(hh, d, r0, g, s, sv, q, qc, carry):
        num, den = sv, jnp.sum(s, axis=1, keepdims=True)
        if carry is not None:
            num = num + g['inter'] * qc
            den = den + g['inter'] * jnp.sum(q * carry[1], axis=1, keepdims=True)
        h_scrs[d][pl.ds(r0, ln), hh * dv:(hh + 1) * dv] = num / jnp.maximum(jnp.abs(den), g['floor'])

    def new_state(g, kw, kv, carry):
        n_add = jnp.sum(kw, axis=0, keepdims=True)
        if carry is None:
            return kv, n_add, g['m_new']
        return g['dec'] * carry[0] + kv, g['dec'] * carry[1] + n_add, g['m_new']

    def state_products(qs, carries):
        return [None if c is None else _bdot(q, c[0]) for q, c in zip(qs, carries)]

    def step_split(ns, carries):
        items = [(hh, d) for hh in range(hpb) for d in range(2)]
        loads = [load_qkv(hh, ns[d]) for hh, d in items]
        gs = [gate_terms(hh, d, ns[d], carries[c][2]) for c, (hh, d) in enumerate(items)]
        ss = [_bdot_nt(ld[1], ld[2]) * g['e'] for ld, g in zip(loads, gs)]
        svs = [_bdot(sm, ld[3]) for sm, ld in zip(ss, loads)]
        qcs = state_products([ld[1] for ld in loads], carries)
        kws = [ld[2] * g['w'] for ld, g in zip(loads, gs)]
        kvs = [_bdot_tn(kw, ld[3]) for kw, ld in zip(kws, loads)]
        out = []
        for c, (hh, d) in enumerate(items):
            finish(hh, d, loads[c][0], gs[c], ss[c], svs[c], loads[c][1], qcs[c], carries[c])
            out.append(new_state(gs[c], kws[c], kvs[c], carries[c]))
        return out

    def step_shared(carries):
        loads = [load_qkv(hh, 0) for hh in range(hpb)]
        zero_m = jnp.zeros((1, 1), F32)
        gs = [gate_terms(c // 2, c % 2, 0, zero_m if carries[c] is None else carries[c][2])
              for c in range(2 * hpb)]
        qks = [_bdot_nt(ld[1], ld[2]) for ld in loads]
        ss = [qks[c // 2] * gs[c]['e'] for c in range(2 * hpb)]
        svs = [_bdot(jnp.concatenate(ss[2 * hh:2 * hh + 2], axis=0), loads[hh][3])
               for hh in range(hpb)]
        qcs = state_products([loads[c // 2][1] for c in range(2 * hpb)], carries)
        kws = [loads[c // 2][2] * gs[c]['w'] for c in range(2 * hpb)]
        kvs = [_bdot_tn(jnp.concatenate(kws[2 * hh:2 * hh + 2], axis=1), loads[hh][3])
               for hh in range(hpb)]
        out = []
        for c in range(2 * hpb):
            hh, d = c // 2, c % 2
            finish(hh, d, loads[hh][0], gs[c], ss[c], svs[hh][d * ln:(d + 1) * ln], loads[hh][1],
                   qcs[c], carries[c])
            out.append(new_state(gs[c], kws[c], kvs[hh][d * dk:(d + 1) * dk], carries[c]))
        return out

    def body(i, carry):
        return tuple(step_split((i, nch - 1 - i), carry))

    init = []
    for hh in range(hpb):
        for d in range(2):
            if has_init:
                m0 = jnp.zeros((1, 1), F32) + m0_ref[(b_id * 2 + d) * nh + h_base + hh]
                init.append((c0_ref[0, d, hh], n0_ref[0, d, hh], m0))
            else:
                init.append(None)
    if nch == 1:
        fin = step_shared(init)
    else:
        zero_state = (jnp.zeros((dk, dv), F32), jnp.zeros((1, dk), F32), jnp.zeros((1, 1), F32))
        fin = _chunk_loop(nch, body, tuple(zero_state if c is None else c for c in init))
    if emit_state:
        for hh in range(hpb):
            for d in range(2):
                cf_ref[0, d, hh] = fin[hh * 2 + d][0]
                nf_ref[0, d, hh] = fin[hh * 2 + d][1]
                mf_ref[0, d, hh] = jnp.broadcast_to(fin[hh * 2 + d][2], (1, LANES))
    for hh in range(hpb):
        sl = slice(hh * dv, (hh + 1) * dv)
        h = _rms_rows(hf_scr[:, sl] + hb_scr[:, sl]) * ng_ref[:, sl]
        y_ref[:, sl] = _sigmoid(o_ref[:, sl]) * h * _silu(z_ref[:, sl])


def _mlstm(main, gr, cols, i_bias, f_bias, norm_g, nb, t_len, init, emit_state, hpb):
    m = main.shape[0]
    nh = ML_HEADS
    dv = norm_g.shape[1] // nh
    dk = dv // 2
    nch = t_len // ML_CHUNK
    gr3 = gr.reshape(gr.shape[0], nb, nch, ML_CHUNK)

    def colspec(width, off):
        return pl.BlockSpec((t_len, hpb * width), lambda b, h: (b, off // (hpb * width) + h))

    def gspec(base):
        return pl.BlockSpec((hpb, 1, nch, ML_CHUNK), lambda b, h: (base // hpb + h, b, 0, 0))

    smem = pl.BlockSpec(memory_space=pltpu.SMEM)
    in_specs = [colspec(dk, cols['q']), colspec(dk, cols['k']), colspec(dv, cols['v']),
                colspec(dv, cols['o']), colspec(dv, cols['z']),
                gspec(0), gspec(nh), gspec(2 * nh), gspec(3 * nh), smem, smem,
                pl.BlockSpec((1, hpb * dv), lambda b, h: (0, h))]
    args = [main, main, main, main, main, gr3, gr3, gr3, gr3,
            i_bias.reshape(-1), f_bias.reshape(-1), norm_g]
    if init is not None:
        c0, n0, m0 = init
        in_specs += [pl.BlockSpec((1, 2, hpb, dk, dv), lambda b, h: (b, 0, h, 0, 0)),
                     pl.BlockSpec((1, 2, hpb, 1, dk), lambda b, h: (b, 0, h, 0, 0)), smem]
        args += [c0, n0.reshape(nb, 2, nh, 1, dk), m0.reshape(-1)]
    out_shape = [jax.ShapeDtypeStruct((m, nh * dv), F32)]
    out_specs = [pl.BlockSpec((t_len, hpb * dv), lambda b, h: (b, h))]
    if emit_state:
        out_shape += [jax.ShapeDtypeStruct((nb, 2, nh, dk, dv), F32),
                      jax.ShapeDtypeStruct((nb, 2, nh, 1, dk), F32),
                      jax.ShapeDtypeStruct((nb, 2, nh, 1, LANES), F32)]
        out_specs += [pl.BlockSpec((1, 2, hpb, dk, dv), lambda b, h: (b, 0, h, 0, 0)),
                      pl.BlockSpec((1, 2, hpb, 1, dk), lambda b, h: (b, 0, h, 0, 0)),
                      pl.BlockSpec((1, 2, hpb, 1, LANES), lambda b, h: (b, 0, h, 0, 0))]
    return pl.pallas_call(
        functools.partial(_mlstm_kernel, t_len, hpb, init is not None, emit_state), name="mlstm",
        grid=(nb, nh // hpb),
        in_specs=in_specs,
        out_specs=out_specs,
        out_shape=out_shape,
        scratch_shapes=[pltpu.VMEM((t_len, hpb * dv), F32), pltpu.VMEM((t_len, hpb * dv), F32)],
        compiler_params=_cparams(("parallel", "parallel")),
    )(*args)


def _rope_swap(x):
    quarter = x.shape[1] // 4
    lane = lax.broadcasted_iota(jnp.int32, x.shape, 1)
    up = pltpu.roll(x, x.shape[1] - quarter, 1)
    down = pltpu.roll(x, quarter, 1)
    return jnp.where((lane // quarter) % 2 == 0, up, down)


def _retention_kernel(t_len, hpb, has_pos, has_init, emit_state, *refs):
    q_ref, k_ref, v_ref, z_ref, lg_ref = refs[:5]
    pos = 5
    if has_pos:
        cos_ref, sin_ref = refs[pos:pos + 2]
        pos += 2
    if has_init:
        r0_ref = refs[pos]
        pos += 1
    y_ref = refs[pos]
    pos += 1
    if emit_state:
        rf_ref = refs[pos]
        pos += 1
    of_scr, ob_scr, q_scr, k_scr = refs[pos:pos + 4]

    ln = ML_CHUNK
    nch = t_len // ln
    dk = q_ref.shape[1] // hpb
    dv = v_ref.shape[1] // hpb
    nh = pl.num_programs(1) * hpb
    h_base = pl.program_id(1) * hpb
    row, col = _chunk_masks(ln)
    scale = dk ** -0.5
    o_scrs = (of_scr, ob_scr)

    dist = jnp.abs(row - col).astype(F32)
    iota_col = lax.broadcasted_iota(jnp.int32, (ln, 1), 0).astype(F32)
    consts = []
    for hh in range(hpb):
        sl = slice(hh * dk, (hh + 1) * dk)
        q = q_ref[:, sl]
        k = k_ref[:, sl]
        if has_pos:
            q = q * cos_ref[...] + _rope_swap(q) * sin_ref[...]
            k = k * cos_ref[...] + _rope_swap(k) * sin_ref[...]
        q_scr[:, sl] = q
        k_scr[:, sl] = k * scale
        for d in range(2):
            incl = _dir_masks(row, col, d)[0]
            lg = _log_sigmoid(jnp.zeros((1, 1), F32) + lg_ref[d * nh + h_base + hh])
            order = iota_col if d == 0 else (ln - 1.0) - iota_col
            dmask = jnp.where(incl, jnp.exp(dist * lg), 0.0)
            consts.append((dmask, jnp.exp((order + 1.0) * lg), jnp.exp((ln - 1.0 - order) * lg),
                           jnp.exp(float(ln) * lg)))

    def load_qkv(hh, n):
        r0 = pl.multiple_of(n * ln, ln)
        return (r0, q_scr[pl.ds(r0, ln), hh * dk:(hh + 1) * dk],
                k_scr[pl.ds(r0, ln), hh * dk:(hh + 1) * dk],
                v_ref[pl.ds(r0, ln), hh * dv:(hh + 1) * dv])

    def step_split(ns, states):
        count = 2 * hpb
        loads = [load_qkv(c // 2, ns[c % 2]) for c in range(count)]
        ss = [_bdot_nt(ld[1], ld[2]) * consts[c][0] for c, ld in enumerate(loads)]
        svs = [_bdot(sm, ld[3]) for sm, ld in zip(ss, loads)]
        qrs = [_bdot(ld[1] * consts[c][1], states[c]) for c, ld in enumerate(loads)]
        kvs = [_bdot_tn(ld[2] * consts[c][2], ld[3]) for c, ld in enumerate(loads)]
        for c in range(count):
            hh, d = c // 2, c % 2
            o_scrs[d][pl.ds(loads[c][0], ln), hh * dv:(hh + 1) * dv] = svs[c] + qrs[c]
        return [consts[c][3] * states[c] + kvs[c] for c in range(count)]

    def step_shared(states):
        count = 2 * hpb
        loads = [load_qkv(hh, 0) for hh in range(hpb)]
        qks = [_bdot_nt(ld[1], ld[2]) for ld in loads]
        svs = [_bdot(jnp.concatenate([qks[hh] * consts[hh * 2 + d][0] for d in range(2)], axis=0),
                     loads[hh][3]) for hh in range(hpb)]
        kvs = [_bdot_tn(jnp.concatenate([loads[hh][2] * consts[hh * 2 + d][2] for d in range(2)],
                                        axis=1), loads[hh][3]) for hh in range(hpb)]
        qrs = [None if states[c] is None else _bdot(loads[c // 2][1] * consts[c][1], states[c])
               for c in range(count)]
        out = []
        for c in range(count):
            hh, d = c // 2, c % 2
            o = svs[hh][d * ln:(d + 1) * ln]
            r_new = kvs[hh][d * dk:(d + 1) * dk]
            if states[c] is not None:
                o = o + qrs[c]
                r_new = r_new + consts[c][3] * states[c]
            o_scrs[d][pl.ds(loads[hh][0], ln), hh * dv:(hh + 1) * dv] = o
            out.append(r_new)
        return out

    def body(i, carry):
        return tuple(step_split((i, nch - 1 - i), carry))

    if has_init:
        init = [r0_ref[0, d, hh] for hh in range(hpb) for d in range(2)]
    else:
        init = [None] * (2 * hpb)
    if nch == 1:
        fin = step_shared(init)
    else:
        fin = _chunk_loop(nch, body, tuple(jnp.zeros((dk, dv), F32) if c is None else c
                                           for c in init))
    for hh in range(hpb):
        if emit_state:
            rf_ref[0, 0, hh] = fin[hh * 2]
            rf_ref[0, 1, hh] = fin[hh * 2 + 1]
        sl = slice(hh * dv, (hh + 1) * dv)
        y_ref[:, sl] = _rms_rows(of_scr[:, sl] + ob_scr[:, sl]) * _silu(z_ref[:, sl])


def _retention(main, cols, decay_logit, nb, t_len, rope, init, emit_state, hpb):
    m = main.shape[0]
    nh = RET_HEADS
    dk = (cols['k'] - cols['q']) // nh
    dv = 2 * dk

    def colspec(width, off):
        return pl.BlockSpec((t_len, hpb * width), lambda b, h: (b, off // (hpb * width) + h))

    smem = pl.BlockSpec(memory_space=pltpu.SMEM)
    in_specs = [colspec(dk, cols['q']), colspec(dk, cols['k']), colspec(dv, cols['v']),
                colspec(dv, cols['z']), smem]
    args = [main, main, main, main, decay_logit.reshape(-1)]
    if rope is not None:
        in_specs += [pl.BlockSpec((t_len, dk), lambda b, h: (0, 0))] * 2
        args += list(rope)
    if init is not None:
        in_specs.append(pl.BlockSpec((1, 2, hpb, dk, dv), lambda b, h: (b, 0, h, 0, 0)))
        args.append(init)
    out_shape = [jax.ShapeDtypeStruct((m, nh * dv), F32)]
    out_specs = [pl.BlockSpec((t_len, hpb * dv), lambda b, h: (b, h))]
    if emit_state:
        out_shape.append(jax.ShapeDtypeStruct((nb, 2, nh, dk, dv), F32))
        out_specs.append(pl.BlockSpec((1, 2, hpb, dk, dv), lambda b, h: (b, 0, h, 0, 0)))
    return pl.pallas_call(
        functools.partial(_retention_kernel, t_len, hpb, rope is not None, init is not None,
                          emit_state),
        name="retention",
        grid=(nb, nh // hpb),
        in_specs=in_specs,
        out_specs=out_specs,
        out_shape=out_shape,
        scratch_shapes=[pltpu.VMEM((t_len, hpb * dv), F32), pltpu.VMEM((t_len, hpb * dv), F32),
                        pltpu.VMEM((t_len, hpb * dk), F32), pltpu.VMEM((t_len, hpb * dk), F32)],
        compiler_params=_cparams(("parallel", "parallel")),
    )(*args)


def _rope_tables(t_len, dk):
    quarter = dk // 4
    t_idx = jnp.arange(t_len)
    row = (t_idx // GRID_W).astype(F32)
    colp = (t_idx % GRID_W).astype(F32)
    freq = ROPE_BASE ** (-jnp.arange(quarter, dtype=F32) / quarter)
    ar = row[:, None] * freq[None]
    ac = colp[:, None] * freq[None]
    cos = jnp.concatenate([jnp.cos(ar), jnp.cos(ar), jnp.cos(ac), jnp.cos(ac)], axis=1)
    sin = jnp.concatenate([-jnp.sin(ar), jnp.sin(ar), -jnp.sin(ac), jnp.sin(ac)], axis=1)
    return cos, sin


def _l2n(x):
    return x * lax.rsqrt(jnp.sum(x * x, axis=-1, keepdims=True) + EPS)


_INV_BASE = 8


def _unit_triangular_inverses(neg_as, row, col):
    ln = neg_as[0].shape[0]
    count = len(neg_as)

    def same_block(size):
        return (row // size) == (col // size)

    base = same_block(_INV_BASE)
    ps = [jnp.where(base, a, 0.0) for a in neg_as]
    invs = [jnp.where(row == col, 1.0, 0.0) + p for p in ps]
    n_sq = int(math.log2(_INV_BASE)) - 1
    ps = [_bdot(p, p) for p in ps]
    for i in range(n_sq):
        if i == n_sq - 1:
            invs = [invs[c] + _bdot(invs[c], ps[c]) for c in range(count)]
        else:
            boths = [_bdot(jnp.concatenate([invs[c], ps[c]], axis=0), ps[c]) for c in range(count)]
            invs = [invs[c] + boths[c][:ln] for c in range(count)]
            ps = [boths[c][ln:] for c in range(count)]
    size = _INV_BASE
    while size < ln:
        sel = same_block(2 * size) & jnp.logical_not(same_block(size))
        mids = [_bdot(invs[c], jnp.where(sel, neg_as[c], 0.0)) for c in range(count)]
        invs = [invs[c] + _bdot(mids[c], invs[c]) for c in range(count)]
        size *= 2
    return invs


def _deltanet_kernel(t_len, hpb, has_init, emit_state, *refs):
    (q_ref, k_ref, v_ref, z_ref, a0_ref, a1_ref, b0_ref, b1_ref, cw_ref,
     alog_ref, dtb_ref, ng_ref) = refs[:12]
    pos = 12
    if has_init:
        s0_ref = refs[pos]
        pos += 1
    y_ref = refs[pos]
    pos += 1
    if emit_state:
        sf_ref = refs[pos]
        pos += 1
    of_scr, ob_scr, q_scr, k_scr, v_scr = refs[pos:pos + 5]

    ln = DN_CHUNK
    nch = t_len // ln
    dk = q_ref.shape[1] // hpb
    dv = v_ref.shape[1] // hpb
    nh = pl.num_programs(1) * hpb
    h_base = pl.program_id(1) * hpb
    row, col = _chunk_masks(ln)
    eye = row == col
    o_scrs = (of_scr, ob_scr)
    a_refs = (a0_ref, a1_ref)
    b_refs = (b0_ref, b1_ref)
    masks = [_dir_masks(row, col, d) for d in range(2)]

    trow = lax.broadcasted_iota(jnp.int32, (t_len, 1), 0)

    def conv_silu(x, w):
        prev = jnp.where(trow == 0, 0.0, pltpu.roll(x, 1, 0))
        nxt = jnp.where(trow == t_len - 1, 0.0, pltpu.roll(x, t_len - 1, 0))
        return _silu(prev * w[0:1, :] + x * w[1:2, :] + nxt * w[2:3, :])

    for hh in range(hpb):
        sl = slice(hh * dk, (hh + 1) * dk)
        q_scr[:, sl] = _l2n(conv_silu(q_ref[:, sl], cw_ref[hh, 0])) * (dk ** -0.5)
        k_scr[:, sl] = _l2n(conv_silu(k_ref[:, sl], cw_ref[hh, 1]))
        v_scr[:, sl] = conv_silu(v_ref[:, sl], cw_ref[hh, 2])

    def load_qkv(hh, n):
        r0 = pl.multiple_of(n * ln, ln)
        return (r0, q_scr[pl.ds(r0, ln), hh * dk:(hh + 1) * dk],
                k_scr[pl.ds(r0, ln), hh * dk:(hh + 1) * dk],
                v_scr[pl.ds(r0, ln), hh * dv:(hh + 1) * dv])

    def gate_terms(hh, d, n):
        incl, incl_t, _ = masks[d]
        a_row = a_refs[d][hh, 0, pl.ds(n, 1), :]
        beta_row = _sigmoid(b_refs[d][hh, 0, pl.ds(n, 1), :])
        neg_rate = -jnp.exp(jnp.zeros((1, 1), F32) + alog_ref[d * nh + h_base + hh])
        g_row = neg_rate * _softplus(a_row + dtb_ref[d * nh + h_base + hh])
        gc_col, gc_row = _cumsum_forms(g_row, eye, incl, incl_t)
        g_last = jnp.sum(g_row, axis=1, keepdims=True)
        return dict(beta=_to_col(eye, beta_row), gc_col=gc_col, g_last=g_last,
                    decay=jnp.exp(jnp.where(incl, gc_col - gc_row, -jnp.inf)))

    def neg_a_of(d, g, kbk):
        return jnp.where(masks[d][2], -(kbk * g['decay']), 0.0)

    def chunks_all(ns, states):
        items = []
        for hh in range(hpb):
            if nch == 1:
                r0, q, k, v = load_qkv(hh, 0)
                gs = [gate_terms(hh, d, 0) for d in range(2)]
                both = _bdot_nt(jnp.concatenate([k * gs[0]['beta'], k * gs[1]['beta'], q], axis=0), k)
                for d in range(2):
                    items.append((hh, d, r0, gs[d], both[d * ln:(d + 1) * ln], both[2 * ln:], q, k, v))
            else:
                for d in range(2):
                    r0, q, k, v = load_qkv(hh, ns[d])
                    g = gate_terms(hh, d, ns[d])
                    both = _bdot_nt(jnp.concatenate([k * g['beta'], q], axis=0), k)
                    items.append((hh, d, r0, g, both[:ln], both[ln:], q, k, v))
        invs = _unit_triangular_inverses([neg_a_of(it[1], it[3], it[4]) for it in items], row, col)
        rs = [_bdot(inv, jnp.concatenate([it[8] * it[3]['beta'],
                                          it[7] * it[3]['beta'] * jnp.exp(it[3]['gc_col'])], axis=1))
              for inv, it in zip(invs, items)]
        inter = [None if st is None else
                 _bdot(jnp.concatenate([r[:, dv:], it[6] * jnp.exp(it[3]['gc_col'])], axis=0), st)
                 for r, it, st in zip(rs, items, states)]
        v_news = [r[:, :dv] if x is None else r[:, :dv] - x[:ln] for r, x in zip(rs, inter)]
        boths = [_bdot(jnp.concatenate(
            [it[5] * it[3]['decay'], (it[7] * jnp.exp(it[3]['g_last'] - it[3]['gc_col'])).T], axis=0), vn)
            for it, vn in zip(items, v_news)]
        out = []
        for it, x, both, st in zip(items, inter, boths, states):
            hh, d, r0, g = it[:4]
            o_scrs[d][pl.ds(r0, ln), hh * dv:(hh + 1) * dv] = (
                both[:ln] if x is None else both[:ln] + x[ln:])
            out.append(both[ln:] if st is None else both[ln:] + st * jnp.exp(g['g_last']))
        return out

    def body(i, carry):
        return tuple(chunks_all((i, nch - 1 - i), carry))

    if has_init:
        init = [s0_ref[0, d, hh] for hh in range(hpb) for d in range(2)]
    else:
        init = [None] * (2 * hpb)
    if nch == 1:
        fin = chunks_all((0, 0), init)
    else:
        fin = _chunk_loop(nch, body, tuple(jnp.zeros((dk, dv), F32) if c is None else c
                                           for c in init))
    for hh in range(hpb):
        if emit_state:
            sf_ref[0, 0, hh] = fin[hh * 2]
            sf_ref[0, 1, hh] = fin[hh * 2 + 1]
        sl = slice(hh * dv, (hh + 1) * dv)
        o = _rms_rows(of_scr[:, sl] + ob_scr[:, sl]) * ng_ref[...]
        y_ref[:, sl] = o * _silu(z_ref[:, sl])


def _deltanet(main, gr, cols, conv_w, a_log, dt_bias, norm_g, nb, t_len, init, emit_state, hpb):
    m = main.shape[0]
    nh = DN_HEADS
    dk = norm_g.shape[1]
    dv = dk
    nch = t_len // DN_CHUNK
    gr3 = gr.reshape(gr.shape[0], nb, nch, DN_CHUNK)
    cw = conv_w.reshape(conv_w.shape[0], 3, nh, dk).transpose(2, 1, 0, 3)

    def colspec(off):
        return pl.BlockSpec((t_len, hpb * dk), lambda b, h: (b, off // (hpb * dk) + h))

    def gspec(base):
        return pl.BlockSpec((hpb, 1, nch, DN_CHUNK), lambda b, h: (base // hpb + h, b, 0, 0))

    smem = pl.BlockSpec(memory_space=pltpu.SMEM)
    in_specs = [colspec(cols['q']), colspec(cols['k']), colspec(cols['v']), colspec(cols['z']),
                gspec(0), gspec(nh), gspec(2 * nh), gspec(3 * nh),
                pl.BlockSpec((hpb, 3, 3, dk), lambda b, h: (h, 0, 0, 0)), smem, smem,
                pl.BlockSpec((1, dv), lambda b, h: (0, 0))]
    args = [main, main, main, main, gr3, gr3, gr3, gr3, cw,
            a_log.reshape(-1), dt_bias.reshape(-1), norm_g]
    if init is not None:
        in_specs.append(pl.BlockSpec((1, 2, hpb, dk, dv), lambda b, h: (b, 0, h, 0, 0)))
        args.append(init)
    out_shape = [jax.ShapeDtypeStruct((m, nh * dv), F32)]
    out_specs = [pl.BlockSpec((t_len, hpb * dv), lambda b, h: (b, h))]
    if emit_state:
        out_shape.append(jax.ShapeDtypeStruct((nb, 2, nh, dk, dv), F32))
        out_specs.append(pl.BlockSpec((1, 2, hpb, dk, dv), lambda b, h: (b, 0, h, 0, 0)))
    return pl.pallas_call(
        functools.partial(_deltanet_kernel, t_len, hpb, init is not None, emit_state),
        name="deltanet",
        grid=(nb, nh // hpb),
        in_specs=in_specs,
        out_specs=out_specs,
        out_shape=out_shape,
        scratch_shapes=[pltpu.VMEM((t_len, hpb * dv), F32), pltpu.VMEM((t_len, hpb * dv), F32),
                        pltpu.VMEM((t_len, hpb * dk), F32), pltpu.VMEM((t_len, hpb * dk), F32),
                        pltpu.VMEM((t_len, hpb * dv), F32)],
        compiler_params=_cparams(("parallel", "parallel")),
    )(*args)


def _pack_cols(w, sizes, order, gate_names):
    offs = {}
    o = 0
    for name, sz in sizes:
        offs[name] = (o, sz)
        o += sz
    main = jnp.concatenate([w[:, offs[n][0]:offs[n][0] + offs[n][1]] for n in order], axis=1)
    gates = jnp.concatenate([w[:, offs[n][0]:offs[n][0] + offs[n][1]] for n in gate_names], axis=1)
    return main.astype(BF16), gates.T.astype(BF16)


def kernel(x_prompt, x_sample, state_s5, state_ml_c, state_ml_n, state_ml_m, state_ret, state_dn,
           c, c_ctx, mod_w, mod_b, norm_pre, norm_post,
           ab_w_in, ab_w_out, s5_lambda_re, s5_lambda_im, s5_log_step, s5_b_re, s5_b_im,
           s5_c_re, s5_c_im, s5_d, s5_w_glu, ml_i_bias, ml_f_bias, ml_norm,
           cd_w_in, cd_w_out, ret_decay_logit, dn_conv, dn_a_log, dn_dt_bias, dn_norm):
    bp, tp, d = x_prompt.shape
    bs, ts, _ = x_sample.shape
    depth = mod_w.shape[0]
    br = s5_d.shape[1]
    g_s5 = s5_lambda_re.shape[2]
    ml_dv = br // ML_HEADS
    ml_dk = ml_dv // 2
    ret_dv = br // RET_HEADS
    ret_dk = ret_dv // 2
    assert tp == SEG and ts % SEG == 0 and (bs * ts // SEG) == SUBLANES and bp % SUBLANES == 0
    nseg = ts // SEG

    cond = jnp.zeros((SUBLANES, d), F32).at[0].set(c_ctx).at[1:1 + bs].set(c)
    mod = _modulation(cond, mod_w, mod_b)
    mod = mod.reshape(depth, SUBLANES, 3, d)
    mod = mod.at[:, :, 1].add(1.0)
    mod = mod[:, :, jnp.array([1, 0, 2])]

    ab_sizes = (('u', br), ('za', br), ('q', ML_HEADS * ml_dk), ('k', ML_HEADS * ml_dk), ('v', br),
                ('o', br), ('ig', 2 * ML_HEADS), ('fg', 2 * ML_HEADS), ('zb', br))
    ab_order = ('u', 'za', 'q', 'k', 'v', 'o', 'zb')
    ab_cols = {'za': 0, 'q': br, 'k': br + ML_HEADS * ml_dk, 'v': br + 2 * ML_HEADS * ml_dk,
               'o': 2 * br + 2 * ML_HEADS * ml_dk, 'z': 3 * br + 2 * ML_HEADS * ml_dk}
    dn_qkv = 3 * br
    cd_sizes = (('rq', RET_HEADS * ret_dk), ('rk', RET_HEADS * ret_dk), ('rv', br), ('zc', br),
                ('qkv', dn_qkv), ('a', 2 * DN_HEADS), ('b', 2 * DN_HEADS), ('zd', br))
    cd_order = ('rq', 'rk', 'rv', 'zc', 'qkv', 'zd')
    ret_cols = {'q': 0, 'k': RET_HEADS * ret_dk, 'v': 2 * RET_HEADS * ret_dk,
                'z': 2 * RET_HEADS * ret_dk + br}
    dn_base = 2 * RET_HEADS * ret_dk + 2 * br
    dn_cols = {'q': dn_base, 'k': dn_base + br, 'v': dn_base + 2 * br, 'z': dn_base + 3 * br}

    rope = _rope_tables(ts, ret_dk)

    xp = x_prompt.reshape(bp * tp, d)
    xs = x_sample.reshape(bs * ts, d)
    new_s5, new_mc, new_mn, new_mm, new_ret, new_dn = [], [], [], [], [], []
    for l in range(depth):
        j = l // 2
        mod_p = mod[l, 0:1]
        mod_s = mod[l, 1:1 + bs]
        g_pre = norm_pre[l][None]
        g_post = norm_post[l][None]
        if l % 2 == 0:
            w_main, w_gt = _pack_cols(ab_w_in[j], ab_sizes, ab_order, ('ig', 'fg'))
            bw, cw, acoef = _s5_weights(s5_lambda_re[j], s5_lambda_im[j], s5_log_step[j],
                                        s5_b_re[j], s5_b_im[j], s5_c_re[j], s5_c_im[j])
            w_glu = s5_w_glu[j].astype(BF16)
            w_out = ab_w_out[j].astype(BF16)
            outs = []
            for (x2, nb, t_len, md, is_sample) in ((xp, bp, tp, mod_p, False), (xs, bs, ts, mod_s, True)):
                rpm = x2.shape[0] // md.shape[0]
                u_tm, main, gr = _inproj(x2, md[:, 0:2], rpm, g_pre, w_main, w_gt, br)
                nrow = x2.shape[0] // SEG
                if is_sample:
                    st = state_s5[:, j].astype(F32)
                    x0 = jnp.zeros((bs, nseg, 2, 2, g_s5, S5_P), F32)
                    x0 = x0.at[:, 0, 0].set(st[:, 0]).at[:, nseg - 1, 1].set(st[:, 1])
                    x0 = _s5_state_to_rows(x0.reshape(bs * nseg, 2, 2, g_s5, S5_P))
                else:
                    x0 = jnp.zeros((nrow, 4 * g_s5 * S5_P), F32)
                y_tm, xf = _s5_scan(u_tm.reshape(SEG, nrow, br), bw, cw, acoef, x0,
                                    nseg if is_sample else 1)
                ya = _s5_glu(y_tm.reshape(SEG, nrow * br), u_tm, main, s5_d[j][None], w_glu)
                init = None
                if is_sample:
                    init = (state_ml_c[:, j].astype(F32), state_ml_n[:, j].astype(F32),
                            state_ml_m[:, j].astype(F32))
                res = _mlstm(main, gr, ab_cols, ml_i_bias[j], ml_f_bias[j], ml_norm[j][None],
                             nb, t_len, init, not is_sample, 2 if is_sample else ML_HEADS)
                yb = res[0]
                if not is_sample:
                    new_s5.append(_s5_rows_to_state(xf, g_s5, S5_P))
                    new_mc.append(res[1])
                    new_mn.append(res[2][:, :, :, 0, :])
                    new_mm.append(res[3][:, :, :, 0, 0])
                outs.append(_outproj(ya, yb, w_out, x2, md, rpm, g_post))
            xp, xs = outs
        else:
            w_main, w_gt = _pack_cols(cd_w_in[j], cd_sizes, cd_order, ('a', 'b'))
            w_out = cd_w_out[j].astype(BF16)
            outs = []
            for (x2, nb, t_len, md, is_sample) in ((xp, bp, tp, mod_p, False), (xs, bs, ts, mod_s, True)):
                rpm = x2.shape[0] // md.shape[0]
                main, gr = _inproj(x2, md[:, 0:2], rpm, g_pre, w_main, w_gt, 0)
                res_c = _retention(main, ret_cols, ret_decay_logit[j], nb, t_len,
                                   rope if is_sample else None,
                                   state_ret[:, j].astype(F32) if is_sample else None, not is_sample,
                                   2 if is_sample else RET_HEADS)
                res_d = _deltanet(main, gr, dn_cols, dn_conv[j], dn_a_log[j], dn_dt_bias[j],
                                  dn_norm[j][None], nb, t_len,
                                  state_dn[:, j].astype(F32) if is_sample else None, not is_sample,
                                  4 if is_sample else DN_HEADS)
                if not is_sample:
                    new_ret.append(res_c[1])
                    new_dn.append(res_d[1])
                outs.append(_outproj(res_c[0], res_d[0], w_out, x2, md, rpm, g_post))
            xp, xs = outs
    return (xp.reshape(bp, tp, d), xs.reshape(bs, ts, d), jnp.stack(new_s5, 1), jnp.stack(new_mc, 1),
            jnp.stack(new_mn, 1), jnp.stack(new_mm, 1), jnp.stack(new_ret, 1), jnp.stack(new_dn, 1))
```

```python
import functools
import math

import jax
import jax.numpy as jnp
from jax import lax
from jax.experimental import pallas as pl
from jax.experimental.pallas import tpu as pltpu

F32 = jnp.float32
BF16 = jnp.bfloat16

EPS = 1e-6
GRID_W = 64
ROPE_BASE = 10000.0
S5_GROUP = 16
S5_P = 64
ML_HEADS = 4
RET_HEADS = 4
DN_HEADS = 8
LANES = 128
SUBLANES = 8
SEG = 256
ML_CHUNK = 256
DN_CHUNK = 256
VMEM_LIMIT = 56 * 1024 * 1024


def _cparams(sem):
    return pltpu.CompilerParams(dimension_semantics=sem, vmem_limit_bytes=VMEM_LIMIT)


def _bdot(a, b):
    return jnp.dot(a.astype(BF16), b.astype(BF16), preferred_element_type=F32)


def _bdot_nt(a, b):
    return lax.dot_general(a.astype(BF16), b.astype(BF16), (((1,), (1,)), ((), ())),
                           preferred_element_type=F32)


def _bdot_tn(a, b):
    return lax.dot_general(a.astype(BF16), b.astype(BF16), (((0,), (0,)), ((), ())),
                           preferred_element_type=F32)


def _sigmoid(x):
    return 1.0 / (1.0 + jnp.exp(-x))


def _silu(x):
    return x * _sigmoid(x)


def _softplus(x):
    return jnp.maximum(x, 0.0) + jnp.log(1.0 + jnp.exp(-jnp.abs(x)))


def _log_sigmoid(x):
    return -_softplus(-x)


def _gelu_tanh(x):
    return 0.5 * x * (1.0 + jnp.tanh(math.sqrt(2.0 / math.pi) * (x + 0.044715 * (x * x * x))))


def _rms_rows(x):
    return x * lax.rsqrt(jnp.mean(x * x, axis=-1, keepdims=True) + EPS)


def _mod_kernel(cond_ref, w_ref, b_ref, o_ref):
    o_ref[0] = _bdot(_silu(cond_ref[...]), w_ref[0]) + b_ref[0]


def _modulation(cond, mod_w, mod_b):
    depth, d, n3 = mod_w.shape
    tn = 1024
    return pl.pallas_call(
        _mod_kernel, name="modulation",
        grid=(depth, n3 // tn),
        in_specs=[pl.BlockSpec((SUBLANES, d), lambda l, n: (0, 0)),
                  pl.BlockSpec((1, d, tn), lambda l, n: (l, 0, n)),
                  pl.BlockSpec((1, 1, tn), lambda l, n: (l, 0, n))],
        out_specs=pl.BlockSpec((1, SUBLANES, tn), lambda l, n: (l, 0, n)),
        out_shape=jax.ShapeDtypeStruct((depth, SUBLANES, n3), F32),
        compiler_params=_cparams(("parallel", "parallel")),
    )(cond, mod_w, mod_b.reshape(depth, 1, n3))


def _inproj_kernel(n_tm, x_ref, mod_ref, g_ref, w_ref, wgt_ref, *out_refs):
    x = x_ref[...]
    h = _rms_rows(x) * g_ref[...]
    h = (h * mod_ref[0, 0:1, :] + mod_ref[0, 1:2, :]).astype(BF16)
    tm_ref = out_refs[0] if n_tm else None
    main_ref, gr_ref = out_refs[-2], out_refs[-1]
    step = 1024
    n_main = main_ref.shape[1]
    for n0 in range(0, n_tm, step):
        tm_ref[:, n0:n0 + step] = jnp.dot(h, w_ref[:, n0:n0 + step], preferred_element_type=F32)
    for n0 in range(0, n_main, step):
        main_ref[:, n0:n0 + step] = jnp.dot(h, w_ref[:, n_tm + n0:n_tm + n0 + step],
                                            preferred_element_type=F32)
    gr_ref[...] = lax.dot_general(wgt_ref[...], h, (((1,), (1,)), ((), ())),
                                  preferred_element_type=F32)


def _inproj(x2, mod, rows_per_mod, gain, w, wgt, n_tm):
    m, d = x2.shape
    n = w.shape[1]
    ng = wgt.shape[0]
    nt = m // SEG
    tiles_per_mod = rows_per_mod // SEG
    out_shape, out_specs = [], []
    if n_tm:
        out_shape.append(jax.ShapeDtypeStruct((SEG, nt * n_tm), F32))
        out_specs.append(pl.BlockSpec((SEG, n_tm), lambda i: (0, i)))
    out_shape += [jax.ShapeDtypeStruct((m, n - n_tm), F32), jax.ShapeDtypeStruct((ng, m), F32)]
    out_specs += [pl.BlockSpec((SEG, n - n_tm), lambda i: (i, 0)),
                  pl.BlockSpec((ng, SEG), lambda i: (0, i))]
    return pl.pallas_call(
        functools.partial(_inproj_kernel, n_tm), name="inproj",
        grid=(nt,),
        in_specs=[pl.BlockSpec((SEG, d), lambda i: (i, 0)),
                  pl.BlockSpec((1, 2, d), lambda i: (i // tiles_per_mod, 0, 0)),
                  pl.BlockSpec((1, d), lambda i: (0, 0)),
                  pl.BlockSpec((d, n), lambda i: (0, 0), pipeline_mode=pl.Buffered(1)),
                  pl.BlockSpec((ng, d), lambda i: (0, 0))],
        out_specs=out_specs,
        out_shape=out_shape,
        compiler_params=_cparams(("parallel",)),
    )(x2, mod, gain, w, wgt)


def _outproj_kernel(ya_ref, yb_ref, w_ref, x_ref, mod_ref, g_ref, o_ref):
    half = ya_ref.shape[1]
    acc = (jnp.dot(ya_ref[...].astype(BF16), w_ref[0:half, :], preferred_element_type=F32)
           + jnp.dot(yb_ref[...].astype(BF16), w_ref[half:2 * half, :], preferred_element_type=F32))
    o_ref[...] = x_ref[...] + mod_ref[0, 2:3, :] * (_rms_rows(acc) * g_ref[...])


def _outproj(ya, yb, w, x2, mod, rows_per_mod, gain):
    m, d = x2.shape
    br = ya.shape[1]
    tm = 2 * SEG
    tiles_per_mod = rows_per_mod // tm
    return pl.pallas_call(
        _outproj_kernel, name="outproj",
        grid=(m // tm,),
        in_specs=[pl.BlockSpec((tm, br), lambda i: (i, 0)),
                  pl.BlockSpec((tm, br), lambda i: (i, 0)),
                  pl.BlockSpec((2 * br, d), lambda i: (0, 0), pipeline_mode=pl.Buffered(1)),
                  pl.BlockSpec((tm, d), lambda i: (i, 0)),
                  pl.BlockSpec((1, 3, d), lambda i: (i // tiles_per_mod, 0, 0)),
                  pl.BlockSpec((1, d), lambda i: (0, 0))],
        out_specs=pl.BlockSpec((tm, d), lambda i: (i, 0)),
        out_shape=jax.ShapeDtypeStruct((m, d), F32),
        compiler_params=_cparams(("parallel",)),
    )(ya, yb, w, x2, mod, gain)


def _cmul(ar, ai, xr, xi):
    return ar * xr - ai * xi, ar * xi + ai * xr


def _s5_kernel(t_len, nseg, u_ref, bw_ref, cw_ref, a_ref, x0_ref, y_ref, xf_ref, bu_scr):
    hp = xf_ref.shape[1] // 4
    u = u_ref[...].reshape(t_len * SUBLANES, LANES)
    bu_scr[...] = _bdot(u, bw_ref[0])
    a = a_ref[0]
    coef = [jnp.broadcast_to(a[r:r + 1, :], (SUBLANES, hp)) for r in range(4)]

    def scan(init, store):
        def step(t, carry):
            xfr, xfi, xbr, xbi = carry
            rf = pl.multiple_of(t * SUBLANES, SUBLANES)
            rb = pl.multiple_of((t_len - 1 - t) * SUBLANES, SUBLANES)
            pr, pi = _cmul(coef[0], coef[1], xfr, xfi)
            nfr = pr + bu_scr[pl.ds(rf, SUBLANES), 0:hp]
            nfi = pi + bu_scr[pl.ds(rf, SUBLANES), hp:2 * hp]
            pr, pi = _cmul(coef[2], coef[3], xbr, xbi)
            nbr = pr + bu_scr[pl.ds(rb, SUBLANES), 2 * hp:3 * hp]
            nbi = pi + bu_scr[pl.ds(rb, SUBLANES), 3 * hp:4 * hp]
            if store:
                bu_scr[pl.ds(rf, SUBLANES), 0:hp] = nfr
                bu_scr[pl.ds(rf, SUBLANES), hp:2 * hp] = nfi
                bu_scr[pl.ds(rb, SUBLANES), 2 * hp:3 * hp] = nbr
                bu_scr[pl.ds(rb, SUBLANES), 3 * hp:4 * hp] = nbi
            return nfr, nfi, nbr, nbi
        return lax.fori_loop(0, t_len, step, init, unroll=4)

    x0 = tuple(x0_ref[:, r * hp:(r + 1) * hp] for r in range(4))
    if nseg > 1:
        zero = jnp.zeros((SUBLANES, hp), F32)
        ffr, ffi, fbr, fbi = scan((zero, zero, zero, zero), False)
        pfr, pfi, pbr, pbi = coef
        for _ in range(int(math.log2(t_len))):
            pfr, pfi = _cmul(pfr, pfi, pfr, pfi)
            pbr, pbi = _cmul(pbr, pbi, pbr, pbi)
        seg = lax.broadcasted_iota(jnp.int32, (SUBLANES, hp), 0) % nseg
        sfr, sfi, sbr, sbi = x0
        for _ in range(nseg - 1):
            tr, ti = _cmul(pfr, pfi, sfr, sfi)
            sfr = x0[0] + jnp.where(seg >= 1, pltpu.roll(tr + ffr, 1, 0), 0.0)
            sfi = x0[1] + jnp.where(seg >= 1, pltpu.roll(ti + ffi, 1, 0), 0.0)
            tr, ti = _cmul(pbr, pbi, sbr, sbi)
            sbr = x0[2] + jnp.where(seg <= nseg - 2, pltpu.roll(tr + fbr, SUBLANES - 1, 0), 0.0)
            sbi = x0[3] + jnp.where(seg <= nseg - 2, pltpu.roll(ti + fbi, SUBLANES - 1, 0), 0.0)
        x0 = (sfr, sfi, sbr, sbi)
    fin = scan(x0, True)
    for r in range(4):
        xf_ref[:, r * hp:(r + 1) * hp] = fin[r]
    y_t = _bdot_nt(cw_ref[0], bu_scr[...])
    y_ref[...] = y_t.T.reshape(t_len, SUBLANES, LANES)


def _s5_scan(u_tm, bw, cw, acoef, x0, nseg):
    t_len, nb, br = u_tm.shape
    nj = br // LANES
    sw = bw.shape[2]
    assert t_len & (t_len - 1) == 0
    return pl.pallas_call(
        functools.partial(_s5_kernel, t_len, nseg), name="s5_scan",
        grid=(nb // SUBLANES, nj),
        in_specs=[pl.BlockSpec((t_len, SUBLANES, LANES), lambda i, j: (0, i, j)),
                  pl.BlockSpec((1, LANES, sw), lambda i, j: (j, 0, 0)),
                  pl.BlockSpec((1, LANES, sw), lambda i, j: (j, 0, 0)),
                  pl.BlockSpec((1, 4, sw // 4), lambda i, j: (j, 0, 0)),
                  pl.BlockSpec((SUBLANES, sw), lambda i, j: (i, j))],
        out_specs=[pl.BlockSpec((t_len, SUBLANES, LANES), lambda i, j: (0, i, j)),
                   pl.BlockSpec((SUBLANES, sw), lambda i, j: (i, j))],
        out_shape=[jax.ShapeDtypeStruct((t_len, nb, br), F32),
                   jax.ShapeDtypeStruct((nb, nj * sw), F32)],
        scratch_shapes=[pltpu.VMEM((t_len * SUBLANES, sw), F32)],
        compiler_params=_cparams(("parallel", "parallel")),
    )(u_tm, bw, cw, acoef, x0)


def _s5_glu_kernel(y_ref, u_ref, z_ref, d_ref, w_ref, o_ref):
    y = y_ref[...] + d_ref[...] * u_ref[...]
    g = _gelu_tanh(y)
    o_ref[...] = g * _sigmoid(_bdot(g, w_ref[...])) * _silu(z_ref[...])


def _s5_glu(y_tm2, u_tm2, main, d_skip, w_glu):
    br = d_skip.shape[1]
    nt = y_tm2.shape[1] // br
    return pl.pallas_call(
        _s5_glu_kernel, name="s5_glu",
        grid=(nt,),
        in_specs=[pl.BlockSpec((SEG, br), lambda i: (0, i)),
                  pl.BlockSpec((SEG, br), lambda i: (0, i)),
                  pl.BlockSpec((SEG, br), lambda i: (i, 0)),
                  pl.BlockSpec((1, br), lambda i: (0, 0)),
                  pl.BlockSpec((br, br), lambda i: (0, 0))],
        out_specs=pl.BlockSpec((SEG, br), lambda i: (i, 0)),
        out_shape=jax.ShapeDtypeStruct((nt * SEG, br), F32),
        compiler_params=_cparams(("parallel",)),
    )(y_tm2, u_tm2, main, d_skip, w_glu)


def _s5_weights(lam_re, lam_im, log_step, b_re, b_im, c_re, c_im):
    lr, li = lam_re.astype(F32), lam_im.astype(F32)
    step = jnp.exp(log_step.astype(F32))[..., None]
    mag = jnp.exp(lr * step)
    a_r, a_i = mag * jnp.cos(li * step), mag * jnp.sin(li * step)
    den = lr * lr + li * li
    c_r = ((a_r - 1.0) * lr + a_i * li) / den
    c_i = (a_i * lr - (a_r - 1.0) * li) / den
    br_, bi_ = b_re.astype(F32), b_im.astype(F32)
    bbar_r = c_r[..., None] * br_ - c_i[..., None] * bi_
    bbar_i = c_r[..., None] * bi_ + c_i[..., None] * br_
    g = lr.shape[1]
    gl = LANES // S5_GROUP
    nj = g // gl
    eye = jnp.eye(gl, dtype=F32)

    def bw_part(x):
        x = x.reshape(nj, gl, S5_P, S5_GROUP)
        return jnp.einsum('ab,jbps->jasbp', eye, x).reshape(nj, LANES, gl * S5_P)

    def cw_part(x):
        x = x.reshape(nj, gl, S5_GROUP, S5_P)
        return jnp.einsum('ab,jbsp->jbpas', eye, x).reshape(nj, gl * S5_P, LANES)

    bw = jnp.concatenate([bw_part(bbar_r[0]), bw_part(bbar_i[0]),
                          bw_part(bbar_r[1]), bw_part(bbar_i[1])], axis=2)
    cw = jnp.concatenate([cw_part(c_re[0].astype(F32)), cw_part(-c_im[0].astype(F32)),
                          cw_part(c_re[1].astype(F32)), cw_part(-c_im[1].astype(F32))], axis=1)
    acoef = jnp.stack([a_r[0], a_i[0], a_r[1], a_i[1]], axis=0)
    acoef = acoef.reshape(4, nj, gl * S5_P).transpose(1, 0, 2)
    return bw.astype(BF16), cw.transpose(0, 2, 1).astype(BF16), acoef


def _s5_state_to_rows(st):
    nb, _, _, g, p = st.shape
    gl = LANES // S5_GROUP
    nj = g // gl
    return st.reshape(nb, 4, nj, gl * p).transpose(0, 2, 1, 3).reshape(nb, nj * 4 * gl * p)


def _s5_rows_to_state(rows, g, p):
    nb = rows.shape[0]
    gl = LANES // S5_GROUP
    nj = g // gl
    return rows.reshape(nb, nj, 4, gl * p).transpose(0, 2, 1, 3).reshape(nb, 2, 2, g, p)


def _chunk_masks(length):
    row = lax.broadcasted_iota(jnp.int32, (length, length), 0)
    col = lax.broadcasted_iota(jnp.int32, (length, length), 1)
    return row, col


def _to_col(eye, x_row):
    return jnp.sum(jnp.where(eye, x_row, 0.0), axis=1, keepdims=True)


def _cumsum_forms(x_row, eye, incl, incl_t):
    x_col = _to_col(eye, x_row)
    c_col = jnp.sum(jnp.where(incl, x_row, 0.0), axis=1, keepdims=True)
    c_row = jnp.sum(jnp.where(incl_t, x_col, 0.0), axis=0, keepdims=True)
    return c_col, c_row


def _dir_masks(row, col, d):
    if d == 0:
        return col <= row, row <= col, col < row
    return col >= row, row >= col, col > row


def _chunk_loop(nch, body, init):
    return lax.fori_loop(0, nch, body, init, unroll=nch <= 2)


def _state_slab_out(slot, nb, nh, hpb, dk, dv):
    j, n_slots, prev = slot
    shape = jax.ShapeDtypeStruct((nb, n_slots, 2, nh, dk, dv), F32)
    if prev is None:
        return shape, pl.BlockSpec((1, n_slots, 2, hpb, dk, dv), lambda b, h: (b, 0, 0, h, 0, 0))
    return shape, pl.BlockSpec((1, 1, 2, hpb, dk, dv), lambda b, h: (b, j, 0, h, 0, 0))


def _write_state_slab(ref, j, hh, per_dir):
    n_in = ref.shape[1]
    for sl in range(n_in):
        for d in range(2):
            keep = n_in == 1 or sl == j
            ref[0, sl, d, hh] = per_dir[d] if keep else jnp.zeros_like(per_dir[d])


def _mlstm_kernel(t_len, hpb, has_init, slot_j, has_prev, *refs):
    (q_ref, k_ref, v_ref, o_ref, z_ref, ig0_ref, ig1_ref, fg0_ref, fg1_ref,
     ib_ref, fb_ref, ng_ref) = refs[:12]
    pos = 12
    if has_init:
        c0_ref, n0_ref, m0_ref = refs[pos:pos + 3]
        pos += 3
    pos += int(has_prev)
    y_ref = refs[pos]
    pos += 1
    emit_state = slot_j is not None
    if emit_state:
        cf_ref, nf_ref, mf_ref = refs[pos:pos + 3]
        pos += 3
    hf_scr, hb_scr = refs[pos:pos + 2]

    ln = ML_CHUNK
    nch = t_len // ln
    dk = q_ref.shape[1] // hpb
    dv = v_ref.shape[1] // hpb
    b_id = pl.program_id(0)
    nh = pl.num_programs(1) * hpb
    h_base = pl.program_id(1) * hpb
    row, col = _chunk_masks(ln)
    eye = row == col
    scale = dk ** -0.5
    ig_refs = (ig0_ref, ig1_ref)
    fg_refs = (fg0_ref, fg1_ref)
    h_scrs = (hf_scr, hb_scr)
    masks = [_dir_masks(row, col, d) for d in range(2)]

    def gate_terms(hh, d, n, m_st):
        incl, incl_t, _ = masks[d]
        i_row = ig_refs[d][hh, 0, pl.ds(n, 1), :] + ib_ref[d * nh + h_base + hh]
        lf_row = _log_sigmoid(fg_refs[d][hh, 0, pl.ds(n, 1), :] + fb_ref[d * nh + h_base + hh])
        i_col = _to_col(eye, i_row)
        b_col, b_row = _cumsum_forms(lf_row, eye, incl, incl_t)
        c_row = i_row - b_row
        c_col = i_col - b_col
        cm = jnp.where(incl, c_row, -jnp.inf)
        mu = jnp.maximum(m_st, jnp.max(cm, axis=1, keepdims=True))
        mu_last = jnp.maximum(m_st, jnp.max(c_row, axis=1, keepdims=True))
        b_last = jnp.sum(lf_row, axis=1, keepdims=True)
        return dict(e=jnp.exp(cm - mu), inter=jnp.exp(m_st - mu), floor=jnp.exp(-b_col - mu),
                    w=jnp.exp(c_col - mu_last), dec=jnp.exp(m_st - mu_last), m_new=b_last + mu_last)

    def load_qkv(hh, n):
        r0 = pl.multiple_of(n * ln, ln)
        return (r0, q_ref[pl.ds(r0, ln), hh * dk:(hh + 1) * dk],
                k_ref[pl.ds(r0, ln), hh * dk:(hh + 1) * dk] * scale,
                v_ref[pl.ds(r0, ln), hh * dv:(hh + 1) * dv])

    def finish(hh, d, r0, g, s, sv, q, qc, carry):
        num, den = sv, jnp.sum(s, axis=1, keepdims=True)
        if carry is not None:
            num = num + g['inter'] * qc
            den = den + g['inter'] * jnp.sum(q * carry[1], axis=1, keepdims=True)
        h_scrs[d][pl.ds(r0, ln), hh * dv:(hh + 1) * dv] = num / jnp.maximum(jnp.abs(den), g['floor'])

    def new_state(g, kw, kv, carry):
        n_add = jnp.sum(kw, axis=0, keepdims=True)
        if carry is None:
            return kv, n_add, g['m_new']
        return g['dec'] * carry[0] + kv, g['dec'] * carry[1] + n_add, g['m_new']

    def state_products(qs, carries):
        return [None if c is None else _bdot(q, c[0]) for q, c in zip(qs, carries)]

    def step_split(ns, carries):
        items = [(hh, d) for hh in range(hpb) for d in range(2)]
        loads = [load_qkv(hh, ns[d]) for hh, d in items]
        gs = [gate_terms(hh, d, ns[d], carries[c][2]) for c, (hh, d) in enumerate(items)]
        ss = [_bdot_nt(ld[1], ld[2]) * g['e'] for ld, g in zip(loads, gs)]
        svs = [_bdot(sm, ld[3]) for sm, ld in zip(ss, loads)]
        qcs = state_products([ld[1] for ld in loads], carries)
        kws = [ld[2] * g['w'] for ld, g in zip(loads, gs)]
        kvs = [_bdot_tn(kw, ld[3]) for kw, ld in zip(kws, loads)]
        out = []
        for c, (hh, d) in enumerate(items):
            finish(hh, d, loads[c][0], gs[c], ss[c], svs[c], loads[c][1], qcs[c], carries[c])
            out.append(new_state(gs[c], kws[c], kvs[c], carries[c]))
        return out

    def step_shared(carries):
        loads = [load_qkv(hh, 0) for hh in range(hpb)]
        zero_m = jnp.zeros((1, 1), F32)
        gs = [gate_terms(c // 2, c % 2, 0, zero_m if carries[c] is None else carries[c][2])
              for c in range(2 * hpb)]
        qks = [_bdot_nt(ld[1], ld[2]) for ld in loads]
        ss = [qks[c // 2] * gs[c]['e'] for c in range(2 * hpb)]
        svs = [_bdot(jnp.concatenate(ss[2 * hh:2 * hh + 2], axis=0), loads[hh][3])
               for hh in range(hpb)]
        qcs = state_products([loads[c // 2][1] for c in range(2 * hpb)], carries)
        kws = [loads[c // 2][2] * gs[c]['w'] for c in range(2 * hpb)]
        kvs = [_bdot_tn(jnp.concatenate(kws[2 * hh:2 * hh + 2], axis=1), loads[hh][3])
               for hh in range(hpb)]
        out = []
        for c in range(2 * hpb):
            hh, d = c // 2, c % 2
            finish(hh, d, loads[hh][0], gs[c], ss[c], svs[hh][d * ln:(d + 1) * ln], loads[hh][1],
                   qcs[c], carries[c])
            out.append(new_state(gs[c], kws[c], kvs[hh][d * dk:(d + 1) * dk], carries[c]))
        return out

    def body(i, carry):
        return tuple(step_split((i, nch - 1 - i), carry))

    init = []
    for hh in range(hpb):
        for d in range(2):
            if has_init:
                m0 = jnp.zeros((1, 1), F32) + m0_ref[(b_id * 2 + d) * nh + h_base + hh]
                init.append((c0_ref[0, d, hh], n0_ref[0, d, hh], m0))
            else:
                init.append(None)
    if nch == 1:
        fin = step_shared(init)
    else:
        zero_state = (jnp.zeros((dk, dv), F32), jnp.zeros((1, dk), F32), jnp.zeros((1, 1), F32))
        fin = _chunk_loop(nch, body, tuple(zero_state if c is None else c for c in init))
    if emit_state:
        for hh in range(hpb):
            _write_state_slab(cf_ref, slot_j, hh, [fin[hh * 2 + d][0] for d in range(2)])
            for d in range(2):
                nf_ref[0, d, hh] = fin[hh * 2 + d][1]
                mf_ref[0, d, hh] = jnp.broadcast_to(fin[hh * 2 + d][2], (1, LANES))
    for hh in range(hpb):
        sl = slice(hh * dv, (hh + 1) * dv)
        h = _rms_rows(hf_scr[:, sl] + hb_scr[:, sl]) * ng_ref[:, sl]
        y_ref[:, sl] = _sigmoid(o_ref[:, sl]) * h * _silu(z_ref[:, sl])


def _mlstm(main, gr, cols, i_bias, f_bias, norm_g, nb, t_len, init, slot, hpb):
    m = main.shape[0]
    nh = ML_HEADS
    dv = norm_g.shape[1] // nh
    dk = dv // 2
    nch = t_len // ML_CHUNK
    gr3 = gr.reshape(gr.shape[0], nb, nch, ML_CHUNK)

    def colspec(width, off):
        return pl.BlockSpec((t_len, hpb * width), lambda b, h: (b, off // (hpb * width) + h))

    def gspec(base):
        return pl.BlockSpec((hpb, 1, nch, ML_CHUNK), lambda b, h: (base // hpb + h, b, 0, 0))

    smem = pl.BlockSpec(memory_space=pltpu.SMEM)
    in_specs = [colspec(dk, cols['q']), colspec(dk, cols['k']), colspec(dv, cols['v']),
                colspec(dv, cols['o']), colspec(dv, cols['z']),
                gspec(0), gspec(nh), gspec(2 * nh), gspec(3 * nh), smem, smem,
                pl.BlockSpec((1, hpb * dv), lambda b, h: (0, h))]
    args = [main, main, main, main, main, gr3, gr3, gr3, gr3,
            i_bias.reshape(-1), f_bias.reshape(-1), norm_g]
    if init is not None:
        c0, n0, m0 = init
        in_specs += [pl.BlockSpec((1, 2, hpb, dk, dv), lambda b, h: (b, 0, h, 0, 0)),
                     pl.BlockSpec((1, 2, hpb, 1, dk), lambda b, h: (b, 0, h, 0, 0)), smem]
        args += [c0, n0.reshape(nb, 2, nh, 1, dk), m0.reshape(-1)]
    out_shape = [jax.ShapeDtypeStruct((m, nh * dv), F32)]
    out_specs = [pl.BlockSpec((t_len, hpb * dv), lambda b, h: (b, h))]
    aliases = {}
    if slot is not None:
        slab_shape, slab_spec = _state_slab_out(slot, nb, nh, hpb, dk, dv)
        out_shape += [slab_shape, jax.ShapeDtypeStruct((nb, 2, nh, 1, dk), F32),
                      jax.ShapeDtypeStruct((nb, 2, nh, 1, LANES), F32)]
        out_specs += [slab_spec, pl.BlockSpec((1, 2, hpb, 1, dk), lambda b, h: (b, 0, h, 0, 0)),
                      pl.BlockSpec((1, 2, hpb, 1, LANES), lambda b, h: (b, 0, h, 0, 0))]
        if slot[2] is not None:
            aliases = {len(args): 1}
            in_specs.append(pl.BlockSpec(memory_space=pl.ANY))
            args.append(slot[2])
    return pl.pallas_call(
        functools.partial(_mlstm_kernel, t_len, hpb, init is not None,
                          None if slot is None else slot[0], bool(aliases)), name="mlstm",
        grid=(nb, nh // hpb),
        in_specs=in_specs,
        out_specs=out_specs,
        out_shape=out_shape,
        input_output_aliases=aliases,
        scratch_shapes=[pltpu.VMEM((t_len, hpb * dv), F32), pltpu.VMEM((t_len, hpb * dv), F32)],
        compiler_params=_cparams(("parallel", "parallel")),
    )(*args)


def _rope_swap(x):
    quarter = x.shape[1] // 4
    lane = lax.broadcasted_iota(jnp.int32, x.shape, 1)
    up = pltpu.roll(x, x.shape[1] - quarter, 1)
    down = pltpu.roll(x, quarter, 1)
    return jnp.where((lane // quarter) % 2 == 0, up, down)


def _retention_kernel(t_len, hpb, has_pos, has_init, slot_j, has_prev, *refs):
    q_ref, k_ref, v_ref, z_ref, lg_ref = refs[:5]
    pos = 5
    if has_pos:
        cos_ref, sin_ref = refs[pos:pos + 2]
        pos += 2
    if has_init:
        r0_ref = refs[pos]
        pos += 1
    pos += int(has_prev)
    y_ref = refs[pos]
    pos += 1
    emit_state = slot_j is not None
    if emit_state:
        rf_ref = refs[pos]
        pos += 1
    of_scr, ob_scr, q_scr, k_scr = refs[pos:pos + 4]

    ln = ML_CHUNK
    nch = t_len // ln
    dk = q_ref.shape[1] // hpb
    dv = v_ref.shape[1] // hpb
    nh = pl.num_programs(1) * hpb
    h_base = pl.program_id(1) * hpb
    row, col = _chunk_masks(ln)
    scale = dk ** -0.5
    o_scrs = (of_scr, ob_scr)

    dist = jnp.abs(row - col).astype(F32)
    iota_col = lax.broadcasted_iota(jnp.int32, (ln, 1), 0).astype(F32)
    consts = []
    for hh in range(hpb):
        sl = slice(hh * dk, (hh + 1) * dk)
        q = q_ref[:, sl]
        k = k_ref[:, sl]
        if has_pos:
            q = q * cos_ref[...] + _rope_swap(q) * sin_ref[...]
            k = k * cos_ref[...] + _rope_swap(k) * sin_ref[...]
        q_scr[:, sl] = q
        k_scr[:, sl] = k * scale
        for d in range(2):
            incl = _dir_masks(row, col, d)[0]
            lg = _log_sigmoid(jnp.zeros((1, 1), F32) + lg_ref[d * nh + h_base + hh])
            order = iota_col if d == 0 else (ln - 1.0) - iota_col
            dmask = jnp.where(incl, jnp.exp(dist * lg), 0.0)
            consts.append((dmask, jnp.exp((order + 1.0) * lg), jnp.exp((ln - 1.0 - order) * lg),
                           jnp.exp(float(ln) * lg)))

    def load_qkv(hh, n):
        r0 = pl.multiple_of(n * ln, ln)
        return (r0, q_scr[pl.ds(r0, ln), hh * dk:(hh + 1) * dk],
                k_scr[pl.ds(r0, ln), hh * dk:(hh + 1) * dk],
                v_ref[pl.ds(r0, ln), hh * dv:(hh + 1) * dv])

    def step_split(ns, states):
        count = 2 * hpb
        loads = [load_qkv(c // 2, ns[c % 2]) for c in range(count)]
        ss = [_bdot_nt(ld[1], ld[2]) * consts[c][0] for c, ld in enumerate(loads)]
        svs = [_bdot(sm, ld[3]) for sm, ld in zip(ss, loads)]
        qrs = [_bdot(ld[1] * consts[c][1], states[c]) for c, ld in enumerate(loads)]
        kvs = [_bdot_tn(ld[2] * consts[c][2], ld[3]) for c, ld in enumerate(loads)]
        for c in range(count):
            hh, d = c // 2, c % 2
            o_scrs[d][pl.ds(loads[c][0], ln), hh * dv:(hh + 1) * dv] = svs[c] + qrs[c]
        return [consts[c][3] * states[c] + kvs[c] for c in range(count)]

    def step_shared(states):
        count = 2 * hpb
        loads = [load_qkv(hh, 0) for hh in range(hpb)]
        qks = [_bdot_nt(ld[1], ld[2]) for ld in loads]
        svs = [_bdot(jnp.concatenate([qks[hh] * consts[hh * 2 + d][0] for d in range(2)], axis=0),
                     loads[hh][3]) for hh in range(hpb)]
        kvs = [_bdot_tn(jnp.concatenate([loads[hh][2] * consts[hh * 2 + d][2] for d in range(2)],
                                        axis=1), loads[hh][3]) for hh in range(hpb)]
        qrs = [None if states[c] is None else _bdot(loads[c // 2][1] * consts[c][1], states[c])
               for c in range(count)]
        out = []
        for c in range(count):
            hh, d = c // 2, c % 2
            o = svs[hh][d * ln:(d + 1) * ln]
            r_new = kvs[hh][d * dk:(d + 1) * dk]
            if states[c] is not None:
                o = o + qrs[c]
                r_new = r_new + consts[c][3] * states[c]
            o_scrs[d][pl.ds(loads[hh][0], ln), hh * dv:(hh + 1) * dv] = o
            out.append(r_new)
        return out

    def body(i, carry):
        return tuple(step_split((i, nch - 1 - i), carry))

    if has_init:
        init = [r0_ref[0, d, hh] for hh in range(hpb) for d in range(2)]
    else:
        init = [None] * (2 * hpb)
    if nch == 1:
        fin = step_shared(init)
    else:
        fin = _chunk_loop(nch, body, tuple(jnp.zeros((dk, dv), F32) if c is None else c
                                           for c in init))
    for hh in range(hpb):
        if emit_state:
            _write_state_slab(rf_ref, slot_j, hh, fin[hh * 2:hh * 2 + 2])
        sl = slice(hh * dv, (hh + 1) * dv)
        y_ref[:, sl] = _rms_rows(of_scr[:, sl] + ob_scr[:, sl]) * _silu(z_ref[:, sl])


def _retention(main, cols, decay_logit, nb, t_len, rope, init, slot, hpb):
    m = main.shape[0]
    nh = RET_HEADS
    dk = (cols['k'] - cols['q']) // nh
    dv = 2 * dk

    def colspec(width, off):
        return pl.BlockSpec((t_len, hpb * width), lambda b, h: (b, off // (hpb * width) + h))

    smem = pl.BlockSpec(memory_space=pltpu.SMEM)
    in_specs = [colspec(dk, cols['q']), colspec(dk, cols['k']), colspec(dv, cols['v']),
                colspec(dv, cols['z']), smem]
    args = [main, main, main, main, decay_logit.reshape(-1)]
    if rope is not None:
        in_specs += [pl.BlockSpec((t_len, dk), lambda b, h: (0, 0))] * 2
        args += list(rope)
    if init is not None:
        in_specs.append(pl.BlockSpec((1, 2, hpb, dk, dv), lambda b, h: (b, 0, h, 0, 0)))
        args.append(init)
    out_shape = [jax.ShapeDtypeStruct((m, nh * dv), F32)]
    out_specs = [pl.BlockSpec((t_len, hpb * dv), lambda b, h: (b, h))]
    aliases = {}
    if slot is not None:
        slab_shape, slab_spec = _state_slab_out(slot, nb, nh, hpb, dk, dv)
        out_shape.append(slab_shape)
        out_specs.append(slab_spec)
        if slot[2] is not None:
            aliases = {len(args): 1}
            in_specs.append(pl.BlockSpec(memory_space=pl.ANY))
            args.append(slot[2])
    return pl.pallas_call(
        functools.partial(_retention_kernel, t_len, hpb, rope is not None, init is not None,
                          None if slot is None else slot[0], bool(aliases)),
        name="retention",
        grid=(nb, nh // hpb),
        in_specs=in_specs,
        out_specs=out_specs,
        out_shape=out_shape,
        input_output_aliases=aliases,
        scratch_shapes=[pltpu.VMEM((t_len, hpb * dv), F32), pltpu.VMEM((t_len, hpb * dv), F32),
                        pltpu.VMEM((t_len, hpb * dk), F32), pltpu.VMEM((t_len, hpb * dk), F32)],
        compiler_params=_cparams(("parallel", "parallel")),
    )(*args)


def _rope_tables(t_len, dk):
    quarter = dk // 4
    t_idx = jnp.arange(t_len)
    row = (t_idx // GRID_W).astype(F32)
    colp = (t_idx % GRID_W).astype(F32)
    freq = ROPE_BASE ** (-jnp.arange(quarter, dtype=F32) / quarter)
    ar = row[:, None] * freq[None]
    ac = colp[:, None] * freq[None]
    cos = jnp.concatenate([jnp.cos(ar), jnp.cos(ar), jnp.cos(ac), jnp.cos(ac)], axis=1)
    sin = jnp.concatenate([-jnp.sin(ar), jnp.sin(ar), -jnp.sin(ac), jnp.sin(ac)], axis=1)
    return cos, sin


def _l2n(x):
    return x * lax.rsqrt(jnp.sum(x * x, axis=-1, keepdims=True) + EPS)


_INV_BASE = 8


def _unit_triangular_inverses(neg_as, row, col):
    ln = neg_as[0].shape[0]
    count = len(neg_as)

    def same_block(size):
        return (row // size) == (col // size)

    base = same_block(_INV_BASE)
    ps = [jnp.where(base, a, 0.0) for a in neg_as]
    invs = [jnp.where(row == col, 1.0, 0.0) + p for p in ps]
    n_sq = int(math.log2(_INV_BASE)) - 1
    ps = [_bdot(p, p) for p in ps]
    for i in range(n_sq):
        if i == n_sq - 1:
            invs = [invs[c] + _bdot(invs[c], ps[c]) for c in range(count)]
        else:
            boths = [_bdot(jnp.concatenate([invs[c], ps[c]], axis=0), ps[c]) for c in range(count)]
            invs = [invs[c] + boths[c][:ln] for c in range(count)]
            ps = [boths[c][ln:] for c in range(count)]
    size = _INV_BASE
    while size < ln:
        sel = same_block(2 * size) & jnp.logical_not(same_block(size))
        mids = [_bdot(invs[c], jnp.where(sel, neg_as[c], 0.0)) for c in range(count)]
        invs = [invs[c] + _bdot(mids[c], invs[c]) for c in range(count)]
        size *= 2
    return invs


def _deltanet_kernel(t_len, hpb, has_init, slot_j, has_prev, *refs):
    (q_ref, k_ref, v_ref, z_ref, a0_ref, a1_ref, b0_ref, b1_ref, cw_ref,
     alog_ref, dtb_ref, ng_ref) = refs[:12]
    pos = 12
    if has_init:
        s0_ref = refs[pos]
        pos += 1
    pos += int(has_prev)
    y_ref = refs[pos]
    pos += 1
    emit_state = slot_j is not None
    if emit_state:
        sf_ref = refs[pos]
        pos += 1
    of_scr, ob_scr, q_scr, k_scr, v_scr = refs[pos:pos + 5]

    ln = DN_CHUNK
    nch = t_len // ln
    dk = q_ref.shape[1] // hpb
    dv = v_ref.shape[1] // hpb
    nh = pl.num_programs(1) * hpb
    h_base = pl.program_id(1) * hpb
    row, col = _chunk_masks(ln)
    eye = row == col
    o_scrs = (of_scr, ob_scr)
    a_refs = (a0_ref, a1_ref)
    b_refs = (b0_ref, b1_ref)
    masks = [_dir_masks(row, col, d) for d in range(2)]

    trow = lax.broadcasted_iota(jnp.int32, (t_len, 1), 0)

    def conv_silu(x, w):
        prev = jnp.where(trow == 0, 0.0, pltpu.roll(x, 1, 0))
        nxt = jnp.where(trow == t_len - 1, 0.0, pltpu.roll(x, t_len - 1, 0))
        return _silu(prev * w[0:1, :] + x * w[1:2, :] + nxt * w[2:3, :])

    for hh in range(hpb):
        sl = slice(hh * dk, (hh + 1) * dk)
        q_scr[:, sl] = _l2n(conv_silu(q_ref[:, sl], cw_ref[hh, 0])) * (dk ** -0.5)
        k_scr[:, sl] = _l2n(conv_silu(k_ref[:, sl], cw_ref[hh, 1]))
        v_scr[:, sl] = conv_silu(v_ref[:, sl], cw_ref[hh, 2])

    def load_qkv(hh, n):
        r0 = pl.multiple_of(n * ln, ln)
        return (r0, q_scr[pl.ds(r0, ln), hh * dk:(hh + 1) * dk],
                k_scr[pl.ds(r0, ln), hh * dk:(hh + 1) * dk],
                v_scr[pl.ds(r0, ln), hh * dv:(hh + 1) * dv])

    def gate_terms(hh, d, n):
        incl, incl_t, _ = masks[d]
        a_row = a_refs[d][hh, 0, pl.ds(n, 1), :]
        beta_row = _sigmoid(b_refs[d][hh, 0, pl.ds(n, 1), :])
        neg_rate = -jnp.exp(jnp.zeros((1, 1), F32) + alog_ref[d * nh + h_base + hh])
        g_row = neg_rate * _softplus(a_row + dtb_ref[d * nh + h_base + hh])
        gc_col, gc_row = _cumsum_forms(g_row, eye, incl, incl_t)
        g_last = jnp.sum(g_row, axis=1, keepdims=True)
        return dict(beta=_to_col(eye, beta_row), gc_col=gc_col, g_last=g_last,
                    decay=jnp.exp(jnp.where(incl, gc_col - gc_row, -jnp.inf)))

    def neg_a_of(d, g, kbk):
        return jnp.where(masks[d][2], -(kbk * g['decay']), 0.0)

    def chunks_all(ns, states):
        items = []
        for hh in range(hpb):
            if nch == 1:
                r0, q, k, v = load_qkv(hh, 0)
                gs = [gate_terms(hh, d, 0) for d in range(2)]
                both = _bdot_nt(jnp.concatenate([k * gs[0]['beta'], k * gs[1]['beta'], q], axis=0), k)
                for d in range(2):
                    items.append((hh, d, r0, gs[d], both[d * ln:(d + 1) * ln], both[2 * ln:], q, k, v))
            else:
                for d in range(2):
                    r0, q, k, v = load_qkv(hh, ns[d])
                    g = gate_terms(hh, d, ns[d])
                    both = _bdot_nt(jnp.concatenate([k * g['beta'], q], axis=0), k)
                    items.append((hh, d, r0, g, both[:ln], both[ln:], q, k, v))
        invs = _unit_triangular_inverses([neg_a_of(it[1], it[3], it[4]) for it in items], row, col)
        rs = [_bdot(inv, jnp.concatenate([it[8] * it[3]['beta'],
                                          it[7] * it[3]['beta'] * jnp.exp(it[3]['gc_col'])], axis=1))
              for inv, it in zip(invs, items)]
        inter = [None if st is None else
                 _bdot(jnp.concatenate([r[:, dv:], it[6] * jnp.exp(it[3]['gc_col'])], axis=0), st)
                 for r, it, st in zip(rs, items, states)]
        v_news = [r[:, :dv] if x is None else r[:, :dv] - x[:ln] for r, x in zip(rs, inter)]
        boths = [_bdot(jnp.concatenate(
            [it[5] * it[3]['decay'], (it[7] * jnp.exp(it[3]['g_last'] - it[3]['gc_col'])).T], axis=0), vn)
            for it, vn in zip(items, v_news)]
        out = []
        for it, x, both, st in zip(items, inter, boths, states):
            hh, d, r0, g = it[:4]
            o_scrs[d][pl.ds(r0, ln), hh * dv:(hh + 1) * dv] = (
                both[:ln] if x is None else both[:ln] + x[ln:])
            out.append(both[ln:] if st is None else both[ln:] + st * jnp.exp(g['g_last']))
        return out

    def body(i, carry):
        return tuple(chunks_all((i, nch - 1 - i), carry))

    if has_init:
        init = [s0_ref[0, d, hh] for hh in range(hpb) for d in range(2)]
    else:
        init = [None] * (2 * hpb)
    if nch == 1:
        fin = chunks_all((0, 0), init)
    else:
        fin = _chunk_loop(nch, body, tuple(jnp.zeros((dk, dv), F32) if c is None else c
                                           for c in init))
    for hh in range(hpb):
        if emit_state:
            _write_state_slab(sf_ref, slot_j, hh, fin[hh * 2:hh * 2 + 2])
        sl = slice(hh * dv, (hh + 1) * dv)
        o = _rms_rows(of_scr[:, sl] + ob_scr[:, sl]) * ng_ref[...]
        y_ref[:, sl] = o * _silu(z_ref[:, sl])


def _deltanet(main, gr, cols, conv_w, a_log, dt_bias, norm_g, nb, t_len, init, slot, hpb):
    m = main.shape[0]
    nh = DN_HEADS
    dk = norm_g.shape[1]
    dv = dk
    nch = t_len // DN_CHUNK
    gr3 = gr.reshape(gr.shape[0], nb, nch, DN_CHUNK)
    cw = conv_w.reshape(conv_w.shape[0], 3, nh, dk).transpose(2, 1, 0, 3)

    def colspec(off):
        return pl.BlockSpec((t_len, hpb * dk), lambda b, h: (b, off // (hpb * dk) + h))

    def gspec(base):
        return pl.BlockSpec((hpb, 1, nch, DN_CHUNK), lambda b, h: (base // hpb + h, b, 0, 0))

    smem = pl.BlockSpec(memory_space=pltpu.SMEM)
    in_specs = [colspec(cols['q']), colspec(cols['k']), colspec(cols['v']), colspec(cols['z']),
                gspec(0), gspec(nh), gspec(2 * nh), gspec(3 * nh),
                pl.BlockSpec((hpb, 3, 3, dk), lambda b, h: (h, 0, 0, 0)), smem, smem,
                pl.BlockSpec((1, dv), lambda b, h: (0, 0))]
    args = [main, main, main, main, gr3, gr3, gr3, gr3, cw,
            a_log.reshape(-1), dt_bias.reshape(-1), norm_g]
    if init is not None:
        in_specs.append(pl.BlockSpec((1, 2, hpb, dk, dv), lambda b, h: (b, 0, h, 0, 0)))
        args.append(init)
    out_shape = [jax.ShapeDtypeStruct((m, nh * dv), F32)]
    out_specs = [pl.BlockSpec((t_len, hpb * dv), lambda b, h: (b, h))]
    aliases = {}
    if slot is not None:
        slab_shape, slab_spec = _state_slab_out(slot, nb, nh, hpb, dk, dv)
        out_shape.append(slab_shape)
        out_specs.append(slab_spec)
        if slot[2] is not None:
            aliases = {len(args): 1}
            in_specs.append(pl.BlockSpec(memory_space=pl.ANY))
            args.append(slot[2])
    return pl.pallas_call(
        functools.partial(_deltanet_kernel, t_len, hpb, init is not None,
                          None if slot is None else slot[0], bool(aliases)),
        name="deltanet",
        grid=(nb, nh // hpb),
        in_specs=in_specs,
        out_specs=out_specs,
        out_shape=out_shape,
        input_output_aliases=aliases,
        scratch_shapes=[pltpu.VMEM((t_len, hpb * dv), F32), pltpu.VMEM((t_len, hpb * dv), F32),
                        pltpu.VMEM((t_len, hpb * dk), F32), pltpu.VMEM((t_len, hpb * dk), F32),
                        pltpu.VMEM((t_len, hpb * dv), F32)],
        compiler_params=_cparams(("parallel", "parallel")),
    )(*args)


def _pack_cols(w, sizes, order, gate_names):
    offs = {}
    o = 0
    for name, sz in sizes:
        offs[name] = (o, sz)
        o += sz
    main = jnp.concatenate([w[:, offs[n][0]:offs[n][0] + offs[n][1]] for n in order], axis=1)
    gates = jnp.concatenate([w[:, offs[n][0]:offs[n][0] + offs[n][1]] for n in gate_names], axis=1)
    return main.astype(BF16), gates.T.astype(BF16)


def kernel(x_prompt, x_sample, state_s5, state_ml_c, state_ml_n, state_ml_m, state_ret, state_dn,
           c, c_ctx, mod_w, mod_b, norm_pre, norm_post,
           ab_w_in, ab_w_out, s5_lambda_re, s5_lambda_im, s5_log_step, s5_b_re, s5_b_im,
           s5_c_re, s5_c_im, s5_d, s5_w_glu, ml_i_bias, ml_f_bias, ml_norm,
           cd_w_in, cd_w_out, ret_decay_logit, dn_conv, dn_a_log, dn_dt_bias, dn_norm):
    bp, tp, d = x_prompt.shape
    bs, ts, _ = x_sample.shape
    depth = mod_w.shape[0]
    br = s5_d.shape[1]
    g_s5 = s5_lambda_re.shape[2]
    ml_dv = br // ML_HEADS
    ml_dk = ml_dv // 2
    ret_dv = br // RET_HEADS
    ret_dk = ret_dv // 2
    assert tp == SEG and ts % SEG == 0 and (bs * ts // SEG) == SUBLANES and bp % SUBLANES == 0
    nseg = ts // SEG

    cond = jnp.zeros((SUBLANES, d), F32).at[0].set(c_ctx).at[1:1 + bs].set(c)
    mod = _modulation(cond, mod_w, mod_b)
    mod = mod.reshape(depth, SUBLANES, 3, d)
    mod = mod.at[:, :, 1].add(1.0)
    mod = mod[:, :, jnp.array([1, 0, 2])]

    ab_sizes = (('u', br), ('za', br), ('q', ML_HEADS * ml_dk), ('k', ML_HEADS * ml_dk), ('v', br),
                ('o', br), ('ig', 2 * ML_HEADS), ('fg', 2 * ML_HEADS), ('zb', br))
    ab_order = ('u', 'za', 'q', 'k', 'v', 'o', 'zb')
    ab_cols = {'za': 0, 'q': br, 'k': br + ML_HEADS * ml_dk, 'v': br + 2 * ML_HEADS * ml_dk,
               'o': 2 * br + 2 * ML_HEADS * ml_dk, 'z': 3 * br + 2 * ML_HEADS * ml_dk}
    dn_qkv = 3 * br
    cd_sizes = (('rq', RET_HEADS * ret_dk), ('rk', RET_HEADS * ret_dk), ('rv', br), ('zc', br),
                ('qkv', dn_qkv), ('a', 2 * DN_HEADS), ('b', 2 * DN_HEADS), ('zd', br))
    cd_order = ('rq', 'rk', 'rv', 'zc', 'qkv', 'zd')
    ret_cols = {'q': 0, 'k': RET_HEADS * ret_dk, 'v': 2 * RET_HEADS * ret_dk,
                'z': 2 * RET_HEADS * ret_dk + br}
    dn_base = 2 * RET_HEADS * ret_dk + 2 * br
    dn_cols = {'q': dn_base, 'k': dn_base + br, 'v': dn_base + 2 * br, 'z': dn_base + 3 * br}

    rope = _rope_tables(ts, ret_dk)

    xp = x_prompt.reshape(bp * tp, d)
    xs = x_sample.reshape(bs * ts, d)
    new_s5, new_mn, new_mm = [], [], []
    slab_mc = slab_ret = slab_dn = None
    n_slots = depth // 2
    for l in range(depth):
        j = l // 2
        mod_p = mod[l, 0:1]
        mod_s = mod[l, 1:1 + bs]
        g_pre = norm_pre[l][None]
        g_post = norm_post[l][None]
        if l % 2 == 0:
            w_main, w_gt = _pack_cols(ab_w_in[j], ab_sizes, ab_order, ('ig', 'fg'))
            bw, cw, acoef = _s5_weights(s5_lambda_re[j], s5_lambda_im[j], s5_log_step[j],
                                        s5_b_re[j], s5_b_im[j], s5_c_re[j], s5_c_im[j])
            w_glu = s5_w_glu[j].astype(BF16)
            w_out = ab_w_out[j].astype(BF16)
            outs = []
            for (x2, nb, t_len, md, is_sample) in ((xp, bp, tp, mod_p, False), (xs, bs, ts, mod_s, True)):
                rpm = x2.shape[0] // md.shape[0]
                u_tm, main, gr = _inproj(x2, md[:, 0:2], rpm, g_pre, w_main, w_gt, br)
                nrow = x2.shape[0] // SEG
                if is_sample:
                    st = state_s5[:, j].astype(F32)
                    x0 = jnp.zeros((bs, nseg, 2, 2, g_s5, S5_P), F32)
                    x0 = x0.at[:, 0, 0].set(st[:, 0]).at[:, nseg - 1, 1].set(st[:, 1])
                    x0 = _s5_state_to_rows(x0.reshape(bs * nseg, 2, 2, g_s5, S5_P))
                else:
                    x0 = jnp.zeros((nrow, 4 * g_s5 * S5_P), F32)
                y_tm, xf = _s5_scan(u_tm.reshape(SEG, nrow, br), bw, cw, acoef, x0,
                                    nseg if is_sample else 1)
                ya = _s5_glu(y_tm.reshape(SEG, nrow * br), u_tm, main, s5_d[j][None], w_glu)
                init = None
                if is_sample:
                    init = (state_ml_c[:, j].astype(F32), state_ml_n[:, j].astype(F32),
                            state_ml_m[:, j].astype(F32))
                res = _mlstm(main, gr, ab_cols, ml_i_bias[j], ml_f_bias[j], ml_norm[j][None],
                             nb, t_len, init, None if is_sample else (j, n_slots, slab_mc),
                             2 if is_sample else ML_HEADS)
                yb = res[0]
                if not is_sample:
                    new_s5.append(_s5_rows_to_state(xf, g_s5, S5_P))
                    slab_mc = res[1]
                    new_mn.append(res[2][:, :, :, 0, :])
                    new_mm.append(res[3][:, :, :, 0, 0])
                outs.append(_outproj(ya, yb, w_out, x2, md, rpm, g_post))
            xp, xs = outs
        else:
            w_main, w_gt = _pack_cols(cd_w_in[j], cd_sizes, cd_order, ('a', 'b'))
            w_out = cd_w_out[j].astype(BF16)
            outs = []
            for (x2, nb, t_len, md, is_sample) in ((xp, bp, tp, mod_p, False), (xs, bs, ts, mod_s, True)):
                rpm = x2.shape[0] // md.shape[0]
                main, gr = _inproj(x2, md[:, 0:2], rpm, g_pre, w_main, w_gt, 0)
                res_c = _retention(main, ret_cols, ret_decay_logit[j], nb, t_len,
                                   rope if is_sample else None,
                                   state_ret[:, j].astype(F32) if is_sample else None,
                                   None if is_sample else (j, n_slots, slab_ret),
                                   2 if is_sample else RET_HEADS)
                res_d = _deltanet(main, gr, dn_cols, dn_conv[j], dn_a_log[j], dn_dt_bias[j],
                                  dn_norm[j][None], nb, t_len,
                                  state_dn[:, j].astype(F32) if is_sample else None,
                                  None if is_sample else (j, n_slots, slab_dn),
                                  4 if is_sample else DN_HEADS)
                if not is_sample:
                    slab_ret = res_c[1]
                    slab_dn = res_d[1]
                outs.append(_outproj(res_c[0], res_d[0], w_out, x2, md, rpm, g_post))
            xp, xs = outs
    return (xp.reshape(bp, tp, d), xs.reshape(bs, ts, d), jnp.stack(new_s5, 1), slab_mc,
            jnp.stack(new_mn, 1), jnp.stack(new_mm, 1), slab_ret, slab_dn)
```

```python
import functools
import math

import jax
import jax.numpy as jnp
from jax import lax
from jax.experimental import pallas as pl
from jax.experimental.pallas import tpu as pltpu

F32 = jnp.float32
BF16 = jnp.bfloat16

EPS = 1e-6
GRID_W = 64
ROPE_BASE = 10000.0
S5_GROUP = 16
S5_P = 64
ML_HEADS = 4
RET_HEADS = 4
DN_HEADS = 8
LANES = 128
SUBLANES = 8
SEG = 256
ML_CHUNK = 256
DN_CHUNK = 256
VMEM_LIMIT = 56 * 1024 * 1024


def _cparams(sem):
    return pltpu.CompilerParams(dimension_semantics=sem, vmem_limit_bytes=VMEM_LIMIT)


def _bdot(a, b):
    return jnp.dot(a.astype(BF16), b.astype(BF16), preferred_element_type=F32)


def _bdot_nt(a, b):
    return lax.dot_general(a.astype(BF16), b.astype(BF16), (((1,), (1,)), ((), ())),
                           preferred_element_type=F32)


def _bdot_tn(a, b):
    return lax.dot_general(a.astype(BF16), b.astype(BF16), (((0,), (0,)), ((), ())),
                           preferred_element_type=F32)


def _sigmoid(x):
    return 1.0 / (1.0 + jnp.exp(-x))


def _silu(x):
    return x * _sigmoid(x)


def _softplus(x):
    return jnp.maximum(x, 0.0) + jnp.log(1.0 + jnp.exp(-jnp.abs(x)))


def _log_sigmoid(x):
    return -_softplus(-x)


def _gelu_tanh(x):
    return 0.5 * x * (1.0 + jnp.tanh(math.sqrt(2.0 / math.pi) * (x + 0.044715 * (x * x * x))))


def _rms_rows(x):
    return x * lax.rsqrt(jnp.mean(x * x, axis=-1, keepdims=True) + EPS)


def _mod_kernel(cond_ref, w_ref, b_ref, o_ref):
    o_ref[0] = _bdot(_silu(cond_ref[...]), w_ref[0]) + b_ref[0]


def _modulation(cond, mod_w, mod_b):
    depth, d, n3 = mod_w.shape
    tn = 1024
    return pl.pallas_call(
        _mod_kernel, name="modulation",
        grid=(depth, n3 // tn),
        in_specs=[pl.BlockSpec((SUBLANES, d), lambda l, n: (0, 0)),
                  pl.BlockSpec((1, d, tn), lambda l, n: (l, 0, n)),
                  pl.BlockSpec((1, 1, tn), lambda l, n: (l, 0, n))],
        out_specs=pl.BlockSpec((1, SUBLANES, tn), lambda l, n: (l, 0, n)),
        out_shape=jax.ShapeDtypeStruct((depth, SUBLANES, n3), F32),
        compiler_params=_cparams(("parallel", "parallel")),
    )(cond, mod_w, mod_b.reshape(depth, 1, n3))


def _inproj_kernel(n_tm, z_off, x_ref, mod_ref, g_ref, w_ref, wgt_ref, *out_refs):
    x = x_ref[...]
    h = _rms_rows(x) * g_ref[...]
    h = (h * mod_ref[0, 0:1, :] + mod_ref[0, 1:2, :]).astype(BF16)
    tm_ref = out_refs[0] if n_tm else None
    main_ref, z_ref, gr_ref = out_refs[-3:]
    step = 1024
    n_main = main_ref.shape[1]
    for n0 in range(0, n_tm, step):
        tm_ref[:, n0:n0 + step] = jnp.dot(h, w_ref[:, n0:n0 + step], preferred_element_type=F32)
    for n0 in range(0, n_main, step):
        main_ref[:, n0:n0 + step] = jnp.dot(h, w_ref[:, n_tm + n0:n_tm + n0 + step],
                                            preferred_element_type=F32)
    tail = jnp.dot(h, w_ref[:, n_tm + n_main:], preferred_element_type=F32)
    z_ref[...] = tail[:, z_off:z_off + z_ref.shape[1]]
    gr_ref[...] = lax.dot_general(wgt_ref[...], h, (((1,), (1,)), ((), ())),
                                  preferred_element_type=F32)


def _inproj(x2, mod, rows_per_mod, gain, w, wgt, n_tm, n_main, z_off, n_z):
    m, d = x2.shape
    n = w.shape[1]
    ng = wgt.shape[0]
    nt = m // SEG
    tiles_per_mod = rows_per_mod // SEG
    out_shape, out_specs = [], []
    if n_tm:
        out_shape.append(jax.ShapeDtypeStruct((SEG, nt * n_tm), F32))
        out_specs.append(pl.BlockSpec((SEG, n_tm), lambda i: (0, i)))
    out_shape += [jax.ShapeDtypeStruct((m, n_main), F32), jax.ShapeDtypeStruct((m, n_z), F32),
                  jax.ShapeDtypeStruct((ng, m), F32)]
    out_specs += [pl.BlockSpec((SEG, n_main), lambda i: (i, 0)),
                  pl.BlockSpec((SEG, n_z), lambda i: (i, 0)),
                  pl.BlockSpec((ng, SEG), lambda i: (0, i))]
    return pl.pallas_call(
        functools.partial(_inproj_kernel, n_tm, z_off), name="inproj",
        grid=(nt,),
        in_specs=[pl.BlockSpec((SEG, d), lambda i: (i, 0)),
                  pl.BlockSpec((1, 2, d), lambda i: (i // tiles_per_mod, 0, 0)),
                  pl.BlockSpec((1, d), lambda i: (0, 0)),
                  pl.BlockSpec((d, n), lambda i: (0, 0), pipeline_mode=pl.Buffered(1)),
                  pl.BlockSpec((ng, d), lambda i: (0, 0))],
        out_specs=out_specs,
        out_shape=out_shape,
        compiler_params=_cparams(("parallel",)),
    )(x2, mod, gain, w, wgt)


def _outproj_kernel(ya_ref, yb_ref, w_ref, x_ref, mod_ref, g_ref, o_ref):
    half = ya_ref.shape[1]
    acc = (jnp.dot(ya_ref[...].astype(BF16), w_ref[0:half, :], preferred_element_type=F32)
           + jnp.dot(yb_ref[...].astype(BF16), w_ref[half:2 * half, :], preferred_element_type=F32))
    o_ref[...] = x_ref[...] + mod_ref[0, 2:3, :] * (_rms_rows(acc) * g_ref[...])


def _outproj(ya, yb, w, x2, mod, rows_per_mod, gain):
    m, d = x2.shape
    br = ya.shape[1]
    tm = 2 * SEG
    tiles_per_mod = rows_per_mod // tm
    return pl.pallas_call(
        _outproj_kernel, name="outproj",
        grid=(m // tm,),
        in_specs=[pl.BlockSpec((tm, br), lambda i: (i, 0)),
                  pl.BlockSpec((tm, br), lambda i: (i, 0)),
                  pl.BlockSpec((2 * br, d), lambda i: (0, 0), pipeline_mode=pl.Buffered(1)),
                  pl.BlockSpec((tm, d), lambda i: (i, 0)),
                  pl.BlockSpec((1, 3, d), lambda i: (i // tiles_per_mod, 0, 0)),
                  pl.BlockSpec((1, d), lambda i: (0, 0))],
        out_specs=pl.BlockSpec((tm, d), lambda i: (i, 0)),
        out_shape=jax.ShapeDtypeStruct((m, d), F32),
        compiler_params=_cparams(("parallel",)),
    )(ya, yb, w, x2, mod, gain)


def _cmul(ar, ai, xr, xi):
    return ar * xr - ai * xi, ar * xi + ai * xr


def _s5_kernel(t_len, nseg, u_ref, bw_ref, cw_ref, a_ref, x0_ref, y_ref, xf_ref, bu_scr):
    hp = xf_ref.shape[1] // 4
    u = u_ref[...].reshape(t_len * SUBLANES, LANES)
    bu_scr[...] = _bdot(u, bw_ref[0])
    a = a_ref[0]
    coef = [jnp.broadcast_to(a[r:r + 1, :], (SUBLANES, hp)) for r in range(4)]

    def scan(init, store):
        def step(t, carry):
            xfr, xfi, xbr, xbi = carry
            rf = pl.multiple_of(t * SUBLANES, SUBLANES)
            rb = pl.multiple_of((t_len - 1 - t) * SUBLANES, SUBLANES)
            pr, pi = _cmul(coef[0], coef[1], xfr, xfi)
            nfr = pr + bu_scr[pl.ds(rf, SUBLANES), 0:hp]
            nfi = pi + bu_scr[pl.ds(rf, SUBLANES), hp:2 * hp]
            pr, pi = _cmul(coef[2], coef[3], xbr, xbi)
            nbr = pr + bu_scr[pl.ds(rb, SUBLANES), 2 * hp:3 * hp]
            nbi = pi + bu_scr[pl.ds(rb, SUBLANES), 3 * hp:4 * hp]
            if store:
                bu_scr[pl.ds(rf, SUBLANES), 0:hp] = nfr
                bu_scr[pl.ds(rf, SUBLANES), hp:2 * hp] = nfi
                bu_scr[pl.ds(rb, SUBLANES), 2 * hp:3 * hp] = nbr
                bu_scr[pl.ds(rb, SUBLANES), 3 * hp:4 * hp] = nbi
            return nfr, nfi, nbr, nbi
        return lax.fori_loop(0, t_len, step, init, unroll=4)

    x0 = tuple(x0_ref[:, r * hp:(r + 1) * hp] for r in range(4))
    if nseg > 1:
        zero = jnp.zeros((SUBLANES, hp), F32)
        ffr, ffi, fbr, fbi = scan((zero, zero, zero, zero), False)
        pfr, pfi, pbr, pbi = coef
        for _ in range(int(math.log2(t_len))):
            pfr, pfi = _cmul(pfr, pfi, pfr, pfi)
            pbr, pbi = _cmul(pbr, pbi, pbr, pbi)
        seg = lax.broadcasted_iota(jnp.int32, (SUBLANES, hp), 0) % nseg
        sfr, sfi, sbr, sbi = x0
        for _ in range(nseg - 1):
            tr, ti = _cmul(pfr, pfi, sfr, sfi)
            sfr = x0[0] + jnp.where(seg >= 1, pltpu.roll(tr + ffr, 1, 0), 0.0)
            sfi = x0[1] + jnp.where(seg >= 1, pltpu.roll(ti + ffi, 1, 0), 0.0)
            tr, ti = _cmul(pbr, pbi, sbr, sbi)
            sbr = x0[2] + jnp.where(seg <= nseg - 2, pltpu.roll(tr + fbr, SUBLANES - 1, 0), 0.0)
            sbi = x0[3] + jnp.where(seg <= nseg - 2, pltpu.roll(ti + fbi, SUBLANES - 1, 0), 0.0)
        x0 = (sfr, sfi, sbr, sbi)
    fin = scan(x0, True)
    for r in range(4):
        xf_ref[:, r * hp:(r + 1) * hp] = fin[r]
    y_t = _bdot_nt(cw_ref[0], bu_scr[...])
    y_ref[...] = y_t.T.reshape(t_len, SUBLANES, LANES)


def _s5_scan(u_tm, bw, cw, acoef, x0, nseg):
    t_len, nb, br = u_tm.shape
    nj = br // LANES
    sw = bw.shape[2]
    assert t_len & (t_len - 1) == 0
    return pl.pallas_call(
        functools.partial(_s5_kernel, t_len, nseg), name="s5_scan",
        grid=(nb // SUBLANES, nj),
        in_specs=[pl.BlockSpec((t_len, SUBLANES, LANES), lambda i, j: (0, i, j)),
                  pl.BlockSpec((1, LANES, sw), lambda i, j: (j, 0, 0)),
                  pl.BlockSpec((1, LANES, sw), lambda i, j: (j, 0, 0)),
                  pl.BlockSpec((1, 4, sw // 4), lambda i, j: (j, 0, 0)),
                  pl.BlockSpec((SUBLANES, sw), lambda i, j: (i, j))],
        out_specs=[pl.BlockSpec((t_len, SUBLANES, LANES), lambda i, j: (0, i, j)),
                   pl.BlockSpec((SUBLANES, sw), lambda i, j: (i, j))],
        out_shape=[jax.ShapeDtypeStruct((t_len, nb, br), F32),
                   jax.ShapeDtypeStruct((nb, nj * sw), F32)],
        scratch_shapes=[pltpu.VMEM((t_len * SUBLANES, sw), F32)],
        compiler_params=_cparams(("parallel", "parallel")),
    )(u_tm, bw, cw, acoef, x0)


def _s5_glu_kernel(y_ref, u_ref, z_ref, d_ref, w_ref, o_ref):
    y = y_ref[...] + d_ref[...] * u_ref[...]
    g = _gelu_tanh(y)
    o_ref[...] = g * _sigmoid(_bdot(g, w_ref[...])) * _silu(z_ref[...])


def _s5_glu(y_tm2, u_tm2, main, d_skip, w_glu):
    br = d_skip.shape[1]
    nt = y_tm2.shape[1] // br
    return pl.pallas_call(
        _s5_glu_kernel, name="s5_glu",
        grid=(nt,),
        in_specs=[pl.BlockSpec((SEG, br), lambda i: (0, i)),
                  pl.BlockSpec((SEG, br), lambda i: (0, i)),
                  pl.BlockSpec((SEG, br), lambda i: (i, 0)),
                  pl.BlockSpec((1, br), lambda i: (0, 0)),
                  pl.BlockSpec((br, br), lambda i: (0, 0))],
        out_specs=pl.BlockSpec((SEG, br), lambda i: (i, 0)),
        out_shape=jax.ShapeDtypeStruct((nt * SEG, br), F32),
        compiler_params=_cparams(("parallel",)),
    )(y_tm2, u_tm2, main, d_skip, w_glu)


def _s5_weights(lam_re, lam_im, log_step, b_re, b_im, c_re, c_im):
    lr, li = lam_re.astype(F32), lam_im.astype(F32)
    step = jnp.exp(log_step.astype(F32))[..., None]
    mag = jnp.exp(lr * step)
    a_r, a_i = mag * jnp.cos(li * step), mag * jnp.sin(li * step)
    den = lr * lr + li * li
    c_r = ((a_r - 1.0) * lr + a_i * li) / den
    c_i = (a_i * lr - (a_r - 1.0) * li) / den
    br_, bi_ = b_re.astype(F32), b_im.astype(F32)
    bbar_r = c_r[..., None] * br_ - c_i[..., None] * bi_
    bbar_i = c_r[..., None] * bi_ + c_i[..., None] * br_
    g = lr.shape[1]
    gl = LANES // S5_GROUP
    nj = g // gl
    eye = jnp.eye(gl, dtype=F32)

    def bw_part(x):
        x = x.reshape(nj, gl, S5_P, S5_GROUP)
        return jnp.einsum('ab,jbps->jasbp', eye, x).reshape(nj, LANES, gl * S5_P)

    def cw_part(x):
        x = x.reshape(nj, gl, S5_GROUP, S5_P)
        return jnp.einsum('ab,jbsp->jbpas', eye, x).reshape(nj, gl * S5_P, LANES)

    bw = jnp.concatenate([bw_part(bbar_r[0]), bw_part(bbar_i[0]),
                          bw_part(bbar_r[1]), bw_part(bbar_i[1])], axis=2)
    cw = jnp.concatenate([cw_part(c_re[0].astype(F32)), cw_part(-c_im[0].astype(F32)),
                          cw_part(c_re[1].astype(F32)), cw_part(-c_im[1].astype(F32))], axis=1)
    acoef = jnp.stack([a_r[0], a_i[0], a_r[1], a_i[1]], axis=0)
    acoef = acoef.reshape(4, nj, gl * S5_P).transpose(1, 0, 2)
    return bw.astype(BF16), cw.transpose(0, 2, 1).astype(BF16), acoef


def _s5_state_to_rows(st):
    nb, _, _, g, p = st.shape
    gl = LANES // S5_GROUP
    nj = g // gl
    return st.reshape(nb, 4, nj, gl * p).transpose(0, 2, 1, 3).reshape(nb, nj * 4 * gl * p)


def _s5_rows_to_state(rows, g, p):
    nb = rows.shape[0]
    gl = LANES // S5_GROUP
    nj = g // gl
    return rows.reshape(nb, nj, 4, gl * p).transpose(0, 2, 1, 3).reshape(nb, 2, 2, g, p)


def _chunk_masks(length):
    row = lax.broadcasted_iota(jnp.int32, (length, length), 0)
    col = lax.broadcasted_iota(jnp.int32, (length, length), 1)
    return row, col


def _to_col(eye, x_row):
    return jnp.sum(jnp.where(eye, x_row, 0.0), axis=1, keepdims=True)


def _cumsum_forms(x_row, eye, incl, incl_t):
    x_col = _to_col(eye, x_row)
    c_col = jnp.sum(jnp.where(incl, x_row, 0.0), axis=1, keepdims=True)
    c_row = jnp.sum(jnp.where(incl_t, x_col, 0.0), axis=0, keepdims=True)
    return c_col, c_row


def _dir_masks(row, col, d):
    if d == 0:
        return col <= row, row <= col, col < row
    return col >= row, row >= col, col > row


def _chunk_loop(nch, body, init):
    return lax.fori_loop(0, nch, body, init, unroll=nch <= 2)


def _state_slab_out(slot, nb, nh, hpb, dk, dv):
    j, n_slots, prev = slot
    shape = jax.ShapeDtypeStruct((nb, n_slots, 2, nh, dk, dv), F32)
    if prev is None:
        return shape, pl.BlockSpec((1, n_slots, 2, hpb, dk, dv), lambda b, h: (b, 0, 0, h, 0, 0))
    return shape, pl.BlockSpec((1, 1, 2, hpb, dk, dv), lambda b, h: (b, j, 0, h, 0, 0))


def _write_state_slab(ref, j, hh, per_dir):
    n_in = ref.shape[1]
    for sl in range(n_in):
        for d in range(2):
            keep = n_in == 1 or sl == j
            ref[0, sl, d, hh] = per_dir[d] if keep else jnp.zeros_like(per_dir[d])


def _mlstm_kernel(t_len, hpb, has_init, slot_j, has_prev, *refs):
    (q_ref, k_ref, v_ref, o_ref, z_ref, ig0_ref, ig1_ref, fg0_ref, fg1_ref,
     ib_ref, fb_ref, ng_ref) = refs[:12]
    pos = 12
    if has_init:
        c0_ref, n0_ref, m0_ref = refs[pos:pos + 3]
        pos += 3
    pos += int(has_prev)
    y_ref = refs[pos]
    pos += 1
    emit_state = slot_j is not None
    if emit_state:
        cf_ref, nf_ref, mf_ref = refs[pos:pos + 3]
        pos += 3
    hf_scr, hb_scr = refs[pos:pos + 2]

    ln = ML_CHUNK
    nch = t_len // ln
    dk = q_ref.shape[1] // hpb
    dv = v_ref.shape[1] // hpb
    b_id = pl.program_id(0)
    nh = pl.num_programs(1) * hpb
    h_base = pl.program_id(1) * hpb
    row, col = _chunk_masks(ln)
    eye = row == col
    scale = dk ** -0.5
    ig_refs = (ig0_ref, ig1_ref)
    fg_refs = (fg0_ref, fg1_ref)
    h_scrs = (hf_scr, hb_scr)
    masks = [_dir_masks(row, col, d) for d in range(2)]

    def gate_terms(hh, d, n, m_st):
        incl, incl_t, _ = masks[d]
        i_row = ig_refs[d][hh, 0, pl.ds(n, 1), :] + ib_ref[d * nh + h_base + hh]
        lf_row = _log_sigmoid(fg_refs[d][hh, 0, pl.ds(n, 1), :] + fb_ref[d * nh + h_base + hh])
        i_col = _to_col(eye, i_row)
        b_col, b_row = _cumsum_forms(lf_row, eye, incl, incl_t)
        c_row = i_row - b_row
        c_col = i_col - b_col
        cm = jnp.where(incl, c_row, -jnp.inf)
        mu = jnp.maximum(m_st, jnp.max(cm, axis=1, keepdims=True))
        mu_last = jnp.maximum(m_st, jnp.max(c_row, axis=1, keepdims=True))
        b_last = jnp.sum(lf_row, axis=1, keepdims=True)
        return dict(e=jnp.exp(cm - mu), inter=jnp.exp(m_st - mu), floor=jnp.exp(-b_col - mu),
                    w=jnp.exp(c_col - mu_last), dec=jnp.exp(m_st - mu_last), m_new=b_last + mu_last)

    def load_qkv(hh, n):
        r0 = pl.multiple_of(n * ln, ln)
        return (r0, q_ref[pl.ds(r0, ln), hh * dk:(hh + 1) * dk],
                k_ref[pl.ds(r0, ln), hh * dk:(hh + 1) * dk] * scale,
                v_ref[pl.ds(r0, ln), hh * dv:(hh + 1) * dv])

    def finish(hh, d, r0, g, s, sv, q, qc, carry):
        num, den = sv, jnp.sum(s, axis=1, keepdims=True)
        if carry is not None:
            num = num + g['inter'] * qc
            den = den + g['inter'] * jnp.sum(q * carry[1], axis=1, keepdims=True)
        h_scrs[d][pl.ds(r0, ln), hh * dv:(hh + 1) * dv] = num / jnp.maximum(jnp.abs(den), g['floor'])

    def new_state(g, kw, kv, carry):
        n_add = jnp.sum(kw, axis=0, keepdims=True)
        if carry is None:
            return kv, n_add, g['m_new']
        return g['dec'] * carry[0] + kv, g['dec'] * carry[1] + n_add, g['m_new']

    def state_products(qs, carries):
        return [None if c is None else _bdot(q, c[0]) for q, c in zip(qs, carries)]

    def step_split(ns, carries):
        items = [(hh, d) for hh in range(hpb) for d in range(2)]
        loads = [load_qkv(hh, ns[d]) for hh, d in items]
        gs = [gate_terms(hh, d, ns[d], carries[c][2]) for c, (hh, d) in enumerate(items)]
        ss = [_bdot_nt(ld[1], ld[2]) * g['e'] for ld, g in zip(loads, gs)]
        svs = [_bdot(sm, ld[3]) for sm, ld in zip(ss, loads)]
        qcs = state_products([ld[1] for ld in loads], carries)
        kws = [ld[2] * g['w'] for ld, g in zip(loads, gs)]
        kvs = [_bdot_tn(kw, ld[3]) for kw, ld in zip(kws, loads)]
        out = []
        for c, (hh, d) in enumerate(items):
            finish(hh, d, loads[c][0], gs[c], ss[c], svs[c], loads[c][1], qcs[c], carries[c])
            out.append(new_state(gs[c], kws[c], kvs[c], carries[c]))
        return out

    def step_shared(carries):
        loads = [load_qkv(hh, 0) for hh in range(hpb)]
        zero_m = jnp.zeros((1, 1), F32)
        gs = [gate_terms(c // 2, c % 2, 0, zero_m if carries[c] is None else carries[c][2])
              for c in range(2 * hpb)]
        qks = [_bdot_nt(ld[1], ld[2]) for ld in loads]
        ss = [qks[c // 2] * gs[c]['e'] for c in range(2 * hpb)]
        svs = [_bdot(jnp.concatenate(ss[2 * hh:2 * hh + 2], axis=0), loads[hh][3])
               for hh in range(hpb)]
        qcs = state_products([loads[c // 2][1] for c in range(2 * hpb)], carries)
        kws = [loads[c // 2][2] * gs[c]['w'] for c in range(2 * hpb)]
        kvs = [_bdot_tn(jnp.concatenate(kws[2 * hh:2 * hh + 2], axis=1), loads[hh][3])
               for hh in range(hpb)]
        out = []
        for c in range(2 * hpb):
            hh, d = c // 2, c % 2
            finish(hh, d, loads[hh][0], gs[c], ss[c], svs[hh][d * ln:(d + 1) * ln], loads[hh][1],
                   qcs[c], carries[c])
            out.append(new_state(gs[c], kws[c], kvs[hh][d * dk:(d + 1) * dk], carries[c]))
        return out

    def body(i, carry):
        return tuple(step_split((i, nch - 1 - i), carry))

    init = []
    for hh in range(hpb):
        for d in range(2):
            if has_init:
                m0 = jnp.zeros((1, 1), F32) + m0_ref[(b_id * 2 + d) * nh + h_base + hh]
                init.append((c0_ref[0, d, hh], n0_ref[0, d, hh], m0))
            else:
                init.append(None)
    if nch == 1:
        fin = step_shared(init)
    else:
        zero_state = (jnp.zeros((dk, dv), F32), jnp.zeros((1, dk), F32), jnp.zeros((1, 1), F32))
        fin = _chunk_loop(nch, body, tuple(zero_state if c is None else c for c in init))
    if emit_state:
        for hh in range(hpb):
            _write_state_slab(cf_ref, slot_j, hh, [fin[hh * 2 + d][0] for d in range(2)])
            for d in range(2):
                nf_ref[0, d, hh] = fin[hh * 2 + d][1]
                mf_ref[0, d, hh] = jnp.broadcast_to(fin[hh * 2 + d][2], (1, LANES))
    for hh in range(hpb):
        sl = slice(hh * dv, (hh + 1) * dv)
        h = _rms_rows(hf_scr[:, sl] + hb_scr[:, sl]) * ng_ref[:, sl]
        y_ref[:, sl] = _sigmoid(o_ref[:, sl]) * h * _silu(z_ref[:, sl])


def _mlstm(main, zarr, gr, cols, i_bias, f_bias, norm_g, nb, t_len, init, slot, hpb):
    m = main.shape[0]
    nh = ML_HEADS
    dv = norm_g.shape[1] // nh
    dk = dv // 2
    nch = t_len // ML_CHUNK
    gr3 = gr.reshape(gr.shape[0], nb, nch, ML_CHUNK)

    def colspec(width, off):
        return pl.BlockSpec((t_len, hpb * width), lambda b, h: (b, off // (hpb * width) + h))

    def gspec(base):
        return pl.BlockSpec((hpb, 1, nch, ML_CHUNK), lambda b, h: (base // hpb + h, b, 0, 0))

    smem = pl.BlockSpec(memory_space=pltpu.SMEM)
    in_specs = [colspec(dk, cols['q']), colspec(dk, cols['k']), colspec(dv, cols['v']),
                colspec(dv, cols['o']), colspec(dv, 0),
                gspec(0), gspec(nh), gspec(2 * nh), gspec(3 * nh), smem, smem,
                pl.BlockSpec((1, hpb * dv), lambda b, h: (0, h))]
    args = [main, main, main, main, zarr, gr3, gr3, gr3, gr3,
            i_bias.reshape(-1), f_bias.reshape(-1), norm_g]
    if init is not None:
        c0, n0, m0 = init
        in_specs += [pl.BlockSpec((1, 2, hpb, dk, dv), lambda b, h: (b, 0, h, 0, 0)),
                     pl.BlockSpec((1, 2, hpb, 1, dk), lambda b, h: (b, 0, h, 0, 0)), smem]
        args += [c0, n0.reshape(nb, 2, nh, 1, dk), m0.reshape(-1)]
    out_shape = [jax.ShapeDtypeStruct((m, nh * dv), F32)]
    out_specs = [pl.BlockSpec((t_len, hpb * dv), lambda b, h: (b, h))]
    aliases = {}
    if slot is not None:
        slab_shape, slab_spec = _state_slab_out(slot, nb, nh, hpb, dk, dv)
        out_shape += [slab_shape, jax.ShapeDtypeStruct((nb, 2, nh, 1, dk), F32),
                      jax.ShapeDtypeStruct((nb, 2, nh, 1, LANES), F32)]
        out_specs += [slab_spec, pl.BlockSpec((1, 2, hpb, 1, dk), lambda b, h: (b, 0, h, 0, 0)),
                      pl.BlockSpec((1, 2, hpb, 1, LANES), lambda b, h: (b, 0, h, 0, 0))]
        if slot[2] is not None:
            aliases = {len(args): 1}
            in_specs.append(pl.BlockSpec(memory_space=pl.ANY))
            args.append(slot[2])
    return pl.pallas_call(
        functools.partial(_mlstm_kernel, t_len, hpb, init is not None,
                          None if slot is None else slot[0], bool(aliases)), name="mlstm",
        grid=(nb, nh // hpb),
        in_specs=in_specs,
        out_specs=out_specs,
        out_shape=out_shape,
        input_output_aliases=aliases,
        scratch_shapes=[pltpu.VMEM((t_len, hpb * dv), F32), pltpu.VMEM((t_len, hpb * dv), F32)],
        compiler_params=_cparams(("parallel", "parallel")),
    )(*args)


def _rope_swap(x):
    quarter = x.shape[1] // 4
    lane = lax.broadcasted_iota(jnp.int32, x.shape, 1)
    up = pltpu.roll(x, x.shape[1] - quarter, 1)
    down = pltpu.roll(x, quarter, 1)
    return jnp.where((lane // quarter) % 2 == 0, up, down)


def _retention_kernel(t_len, hpb, has_pos, has_init, slot_j, has_prev, *refs):
    q_ref, k_ref, v_ref, z_ref, lg_ref = refs[:5]
    pos = 5
    if has_pos:
        cos_ref, sin_ref = refs[pos:pos + 2]
        pos += 2
    if has_init:
        r0_ref = refs[pos]
        pos += 1
    pos += int(has_prev)
    y_ref = refs[pos]
    pos += 1
    emit_state = slot_j is not None
    if emit_state:
        rf_ref = refs[pos]
        pos += 1
    of_scr, ob_scr, q_scr, k_scr = refs[pos:pos + 4]

    ln = ML_CHUNK
    nch = t_len // ln
    dk = q_ref.shape[1] // hpb
    dv = v_ref.shape[1] // hpb
    nh = pl.num_programs(1) * hpb
    h_base = pl.program_id(1) * hpb
    row, col = _chunk_masks(ln)
    scale = dk ** -0.5
    o_scrs = (of_scr, ob_scr)

    dist = jnp.abs(row - col).astype(F32)
    iota_col = lax.broadcasted_iota(jnp.int32, (ln, 1), 0).astype(F32)
    consts = []
    for hh in range(hpb):
        sl = slice(hh * dk, (hh + 1) * dk)
        q = q_ref[:, sl]
        k = k_ref[:, sl]
        if has_pos:
            q = q * cos_ref[...] + _rope_swap(q) * sin_ref[...]
            k = k * cos_ref[...] + _rope_swap(k) * sin_ref[...]
        q_scr[:, sl] = q
        k_scr[:, sl] = k * scale
        for d in range(2):
            incl = _dir_masks(row, col, d)[0]
            lg = _log_sigmoid(jnp.zeros((1, 1), F32) + lg_ref[d * nh + h_base + hh])
            order = iota_col if d == 0 else (ln - 1.0) - iota_col
            dmask = jnp.where(incl, jnp.exp(dist * lg), 0.0)
            consts.append((dmask, jnp.exp((order + 1.0) * lg), jnp.exp((ln - 1.0 - order) * lg),
                           jnp.exp(float(ln) * lg)))

    def load_qkv(hh, n):
        r0 = pl.multiple_of(n * ln, ln)
        return (r0, q_scr[pl.ds(r0, ln), hh * dk:(hh + 1) * dk],
                k_scr[pl.ds(r0, ln), hh * dk:(hh + 1) * dk],
                v_ref[pl.ds(r0, ln), hh * dv:(hh + 1) * dv])

    def step_split(ns, states):
        count = 2 * hpb
        loads = [load_qkv(c // 2, ns[c % 2]) for c in range(count)]
        ss = [_bdot_nt(ld[1], ld[2]) * consts[c][0] for c, ld in enumerate(loads)]
        svs = [_bdot(sm, ld[3]) for sm, ld in zip(ss, loads)]
        qrs = [_bdot(ld[1] * consts[c][1], states[c]) for c, ld in enumerate(loads)]
        kvs = [_bdot_tn(ld[2] * consts[c][2], ld[3]) for c, ld in enumerate(loads)]
        for c in range(count):
            hh, d = c // 2, c % 2
            o_scrs[d][pl.ds(loads[c][0], ln), hh * dv:(hh + 1) * dv] = svs[c] + qrs[c]
        return [consts[c][3] * states[c] + kvs[c] for c in range(count)]

    def step_shared(states):
        count = 2 * hpb
        loads = [load_qkv(hh, 0) for hh in range(hpb)]
        qks = [_bdot_nt(ld[1], ld[2]) for ld in loads]
        svs = [_bdot(jnp.concatenate([qks[hh] * consts[hh * 2 + d][0] for d in range(2)], axis=0),
                     loads[hh][3]) for hh in range(hpb)]
        kvs = [_bdot_tn(jnp.concatenate([loads[hh][2] * consts[hh * 2 + d][2] for d in range(2)],
                                        axis=1), loads[hh][3]) for hh in range(hpb)]
        qrs = [None if states[c] is None else _bdot(loads[c // 2][1] * consts[c][1], states[c])
               for c in range(count)]
        out = []
        for c in range(count):
            hh, d = c // 2, c % 2
            o = svs[hh][d * ln:(d + 1) * ln]
            r_new = kvs[hh][d * dk:(d + 1) * dk]
            if states[c] is not None:
                o = o + qrs[c]
                r_new = r_new + consts[c][3] * states[c]
            o_scrs[d][pl.ds(loads[hh][0], ln), hh * dv:(hh + 1) * dv] = o
            out.append(r_new)
        return out

    def body(i, carry):
        return tuple(step_split((i, nch - 1 - i), carry))

    if has_init:
        init = [r0_ref[0, d, hh] for hh in range(hpb) for d in range(2)]
    else:
        init = [None] * (2 * hpb)
    if nch == 1:
        fin = step_shared(init)
    else:
        fin = _chunk_loop(nch, body, tuple(jnp.zeros((dk, dv), F32) if c is None else c
                                           for c in init))
    for hh in range(hpb):
        if emit_state:
            _write_state_slab(rf_ref, slot_j, hh, fin[hh * 2:hh * 2 + 2])
        sl = slice(hh * dv, (hh + 1) * dv)
        y_ref[:, sl] = _rms_rows(of_scr[:, sl] + ob_scr[:, sl]) * _silu(z_ref[:, sl])


def _retention(main, cols, decay_logit, nb, t_len, rope, init, slot, hpb):
    m = main.shape[0]
    nh = RET_HEADS
    dk = (cols['k'] - cols['q']) // nh
    dv = 2 * dk

    def colspec(width, off):
        return pl.BlockSpec((t_len, hpb * width), lambda b, h: (b, off // (hpb * width) + h))

    smem = pl.BlockSpec(memory_space=pltpu.SMEM)
    in_specs = [colspec(dk, cols['q']), colspec(dk, cols['k']), colspec(dv, cols['v']),
                colspec(dv, cols['z']), smem]
    args = [main, main, main, main, decay_logit.reshape(-1)]
    if rope is not None:
        in_specs += [pl.BlockSpec((t_len, dk), lambda b, h: (0, 0))] * 2
        args += list(rope)
    if init is not None:
        in_specs.append(pl.BlockSpec((1, 2, hpb, dk, dv), lambda b, h: (b, 0, h, 0, 0)))
        args.append(init)
    out_shape = [jax.ShapeDtypeStruct((m, nh * dv), F32)]
    out_specs = [pl.BlockSpec((t_len, hpb * dv), lambda b, h: (b, h))]
    aliases = {}
    if slot is not None:
        slab_shape, slab_spec = _state_slab_out(slot, nb, nh, hpb, dk, dv)
        out_shape.append(slab_shape)
        out_specs.append(slab_spec)
        if slot[2] is not None:
            aliases = {len(args): 1}
            in_specs.append(pl.BlockSpec(memory_space=pl.ANY))
            args.append(slot[2])
    return pl.pallas_call(
        functools.partial(_retention_kernel, t_len, hpb, rope is not None, init is not None,
                          None if slot is None else slot[0], bool(aliases)),
        name="retention",
        grid=(nb, nh // hpb),
        in_specs=in_specs,
        out_specs=out_specs,
        out_shape=out_shape,
        input_output_aliases=aliases,
        scratch_shapes=[pltpu.VMEM((t_len, hpb * dv), F32), pltpu.VMEM((t_len, hpb * dv), F32),
                        pltpu.VMEM((t_len, hpb * dk), F32), pltpu.VMEM((t_len, hpb * dk), F32)],
        compiler_params=_cparams(("parallel", "parallel")),
    )(*args)


def _rope_tables(t_len, dk):
    quarter = dk // 4
    t_idx = jnp.arange(t_len)
    row = (t_idx // GRID_W).astype(F32)
    colp = (t_idx % GRID_W).astype(F32)
    freq = ROPE_BASE ** (-jnp.arange(quarter, dtype=F32) / quarter)
    ar = row[:, None] * freq[None]
    ac = colp[:, None] * freq[None]
    cos = jnp.concatenate([jnp.cos(ar), jnp.cos(ar), jnp.cos(ac), jnp.cos(ac)], axis=1)
    sin = jnp.concatenate([-jnp.sin(ar), jnp.sin(ar), -jnp.sin(ac), jnp.sin(ac)], axis=1)
    return cos, sin


def _l2n(x):
    return x * lax.rsqrt(jnp.sum(x * x, axis=-1, keepdims=True) + EPS)


_INV_BASE = 8


def _unit_triangular_inverses(neg_as, row, col):
    ln = neg_as[0].shape[0]
    count = len(neg_as)

    def same_block(size):
        return (row // size) == (col // size)

    base = same_block(_INV_BASE)
    ps = [jnp.where(base, a, 0.0) for a in neg_as]
    invs = [jnp.where(row == col, 1.0, 0.0) + p for p in ps]
    n_sq = int(math.log2(_INV_BASE)) - 1
    ps = [_bdot(p, p) for p in ps]
    for i in range(n_sq):
        if i == n_sq - 1:
            invs = [invs[c] + _bdot(invs[c], ps[c]) for c in range(count)]
        else:
            boths = [_bdot(jnp.concatenate([invs[c], ps[c]], axis=0), ps[c]) for c in range(count)]
            invs = [invs[c] + boths[c][:ln] for c in range(count)]
            ps = [boths[c][ln:] for c in range(count)]
    size = _INV_BASE
    while size < ln:
        sel = same_block(2 * size) & jnp.logical_not(same_block(size))
        mids = [_bdot(invs[c], jnp.where(sel, neg_as[c], 0.0)) for c in range(count)]
        invs = [invs[c] + _bdot(mids[c], invs[c]) for c in range(count)]
        size *= 2
    return invs


def _deltanet_kernel(t_len, hpb, has_init, slot_j, has_prev, *refs):
    (q_ref, k_ref, v_ref, z_ref, a0_ref, a1_ref, b0_ref, b1_ref, cw_ref,
     alog_ref, dtb_ref, ng_ref) = refs[:12]
    pos = 12
    if has_init:
        s0_ref = refs[pos]
        pos += 1
    pos += int(has_prev)
    y_ref = refs[pos]
    pos += 1
    emit_state = slot_j is not None
    if emit_state:
        sf_ref = refs[pos]
        pos += 1
    of_scr, ob_scr, q_scr, k_scr, v_scr = refs[pos:pos + 5]

    ln = DN_CHUNK
    nch = t_len // ln
    dk = q_ref.shape[1] // hpb
    dv = v_ref.shape[1] // hpb
    nh = pl.num_programs(1) * hpb
    h_base = pl.program_id(1) * hpb
    row, col = _chunk_masks(ln)
    eye = row == col
    o_scrs = (of_scr, ob_scr)
    a_refs = (a0_ref, a1_ref)
    b_refs = (b0_ref, b1_ref)
    masks = [_dir_masks(row, col, d) for d in range(2)]

    trow = lax.broadcasted_iota(jnp.int32, (t_len, 1), 0)

    def conv_silu(x, w):
        prev = jnp.where(trow == 0, 0.0, pltpu.roll(x, 1, 0))
        nxt = jnp.where(trow == t_len - 1, 0.0, pltpu.roll(x, t_len - 1, 0))
        return _silu(prev * w[0:1, :] + x * w[1:2, :] + nxt * w[2:3, :])

    for hh in range(hpb):
        sl = slice(hh * dk, (hh + 1) * dk)
        q_scr[:, sl] = _l2n(conv_silu(q_ref[:, sl], cw_ref[hh, 0])) * (dk ** -0.5)
        k_scr[:, sl] = _l2n(conv_silu(k_ref[:, sl], cw_ref[hh, 1]))
        v_scr[:, sl] = conv_silu(v_ref[:, sl], cw_ref[hh, 2])

    def load_qkv(hh, n):
        r0 = pl.multiple_of(n * ln, ln)
        return (r0, q_scr[pl.ds(r0, ln), hh * dk:(hh + 1) * dk],
                k_scr[pl.ds(r0, ln), hh * dk:(hh + 1) * dk],
                v_scr[pl.ds(r0, ln), hh * dv:(hh + 1) * dv])

    def gate_terms(hh, d, n):
        incl, incl_t, _ = masks[d]
        a_row = a_refs[d][hh, 0, pl.ds(n, 1), :]
        beta_row = _sigmoid(b_refs[d][hh, 0, pl.ds(n, 1), :])
        neg_rate = -jnp.exp(jnp.zeros((1, 1), F32) + alog_ref[d * nh + h_base + hh])
        g_row = neg_rate * _softplus(a_row + dtb_ref[d * nh + h_base + hh])
        gc_col, gc_row = _cumsum_forms(g_row, eye, incl, incl_t)
        g_last = jnp.sum(g_row, axis=1, keepdims=True)
        return dict(beta=_to_col(eye, beta_row), gc_col=gc_col, g_last=g_last,
                    decay=jnp.exp(jnp.where(incl, gc_col - gc_row, -jnp.inf)))

    def neg_a_of(d, g, kbk):
        return jnp.where(masks[d][2], -(kbk * g['decay']), 0.0)

    def chunks_all(ns, states):
        items = []
        for hh in range(hpb):
            if nch == 1:
                r0, q, k, v = load_qkv(hh, 0)
                gs = [gate_terms(hh, d, 0) for d in range(2)]
                both = _bdot_nt(jnp.concatenate([k * gs[0]['beta'], k * gs[1]['beta'], q], axis=0), k)
                for d in range(2):
                    items.append((hh, d, r0, gs[d], both[d * ln:(d + 1) * ln], both[2 * ln:], q, k, v))
            else:
                for d in range(2):
                    r0, q, k, v = load_qkv(hh, ns[d])
                    g = gate_terms(hh, d, ns[d])
                    both = _bdot_nt(jnp.concatenate([k * g['beta'], q], axis=0), k)
                    items.append((hh, d, r0, g, both[:ln], both[ln:], q, k, v))
        invs = _unit_triangular_inverses([neg_a_of(it[1], it[3], it[4]) for it in items], row, col)
        rs = [_bdot(inv, jnp.concatenate([it[8] * it[3]['beta'],
                                          it[7] * it[3]['beta'] * jnp.exp(it[3]['gc_col'])], axis=1))
              for inv, it in zip(invs, items)]
        inter = [None if st is None else
                 _bdot(jnp.concatenate([r[:, dv:], it[6] * jnp.exp(it[3]['gc_col'])], axis=0), st)
                 for r, it, st in zip(rs, items, states)]
        v_news = [r[:, :dv] if x is None else r[:, :dv] - x[:ln] for r, x in zip(rs, inter)]
        boths = [_bdot(jnp.concatenate(
            [it[5] * it[3]['decay'], (it[7] * jnp.exp(it[3]['g_last'] - it[3]['gc_col'])).T], axis=0), vn)
            for it, vn in zip(items, v_news)]
        out = []
        for it, x, both, st in zip(items, inter, boths, states):
            hh, d, r0, g = it[:4]
            o_scrs[d][pl.ds(r0, ln), hh * dv:(hh + 1) * dv] = (
                both[:ln] if x is None else both[:ln] + x[ln:])
            out.append(both[ln:] if st is None else both[ln:] + st * jnp.exp(g['g_last']))
        return out

    def body(i, carry):
        return tuple(chunks_all((i, nch - 1 - i), carry))

    if has_init:
        init = [s0_ref[0, d, hh] for hh in range(hpb) for d in range(2)]
    else:
        init = [None] * (2 * hpb)
    if nch == 1:
        fin = chunks_all((0, 0), init)
    else:
        fin = _chunk_loop(nch, body, tuple(jnp.zeros((dk, dv), F32) if c is None else c
                                           for c in init))
    for hh in range(hpb):
        if emit_state:
            _write_state_slab(sf_ref, slot_j, hh, fin[hh * 2:hh * 2 + 2])
        sl = slice(hh * dv, (hh + 1) * dv)
        o = _rms_rows(of_scr[:, sl] + ob_scr[:, sl]) * ng_ref[...]
        y_ref[:, sl] = o * _silu(z_ref[:, sl])


def _deltanet(main, zarr, gr, cols, conv_w, a_log, dt_bias, norm_g, nb, t_len, init, slot, hpb):
    m = main.shape[0]
    nh = DN_HEADS
    dk = norm_g.shape[1]
    dv = dk
    nch = t_len // DN_CHUNK
    gr3 = gr.reshape(gr.shape[0], nb, nch, DN_CHUNK)
    cw = conv_w.reshape(conv_w.shape[0], 3, nh, dk).transpose(2, 1, 0, 3)

    def colspec(off):
        return pl.BlockSpec((t_len, hpb * dk), lambda b, h: (b, off // (hpb * dk) + h))

    def gspec(base):
        return pl.BlockSpec((hpb, 1, nch, DN_CHUNK), lambda b, h: (base // hpb + h, b, 0, 0))

    smem = pl.BlockSpec(memory_space=pltpu.SMEM)
    in_specs = [colspec(cols['q']), colspec(cols['k']), colspec(cols['v']), colspec(0),
                gspec(0), gspec(nh), gspec(2 * nh), gspec(3 * nh),
                pl.BlockSpec((hpb, 3, 3, dk), lambda b, h: (h, 0, 0, 0)), smem, smem,
                pl.BlockSpec((1, dv), lambda b, h: (0, 0))]
    args = [main, main, main, zarr, gr3, gr3, gr3, gr3, cw,
            a_log.reshape(-1), dt_bias.reshape(-1), norm_g]
    if init is not None:
        in_specs.append(pl.BlockSpec((1, 2, hpb, dk, dv), lambda b, h: (b, 0, h, 0, 0)))
        args.append(init)
    out_shape = [jax.ShapeDtypeStruct((m, nh * dv), F32)]
    out_specs = [pl.BlockSpec((t_len, hpb * dv), lambda b, h: (b, h))]
    aliases = {}
    if slot is not None:
        slab_shape, slab_spec = _state_slab_out(slot, nb, nh, hpb, dk, dv)
        out_shape.append(slab_shape)
        out_specs.append(slab_spec)
        if slot[2] is not None:
            aliases = {len(args): 1}
            in_specs.append(pl.BlockSpec(memory_space=pl.ANY))
            args.append(slot[2])
    return pl.pallas_call(
        functools.partial(_deltanet_kernel, t_len, hpb, init is not None,
                          None if slot is None else slot[0], bool(aliases)),
        name="deltanet",
        grid=(nb, nh // hpb),
        in_specs=in_specs,
        out_specs=out_specs,
        out_shape=out_shape,
        input_output_aliases=aliases,
        scratch_shapes=[pltpu.VMEM((t_len, hpb * dv), F32), pltpu.VMEM((t_len, hpb * dv), F32),
                        pltpu.VMEM((t_len, hpb * dk), F32), pltpu.VMEM((t_len, hpb * dk), F32),
                        pltpu.VMEM((t_len, hpb * dv), F32)],
        compiler_params=_cparams(("parallel", "parallel")),
    )(*args)


def _pack_weights(w, gate_lo, gate_hi):
    pad = (-w.shape[1]) % LANES
    return (jnp.pad(w.astype(BF16), ((0, 0), (0, pad))), w[:, gate_lo:gate_hi].T.astype(BF16))


def kernel(x_prompt, x_sample, state_s5, state_ml_c, state_ml_n, state_ml_m, state_ret, state_dn,
           c, c_ctx, mod_w, mod_b, norm_pre, norm_post,
           ab_w_in, ab_w_out, s5_lambda_re, s5_lambda_im, s5_log_step, s5_b_re, s5_b_im,
           s5_c_re, s5_c_im, s5_d, s5_w_glu, ml_i_bias, ml_f_bias, ml_norm,
           cd_w_in, cd_w_out, ret_decay_logit, dn_conv, dn_a_log, dn_dt_bias, dn_norm):
    bp, tp, d = x_prompt.shape
    bs, ts, _ = x_sample.shape
    depth = mod_w.shape[0]
    br = s5_d.shape[1]
    g_s5 = s5_lambda_re.shape[2]
    ml_dv = br // ML_HEADS
    ml_dk = ml_dv // 2
    ret_dv = br // RET_HEADS
    ret_dk = ret_dv // 2
    assert tp == SEG and ts % SEG == 0 and (bs * ts // SEG) == SUBLANES and bp % SUBLANES == 0
    nseg = ts // SEG

    cond = jnp.zeros((SUBLANES, d), F32).at[0].set(c_ctx).at[1:1 + bs].set(c)
    mod = _modulation(cond, mod_w, mod_b)
    mod = mod.reshape(depth, SUBLANES, 3, d)
    mod = mod.at[:, :, 1].add(1.0)
    mod = mod[:, :, jnp.array([1, 0, 2])]

    qk_ml = ML_HEADS * ml_dk
    ab_main = 3 * br + 2 * qk_ml
    ab_cols = {'za': 0, 'q': br, 'k': br + qk_ml, 'v': br + 2 * qk_ml, 'o': 2 * br + 2 * qk_ml}
    ab_gates = (br + ab_main, br + ab_main + 4 * ML_HEADS)
    qk_ret = RET_HEADS * ret_dk
    cd_main = 2 * qk_ret + 2 * br + 3 * br
    ret_cols = {'q': 0, 'k': qk_ret, 'v': 2 * qk_ret, 'z': 2 * qk_ret + br}
    dn_base = 2 * qk_ret + 2 * br
    dn_cols = {'q': dn_base, 'k': dn_base + br, 'v': dn_base + 2 * br}
    cd_gates = (cd_main, cd_main + 4 * DN_HEADS)

    rope = _rope_tables(ts, ret_dk)

    xp = x_prompt.reshape(bp * tp, d)
    xs = x_sample.reshape(bs * ts, d)
    new_s5, new_mn, new_mm = [], [], []
    slab_mc = slab_ret = slab_dn = None
    n_slots = depth // 2
    for l in range(depth):
        j = l // 2
        mod_p = mod[l, 0:1]
        mod_s = mod[l, 1:1 + bs]
        g_pre = norm_pre[l][None]
        g_post = norm_post[l][None]
        if l % 2 == 0:
            w_main, w_gt = _pack_weights(ab_w_in[j], *ab_gates)
            bw, cw, acoef = _s5_weights(s5_lambda_re[j], s5_lambda_im[j], s5_log_step[j],
                                        s5_b_re[j], s5_b_im[j], s5_c_re[j], s5_c_im[j])
            w_glu = s5_w_glu[j].astype(BF16)
            w_out = ab_w_out[j].astype(BF16)
            outs = []
            for (x2, nb, t_len, md, is_sample) in ((xp, bp, tp, mod_p, False), (xs, bs, ts, mod_s, True)):
                rpm = x2.shape[0] // md.shape[0]
                u_tm, main, zb, gr = _inproj(x2, md[:, 0:2], rpm, g_pre, w_main, w_gt, br, ab_main,
                                             ab_gates[1] - ab_gates[0], br)
                nrow = x2.shape[0] // SEG
                if is_sample:
                    st = state_s5[:, j].astype(F32)
                    x0 = jnp.zeros((bs, nseg, 2, 2, g_s5, S5_P), F32)
                    x0 = x0.at[:, 0, 0].set(st[:, 0]).at[:, nseg - 1, 1].set(st[:, 1])
                    x0 = _s5_state_to_rows(x0.reshape(bs * nseg, 2, 2, g_s5, S5_P))
                else:
                    x0 = jnp.zeros((nrow, 4 * g_s5 * S5_P), F32)
                y_tm, xf = _s5_scan(u_tm.reshape(SEG, nrow, br), bw, cw, acoef, x0,
                                    nseg if is_sample else 1)
                ya = _s5_glu(y_tm.reshape(SEG, nrow * br), u_tm, main, s5_d[j][None], w_glu)
                init = None
                if is_sample:
                    init = (state_ml_c[:, j].astype(F32), state_ml_n[:, j].astype(F32),
                            state_ml_m[:, j].astype(F32))
                res = _mlstm(main, zb, gr, ab_cols, ml_i_bias[j], ml_f_bias[j], ml_norm[j][None],
                             nb, t_len, init, None if is_sample else (j, n_slots, slab_mc),
                             2 if is_sample else ML_HEADS)
                yb = res[0]
                if not is_sample:
                    new_s5.append(_s5_rows_to_state(xf, g_s5, S5_P))
                    slab_mc = res[1]
                    new_mn.append(res[2][:, :, :, 0, :])
                    new_mm.append(res[3][:, :, :, 0, 0])
                outs.append(_outproj(ya, yb, w_out, x2, md, rpm, g_post))
            xp, xs = outs
        else:
            w_main, w_gt = _pack_weights(cd_w_in[j], *cd_gates)
            w_out = cd_w_out[j].astype(BF16)
            outs = []
            for (x2, nb, t_len, md, is_sample) in ((xp, bp, tp, mod_p, False), (xs, bs, ts, mod_s, True)):
                rpm = x2.shape[0] // md.shape[0]
                main, zd, gr = _inproj(x2, md[:, 0:2], rpm, g_pre, w_main, w_gt, 0, cd_main,
                                       cd_gates[1] - cd_gates[0], br)
                res_c = _retention(main, ret_cols, ret_decay_logit[j], nb, t_len,
                                   rope if is_sample else None,
                                   state_ret[:, j].astype(F32) if is_sample else None,
                                   None if is_sample else (j, n_slots, slab_ret),
                                   2 if is_sample else RET_HEADS)
                res_d = _deltanet(main, zd, gr, dn_cols, dn_conv[j], dn_a_log[j], dn_dt_bias[j],
                                  dn_norm[j][None], nb, t_len,
                                  state_dn[:, j].astype(F32) if is_sample else None,
                                  None if is_sample else (j, n_slots, slab_dn),
                                  4 if is_sample else DN_HEADS)
                if not is_sample:
                    slab_ret = res_c[1]
                    slab_dn = res_d[1]
                outs.append(_outproj(res_c[0], res_d[0], w_out, x2, md, rpm, g_post))
            xp, xs = outs
    return (xp.reshape(bp, tp, d), xs.reshape(bs, ts, d), jnp.stack(new_s5, 1), slab_mc,
            jnp.stack(new_mn, 1), jnp.stack(new_mm, 1), slab_ret, slab_dn)
```

```python
import functools
import math

import jax
import jax.numpy as jnp
from jax import lax
from jax.experimental import pallas as pl
from jax.experimental.pallas import tpu as pltpu

F32 = jnp.float32
BF16 = jnp.bfloat16

EPS = 1e-6
GRID_W = 64
ROPE_BASE = 10000.0
S5_GROUP = 16
S5_P = 64
ML_HEADS = 4
RET_HEADS = 4
DN_HEADS = 8
LANES = 128
SUBLANES = 8
SEG = 256
ML_CHUNK = 256
DN_CHUNK = 256
VMEM_LIMIT = 56 * 1024 * 1024


def _cparams(sem):
    return pltpu.CompilerParams(dimension_semantics=sem, vmem_limit_bytes=VMEM_LIMIT)


def _bdot(a, b):
    return jnp.dot(a.astype(BF16), b.astype(BF16), preferred_element_type=F32)


def _bdot_nt(a, b):
    return lax.dot_general(a.astype(BF16), b.astype(BF16), (((1,), (1,)), ((), ())),
                           preferred_element_type=F32)


def _bdot_tn(a, b):
    return lax.dot_general(a.astype(BF16), b.astype(BF16), (((0,), (0,)), ((), ())),
                           preferred_element_type=F32)


def _sigmoid(x):
    return 1.0 / (1.0 + jnp.exp(-x))


def _silu(x):
    return x * _sigmoid(x)


def _softplus(x):
    return jnp.maximum(x, 0.0) + jnp.log(1.0 + jnp.exp(-jnp.abs(x)))


def _log_sigmoid(x):
    return -_softplus(-x)


def _gelu_tanh(x):
    return 0.5 * x * (1.0 + jnp.tanh(math.sqrt(2.0 / math.pi) * (x + 0.044715 * (x * x * x))))


def _rms_rows(x):
    return x * lax.rsqrt(jnp.mean(x * x, axis=-1, keepdims=True) + EPS)


def _mod_kernel(cond_ref, w_ref, b_ref, o_ref):
    o_ref[0] = _bdot(_silu(cond_ref[...]), w_ref[0]) + b_ref[0]


def _modulation(cond, mod_w, mod_b):
    depth, d, n3 = mod_w.shape
    tn = 1024
    return pl.pallas_call(
        _mod_kernel, name="modulation",
        grid=(depth, n3 // tn),
        in_specs=[pl.BlockSpec((SUBLANES, d), lambda l, n: (0, 0)),
                  pl.BlockSpec((1, d, tn), lambda l, n: (l, 0, n)),
                  pl.BlockSpec((1, 1, tn), lambda l, n: (l, 0, n))],
        out_specs=pl.BlockSpec((1, SUBLANES, tn), lambda l, n: (l, 0, n)),
        out_shape=jax.ShapeDtypeStruct((depth, SUBLANES, n3), F32),
        compiler_params=_cparams(("parallel", "parallel")),
    )(cond, mod_w, mod_b.reshape(depth, 1, n3))


def _inproj_kernel(n_tm, z_off, x_ref, mod_ref, g_ref, w_ref, wgt_ref, *out_refs):
    x = x_ref[...]
    h = _rms_rows(x) * g_ref[...]
    h = (h * mod_ref[0, 0:1, :] + mod_ref[0, 1:2, :]).astype(BF16)
    tm_ref = out_refs[0] if n_tm else None
    main_ref, z_ref, gr_ref = out_refs[-3:]
    step = 1024
    n_main = main_ref.shape[1]
    for n0 in range(0, n_tm, step):
        tm_ref[:, n0:n0 + step] = jnp.dot(h, w_ref[:, n0:n0 + step], preferred_element_type=F32)
    for n0 in range(0, n_main, step):
        main_ref[:, n0:n0 + step] = jnp.dot(h, w_ref[:, n_tm + n0:n_tm + n0 + step],
                                            preferred_element_type=F32)
    tail = jnp.dot(h, w_ref[:, n_tm + n_main:], preferred_element_type=F32)
    z_ref[...] = tail[:, z_off:z_off + z_ref.shape[1]]
    gr_ref[...] = lax.dot_general(wgt_ref[...], h, (((1,), (1,)), ((), ())),
                                  preferred_element_type=F32)


def _inproj(x2, mod, rows_per_mod, gain, w, wgt, n_tm, n_main, z_off, n_z):
    m, d = x2.shape
    n = w.shape[1]
    ng = wgt.shape[0]
    nt = m // SEG
    tiles_per_mod = rows_per_mod // SEG
    out_shape, out_specs = [], []
    if n_tm:
        out_shape.append(jax.ShapeDtypeStruct((SEG, nt * n_tm), F32))
        out_specs.append(pl.BlockSpec((SEG, n_tm), lambda i: (0, i)))
    out_shape += [jax.ShapeDtypeStruct((m, n_main), F32), jax.ShapeDtypeStruct((m, n_z), F32),
                  jax.ShapeDtypeStruct((ng, m), F32)]
    out_specs += [pl.BlockSpec((SEG, n_main), lambda i: (i, 0)),
                  pl.BlockSpec((SEG, n_z), lambda i: (i, 0)),
                  pl.BlockSpec((ng, SEG), lambda i: (0, i))]
    return pl.pallas_call(
        functools.partial(_inproj_kernel, n_tm, z_off), name="inproj",
        grid=(nt,),
        in_specs=[pl.BlockSpec((SEG, d), lambda i: (i, 0)),
                  pl.BlockSpec((1, 2, d), lambda i: (i // tiles_per_mod, 0, 0)),
                  pl.BlockSpec((1, d), lambda i: (0, 0)),
                  pl.BlockSpec((d, n), lambda i: (0, 0), pipeline_mode=pl.Buffered(1)),
                  pl.BlockSpec((ng, d), lambda i: (0, 0))],
        out_specs=out_specs,
        out_shape=out_shape,
        compiler_params=_cparams(("parallel",)),
    )(x2, mod, gain, w, wgt)


def _outproj_kernel(ya_ref, yb_ref, w_ref, x_ref, mod_ref, g_ref, o_ref):
    half = ya_ref.shape[1]
    acc = (jnp.dot(ya_ref[...], w_ref[0:half, :], preferred_element_type=F32)
           + jnp.dot(yb_ref[...], w_ref[half:2 * half, :], preferred_element_type=F32))
    o_ref[...] = x_ref[...] + mod_ref[0, 2:3, :] * (_rms_rows(acc) * g_ref[...])


def _outproj(ya, yb, w, x2, mod, rows_per_mod, gain):
    m, d = x2.shape
    br = ya.shape[1]
    tm = 2 * SEG
    tiles_per_mod = rows_per_mod // tm
    return pl.pallas_call(
        _outproj_kernel, name="outproj",
        grid=(m // tm,),
        in_specs=[pl.BlockSpec((tm, br), lambda i: (i, 0)),
                  pl.BlockSpec((tm, br), lambda i: (i, 0)),
                  pl.BlockSpec((2 * br, d), lambda i: (0, 0), pipeline_mode=pl.Buffered(1)),
                  pl.BlockSpec((tm, d), lambda i: (i, 0)),
                  pl.BlockSpec((1, 3, d), lambda i: (i // tiles_per_mod, 0, 0)),
                  pl.BlockSpec((1, d), lambda i: (0, 0))],
        out_specs=pl.BlockSpec((tm, d), lambda i: (i, 0)),
        out_shape=jax.ShapeDtypeStruct((m, d), F32),
        compiler_params=_cparams(("parallel",)),
    )(ya, yb, w, x2, mod, gain)


def _cmul(ar, ai, xr, xi):
    return ar * xr - ai * xi, ar * xi + ai * xr


def _s5_kernel(t_len, nseg, u_ref, bw_ref, cw_ref, a_ref, x0_ref, y_ref, xf_ref, bu_scr):
    hp = xf_ref.shape[1] // 4
    u = u_ref[...].reshape(t_len * SUBLANES, LANES)
    bu_scr[...] = _bdot(u, bw_ref[0])
    a = a_ref[0]
    coef = [jnp.broadcast_to(a[r:r + 1, :], (SUBLANES, hp)) for r in range(4)]

    def scan(init, store):
        def step(t, carry):
            xfr, xfi, xbr, xbi = carry
            rf = pl.multiple_of(t * SUBLANES, SUBLANES)
            rb = pl.multiple_of((t_len - 1 - t) * SUBLANES, SUBLANES)
            pr, pi = _cmul(coef[0], coef[1], xfr, xfi)
            nfr = pr + bu_scr[pl.ds(rf, SUBLANES), 0:hp]
            nfi = pi + bu_scr[pl.ds(rf, SUBLANES), hp:2 * hp]
            pr, pi = _cmul(coef[2], coef[3], xbr, xbi)
            nbr = pr + bu_scr[pl.ds(rb, SUBLANES), 2 * hp:3 * hp]
            nbi = pi + bu_scr[pl.ds(rb, SUBLANES), 3 * hp:4 * hp]
            if store:
                bu_scr[pl.ds(rf, SUBLANES), 0:hp] = nfr
                bu_scr[pl.ds(rf, SUBLANES), hp:2 * hp] = nfi
                bu_scr[pl.ds(rb, SUBLANES), 2 * hp:3 * hp] = nbr
                bu_scr[pl.ds(rb, SUBLANES), 3 * hp:4 * hp] = nbi
            return nfr, nfi, nbr, nbi
        return lax.fori_loop(0, t_len, step, init, unroll=4)

    x0 = tuple(x0_ref[:, r * hp:(r + 1) * hp] for r in range(4))
    if nseg > 1:
        zero = jnp.zeros((SUBLANES, hp), F32)
        ffr, ffi, fbr, fbi = scan((zero, zero, zero, zero), False)
        pfr, pfi, pbr, pbi = coef
        for _ in range(int(math.log2(t_len))):
            pfr, pfi = _cmul(pfr, pfi, pfr, pfi)
            pbr, pbi = _cmul(pbr, pbi, pbr, pbi)
        seg = lax.broadcasted_iota(jnp.int32, (SUBLANES, hp), 0) % nseg
        sfr, sfi, sbr, sbi = x0
        for _ in range(nseg - 1):
            tr, ti = _cmul(pfr, pfi, sfr, sfi)
            sfr = x0[0] + jnp.where(seg >= 1, pltpu.roll(tr + ffr, 1, 0), 0.0)
            sfi = x0[1] + jnp.where(seg >= 1, pltpu.roll(ti + ffi, 1, 0), 0.0)
            tr, ti = _cmul(pbr, pbi, sbr, sbi)
            sbr = x0[2] + jnp.where(seg <= nseg - 2, pltpu.roll(tr + fbr, SUBLANES - 1, 0), 0.0)
            sbi = x0[3] + jnp.where(seg <= nseg - 2, pltpu.roll(ti + fbi, SUBLANES - 1, 0), 0.0)
        x0 = (sfr, sfi, sbr, sbi)
    fin = scan(x0, True)
    for r in range(4):
        xf_ref[:, r * hp:(r + 1) * hp] = fin[r]
    y_t = _bdot_nt(cw_ref[0], bu_scr[...])
    y_ref[...] = y_t.T.reshape(t_len, SUBLANES, LANES)


def _s5_scan(u_tm, bw, cw, acoef, x0, nseg):
    t_len, nb, br = u_tm.shape
    nj = br // LANES
    sw = bw.shape[2]
    assert t_len & (t_len - 1) == 0
    return pl.pallas_call(
        functools.partial(_s5_kernel, t_len, nseg), name="s5_scan",
        grid=(nb // SUBLANES, nj),
        in_specs=[pl.BlockSpec((t_len, SUBLANES, LANES), lambda i, j: (0, i, j)),
                  pl.BlockSpec((1, LANES, sw), lambda i, j: (j, 0, 0)),
                  pl.BlockSpec((1, LANES, sw), lambda i, j: (j, 0, 0)),
                  pl.BlockSpec((1, 4, sw // 4), lambda i, j: (j, 0, 0)),
                  pl.BlockSpec((SUBLANES, sw), lambda i, j: (i, j))],
        out_specs=[pl.BlockSpec((t_len, SUBLANES, LANES), lambda i, j: (0, i, j)),
                   pl.BlockSpec((SUBLANES, sw), lambda i, j: (i, j))],
        out_shape=[jax.ShapeDtypeStruct((t_len, nb, br), F32),
                   jax.ShapeDtypeStruct((nb, nj * sw), F32)],
        scratch_shapes=[pltpu.VMEM((t_len * SUBLANES, sw), F32)],
        compiler_params=_cparams(("parallel", "parallel")),
    )(u_tm, bw, cw, acoef, x0)


S5_PAIR_CHUNKS = 8


def _s5_pair_kernel(t_len, u_ref, bw_ref, cw_ref, a_ref, x0_ref, y_ref, xf_ref, bu_a, bu_b):
    hp = bu_a.shape[1] // 4
    rows = t_len * SUBLANES
    crow = rows // S5_PAIR_CHUNKS
    csteps = t_len // S5_PAIR_CHUNKS
    scrs = (bu_a, bu_b)
    u2 = u_ref[...].reshape(rows, 2 * LANES)

    def bu_chunk(blk, c):
        lhs = u2[c * crow:(c + 1) * crow, blk * LANES:(blk + 1) * LANES]
        scrs[blk][c * crow:(c + 1) * crow, :] = _bdot(lhs, bw_ref[blk])

    def out_chunk(blk, c):
        y_t = _bdot_nt(cw_ref[blk], scrs[blk][c * crow:(c + 1) * crow, :])
        y_ref[c * csteps:(c + 1) * csteps, :, blk * LANES:(blk + 1) * LANES] = (
            y_t.T.reshape(csteps, SUBLANES, LANES))

    def scan_steps(blk, coef, t0, t1, carry):
        scr = scrs[blk]
        xfr, xfi, xbr, xbi = carry
        for t in range(t0, t1):
            rf, rb = t * SUBLANES, (t_len - 1 - t) * SUBLANES
            pr, pi = _cmul(coef[0], coef[1], xfr, xfi)
            xfr = pr + scr[rf:rf + SUBLANES, 0:hp]
            xfi = pi + scr[rf:rf + SUBLANES, hp:2 * hp]
            pr, pi = _cmul(coef[2], coef[3], xbr, xbi)
            xbr = pr + scr[rb:rb + SUBLANES, 2 * hp:3 * hp]
            xbi = pi + scr[rb:rb + SUBLANES, 3 * hp:4 * hp]
            scr[rf:rf + SUBLANES, 0:hp] = xfr
            scr[rf:rf + SUBLANES, hp:2 * hp] = xfi
            scr[rb:rb + SUBLANES, 2 * hp:3 * hp] = xbr
            scr[rb:rb + SUBLANES, 3 * hp:4 * hp] = xbi
        return xfr, xfi, xbr, xbi

    def scan_with(blk, other_work):
        a = a_ref[blk]
        coef = [jnp.broadcast_to(a[r:r + 1, :], (SUBLANES, hp)) for r in range(4)]
        base = blk * 4 * hp
        carry = tuple(x0_ref[:, base + r * hp:base + (r + 1) * hp] for r in range(4))
        for c in range(S5_PAIR_CHUNKS):
            other_work(c)
            carry = scan_steps(blk, coef, c * csteps, (c + 1) * csteps, carry)
        for r in range(4):
            xf_ref[:, base + r * hp:base + (r + 1) * hp] = carry[r]

    for c in range(S5_PAIR_CHUNKS):
        bu_chunk(0, c)
    scan_with(0, lambda c: bu_chunk(1, c))
    scan_with(1, lambda c: out_chunk(0, c))
    for c in range(S5_PAIR_CHUNKS):
        out_chunk(1, c)


def _s5_scan_pair(u_tm, bw, cw, acoef, x0):
    t_len, nb, br = u_tm.shape
    nj = br // LANES
    sw = bw.shape[2]
    return pl.pallas_call(
        functools.partial(_s5_pair_kernel, t_len), name="s5_scan_pair",
        grid=(nb // SUBLANES, nj // 2),
        in_specs=[pl.BlockSpec((t_len, SUBLANES, 2 * LANES), lambda i, j: (0, i, j)),
                  pl.BlockSpec((2, LANES, sw), lambda i, j: (j, 0, 0)),
                  pl.BlockSpec((2, LANES, sw), lambda i, j: (j, 0, 0)),
                  pl.BlockSpec((2, 4, sw // 4), lambda i, j: (j, 0, 0)),
                  pl.BlockSpec((SUBLANES, 2 * sw), lambda i, j: (i, j))],
        out_specs=[pl.BlockSpec((t_len, SUBLANES, 2 * LANES), lambda i, j: (0, i, j)),
                   pl.BlockSpec((SUBLANES, 2 * sw), lambda i, j: (i, j))],
        out_shape=[jax.ShapeDtypeStruct((t_len, nb, br), F32),
                   jax.ShapeDtypeStruct((nb, nj * sw), F32)],
        scratch_shapes=[pltpu.VMEM((t_len * SUBLANES, sw), F32),
                        pltpu.VMEM((t_len * SUBLANES, sw), F32)],
        compiler_params=_cparams(("parallel", "parallel")),
    )(u_tm, bw, cw, acoef, x0)


def _s5_glu_kernel(y_ref, u_ref, z_ref, d_ref, w_ref, o_ref):
    y = y_ref[...] + d_ref[...] * u_ref[...]
    g = _gelu_tanh(y)
    o_ref[...] = (g * _sigmoid(_bdot(g, w_ref[...])) * _silu(z_ref[...])).astype(o_ref.dtype)


def _s5_glu(y_tm2, u_tm2, main, d_skip, w_glu):
    br = d_skip.shape[1]
    nt = y_tm2.shape[1] // br
    return pl.pallas_call(
        _s5_glu_kernel, name="s5_glu",
        grid=(nt,),
        in_specs=[pl.BlockSpec((SEG, br), lambda i: (0, i)),
                  pl.BlockSpec((SEG, br), lambda i: (0, i)),
                  pl.BlockSpec((SEG, br), lambda i: (i, 0)),
                  pl.BlockSpec((1, br), lambda i: (0, 0)),
                  pl.BlockSpec((br, br), lambda i: (0, 0))],
        out_specs=pl.BlockSpec((SEG, br), lambda i: (i, 0)),
        out_shape=jax.ShapeDtypeStruct((nt * SEG, br), BF16),
        compiler_params=_cparams(("parallel",)),
    )(y_tm2, u_tm2, main, d_skip, w_glu)


def _s5_weights(lam_re, lam_im, log_step, b_re, b_im, c_re, c_im):
    lr, li = lam_re.astype(F32), lam_im.astype(F32)
    step = jnp.exp(log_step.astype(F32))[..., None]
    mag = jnp.exp(lr * step)
    a_r, a_i = mag * jnp.cos(li * step), mag * jnp.sin(li * step)
    den = lr * lr + li * li
    c_r = ((a_r - 1.0) * lr + a_i * li) / den
    c_i = (a_i * lr - (a_r - 1.0) * li) / den
    br_, bi_ = b_re.astype(F32), b_im.astype(F32)
    bbar_r = c_r[..., None] * br_ - c_i[..., None] * bi_
    bbar_i = c_r[..., None] * bi_ + c_i[..., None] * br_
    g = lr.shape[1]
    gl = LANES // S5_GROUP
    nj = g // gl
    eye = jnp.eye(gl, dtype=F32)

    def bw_part(x):
        x = x.reshape(nj, gl, S5_P, S5_GROUP)
        return jnp.einsum('ab,jbps->jasbp', eye, x).reshape(nj, LANES, gl * S5_P)

    def cw_part(x):
        x = x.reshape(nj, gl, S5_GROUP, S5_P)
        return jnp.einsum('ab,jbsp->jbpas', eye, x).reshape(nj, gl * S5_P, LANES)

    bw = jnp.concatenate([bw_part(bbar_r[0]), bw_part(bbar_i[0]),
                          bw_part(bbar_r[1]), bw_part(bbar_i[1])], axis=2)
    cw = jnp.concatenate([cw_part(c_re[0].astype(F32)), cw_part(-c_im[0].astype(F32)),
                          cw_part(c_re[1].astype(F32)), cw_part(-c_im[1].astype(F32))], axis=1)
    acoef = jnp.stack([a_r[0], a_i[0], a_r[1], a_i[1]], axis=0)
    acoef = acoef.reshape(4, nj, gl * S5_P).transpose(1, 0, 2)
    return bw.astype(BF16), cw.transpose(0, 2, 1).astype(BF16), acoef


def _s5_state_to_rows(st):
    nb, _, _, g, p = st.shape
    gl = LANES // S5_GROUP
    nj = g // gl
    return st.reshape(nb, 4, nj, gl * p).transpose(0, 2, 1, 3).reshape(nb, nj * 4 * gl * p)


def _s5_rows_to_state(rows, g, p):
    nb = rows.shape[0]
    gl = LANES // S5_GROUP
    nj = g // gl
    return rows.reshape(nb, nj, 4, gl * p).transpose(0, 2, 1, 3).reshape(nb, 2, 2, g, p)


def _chunk_masks(length):
    row = lax.broadcasted_iota(jnp.int32, (length, length), 0)
    col = lax.broadcasted_iota(jnp.int32, (length, length), 1)
    return row, col


def _to_col(eye, x_row):
    return jnp.sum(jnp.where(eye, x_row, 0.0), axis=1, keepdims=True)


def _cumsum_forms(x_row, eye, incl, incl_t):
    x_col = _to_col(eye, x_row)
    c_col = jnp.sum(jnp.where(incl, x_row, 0.0), axis=1, keepdims=True)
    c_row = jnp.sum(jnp.where(incl_t, x_col, 0.0), axis=0, keepdims=True)
    return c_col, c_row


def _dir_masks(row, col, d):
    if d == 0:
        return col <= row, row <= col, col < row
    return col >= row, row >= col, col > row


def _chunk_loop(nch, body, init):
    return lax.fori_loop(0, nch, body, init, unroll=nch <= 2)


def _state_slab_out(slot, nb, nh, hpb, dk, dv):
    j, n_slots, prev = slot
    shape = jax.ShapeDtypeStruct((nb, n_slots, 2, nh, dk, dv), F32)
    if prev is None:
        return shape, pl.BlockSpec((1, n_slots, 2, hpb, dk, dv), lambda b, h: (b, 0, 0, h, 0, 0))
    return shape, pl.BlockSpec((1, 1, 2, hpb, dk, dv), lambda b, h: (b, j, 0, h, 0, 0))


def _write_state_slab(ref, j, hh, per_dir):
    n_in = ref.shape[1]
    for sl in range(n_in):
        for d in range(2):
            keep = n_in == 1 or sl == j
            ref[0, sl, d, hh] = per_dir[d] if keep else jnp.zeros_like(per_dir[d])


def _mlstm_kernel(t_len, hpb, has_init, slot_j, has_prev, *refs):
    (q_ref, k_ref, v_ref, o_ref, z_ref, ig0_ref, ig1_ref, fg0_ref, fg1_ref,
     ib_ref, fb_ref, ng_ref) = refs[:12]
    pos = 12
    if has_init:
        c0_ref, n0_ref, m0_ref = refs[pos:pos + 3]
        pos += 3
    pos += int(has_prev)
    y_ref = refs[pos]
    pos += 1
    emit_state = slot_j is not None
    if emit_state:
        cf_ref, nf_ref, mf_ref = refs[pos:pos + 3]
        pos += 3
    hf_scr, hb_scr = refs[pos:pos + 2]

    ln = ML_CHUNK
    nch = t_len // ln
    dk = q_ref.shape[1] // hpb
    dv = v_ref.shape[1] // hpb
    b_id = pl.program_id(0)
    nh = pl.num_programs(1) * hpb
    h_base = pl.program_id(1) * hpb
    row, col = _chunk_masks(ln)
    eye = row == col
    scale = dk ** -0.5
    ig_refs = (ig0_ref, ig1_ref)
    fg_refs = (fg0_ref, fg1_ref)
    h_scrs = (hf_scr, hb_scr)
    masks = [_dir_masks(row, col, d) for d in range(2)]

    def gate_terms(hh, d, n, m_st):
        incl, incl_t, _ = masks[d]
        i_row = ig_refs[d][hh, 0, pl.ds(n, 1), :] + ib_ref[d * nh + h_base + hh]
        lf_row = _log_sigmoid(fg_refs[d][hh, 0, pl.ds(n, 1), :] + fb_ref[d * nh + h_base + hh])
        i_col = _to_col(eye, i_row)
        b_col, b_row = _cumsum_forms(lf_row, eye, incl, incl_t)
        c_row = i_row - b_row
        c_col = i_col - b_col
        cm = jnp.where(incl, c_row, -jnp.inf)
        mu = jnp.maximum(m_st, jnp.max(cm, axis=1, keepdims=True))
        mu_last = jnp.maximum(m_st, jnp.max(c_row, axis=1, keepdims=True))
        b_last = jnp.sum(lf_row, axis=1, keepdims=True)
        return dict(e=jnp.exp(cm - mu), inter=jnp.exp(m_st - mu), floor=jnp.exp(-b_col - mu),
                    w=jnp.exp(c_col - mu_last), dec=jnp.exp(m_st - mu_last), m_new=b_last + mu_last)

    def load_qkv(hh, n):
        r0 = pl.multiple_of(n * ln, ln)
        return (r0, q_ref[pl.ds(r0, ln), hh * dk:(hh + 1) * dk],
                k_ref[pl.ds(r0, ln), hh * dk:(hh + 1) * dk] * scale,
                v_ref[pl.ds(r0, ln), hh * dv:(hh + 1) * dv])

    def finish(hh, d, r0, g, s, sv, q, qc, carry):
        num, den = sv, jnp.sum(s, axis=1, keepdims=True)
        if carry is not None:
            num = num + g['inter'] * qc
            den = den + g['inter'] * jnp.sum(q * carry[1], axis=1, keepdims=True)
        h_scrs[d][pl.ds(r0, ln), hh * dv:(hh + 1) * dv] = num / jnp.maximum(jnp.abs(den), g['floor'])

    def new_state(g, kw, kv, carry):
        n_add = jnp.sum(kw, axis=0, keepdims=True)
        if carry is None:
            return kv, n_add, g['m_new']
        return g['dec'] * carry[0] + kv, g['dec'] * carry[1] + n_add, g['m_new']

    def state_products(qs, carries):
        return [None if c is None else _bdot(q, c[0]) for q, c in zip(qs, carries)]

    def step_split(ns, carries):
        items = [(hh, d) for hh in range(hpb) for d in range(2)]
        loads = [load_qkv(hh, ns[d]) for hh, d in items]
        gs = [gate_terms(hh, d, ns[d], carries[c][2]) for c, (hh, d) in enumerate(items)]
        ss = [_bdot_nt(ld[1], ld[2]) * g['e'] for ld, g in zip(loads, gs)]
        svs = [_bdot(sm, ld[3]) for sm, ld in zip(ss, loads)]
        qcs = state_products([ld[1] for ld in loads], carries)
        kws = [ld[2] * g['w'] for ld, g in zip(loads, gs)]
        kvs = [_bdot_tn(kw, ld[3]) for kw, ld in zip(kws, loads)]
        out = []
        for c, (hh, d) in enumerate(items):
            finish(hh, d, loads[c][0], gs[c], ss[c], svs[c], loads[c][1], qcs[c], carries[c])
            out.append(new_state(gs[c], kws[c], kvs[c], carries[c]))
        return out

    def step_shared(carries):
        loads = [load_qkv(hh, 0) for hh in range(hpb)]
        zero_m = jnp.zeros((1, 1), F32)
        gs = [gate_terms(c // 2, c % 2, 0, zero_m if carries[c] is None else carries[c][2])
              for c in range(2 * hpb)]
        qks = [_bdot_nt(ld[1], ld[2]) for ld in loads]
        ss = [qks[c // 2] * gs[c]['e'] for c in range(2 * hpb)]
        svs = [_bdot(jnp.concatenate(ss[2 * hh:2 * hh + 2], axis=0), loads[hh][3])
               for hh in range(hpb)]
        qcs = state_products([loads[c // 2][1] for c in range(2 * hpb)], carries)
        kws = [loads[c // 2][2] * gs[c]['w'] for c in range(2 * hpb)]
        kvs = [_bdot_tn(jnp.concatenate(kws[2 * hh:2 * hh + 2], axis=1), loads[hh][3])
               for hh in range(hpb)]
        out = []
        for c in range(2 * hpb):
            hh, d = c // 2, c % 2
            finish(hh, d, loads[hh][0], gs[c], ss[c], svs[hh][d * ln:(d + 1) * ln], loads[hh][1],
                   qcs[c], carries[c])
            out.append(new_state(gs[c], kws[c], kvs[hh][d * dk:(d + 1) * dk], carries[c]))
        return out

    def body(i, carry):
        return tuple(step_split((i, nch - 1 - i), carry))

    init = []
    for hh in range(hpb):
        for d in range(2):
            if has_init:
                m0 = jnp.zeros((1, 1), F32) + m0_ref[(b_id * 2 + d) * nh + h_base + hh]
                init.append((c0_ref[0, d, hh], n0_ref[0, d, hh], m0))
            else:
                init.append(None)
    if nch == 1:
        fin = step_shared(init)
    else:
        zero_state = (jnp.zeros((dk, dv), F32), jnp.zeros((1, dk), F32), jnp.zeros((1, 1), F32))
        fin = _chunk_loop(nch, body, tuple(zero_state if c is None else c for c in init))
    if emit_state:
        for hh in range(hpb):
            _write_state_slab(cf_ref, slot_j, hh, [fin[hh * 2 + d][0] for d in range(2)])
            for d in range(2):
                nf_ref[0, d, hh] = fin[hh * 2 + d][1]
                mf_ref[0, d, hh] = jnp.broadcast_to(fin[hh * 2 + d][2], (1, LANES))
    for hh in range(hpb):
        sl = slice(hh * dv, (hh + 1) * dv)
        h = _rms_rows(hf_scr[:, sl] + hb_scr[:, sl]) * ng_ref[:, sl]
        y_ref[:, sl] = (_sigmoid(o_ref[:, sl]) * h * _silu(z_ref[:, sl])).astype(y_ref.dtype)


def _mlstm(main, zarr, gr, cols, i_bias, f_bias, norm_g, nb, t_len, init, slot, hpb):
    m = main.shape[0]
    nh = ML_HEADS
    dv = norm_g.shape[1] // nh
    dk = dv // 2
    nch = t_len // ML_CHUNK
    gr3 = gr.reshape(gr.shape[0], nb, nch, ML_CHUNK)

    def colspec(width, off):
        return pl.BlockSpec((t_len, hpb * width), lambda b, h: (b, off // (hpb * width) + h))

    def gspec(base):
        return pl.BlockSpec((hpb, 1, nch, ML_CHUNK), lambda b, h: (base // hpb + h, b, 0, 0))

    smem = pl.BlockSpec(memory_space=pltpu.SMEM)
    in_specs = [colspec(dk, cols['q']), colspec(dk, cols['k']), colspec(dv, cols['v']),
                colspec(dv, cols['o']), colspec(dv, 0),
                gspec(0), gspec(nh), gspec(2 * nh), gspec(3 * nh), smem, smem,
                pl.BlockSpec((1, hpb * dv), lambda b, h: (0, h))]
    args = [main, main, main, main, zarr, gr3, gr3, gr3, gr3,
            i_bias.reshape(-1), f_bias.reshape(-1), norm_g]
    if init is not None:
        c0, n0, m0 = init
        in_specs += [pl.BlockSpec((1, 2, hpb, dk, dv), lambda b, h: (b, 0, h, 0, 0)),
                     pl.BlockSpec((1, 2, hpb, 1, dk), lambda b, h: (b, 0, h, 0, 0)), smem]
        args += [c0, n0.reshape(nb, 2, nh, 1, dk), m0.reshape(-1)]
    out_shape = [jax.ShapeDtypeStruct((m, nh * dv), BF16)]
    out_specs = [pl.BlockSpec((t_len, hpb * dv), lambda b, h: (b, h))]
    aliases = {}
    if slot is not None:
        slab_shape, slab_spec = _state_slab_out(slot, nb, nh, hpb, dk, dv)
        out_shape += [slab_shape, jax.ShapeDtypeStruct((nb, 2, nh, 1, dk), F32),
                      jax.ShapeDtypeStruct((nb, 2, nh, 1, LANES), F32)]
        out_specs += [slab_spec, pl.BlockSpec((1, 2, hpb, 1, dk), lambda b, h: (b, 0, h, 0, 0)),
                      pl.BlockSpec((1, 2, hpb, 1, LANES), lambda b, h: (b, 0, h, 0, 0))]
        if slot[2] is not None:
            aliases = {len(args): 1}
            in_specs.append(pl.BlockSpec(memory_space=pl.ANY))
            args.append(slot[2])
    return pl.pallas_call(
        functools.partial(_mlstm_kernel, t_len, hpb, init is not None,
                          None if slot is None else slot[0], bool(aliases)), name="mlstm",
        grid=(nb, nh // hpb),
        in_specs=in_specs,
        out_specs=out_specs,
        out_shape=out_shape,
        input_output_aliases=aliases,
        scratch_shapes=[pltpu.VMEM((t_len, hpb * dv), F32), pltpu.VMEM((t_len, hpb * dv), F32)],
        compiler_params=_cparams(("parallel", "parallel")),
    )(*args)


def _rope_swap(x):
    quarter = x.shape[1] // 4
    lane = lax.broadcasted_iota(jnp.int32, x.shape, 1)
    up = pltpu.roll(x, x.shape[1] - quarter, 1)
    down = pltpu.roll(x, quarter, 1)
    return jnp.where((lane // quarter) % 2 == 0, up, down)


def _retention_kernel(t_len, hpb, has_pos, has_init, slot_j, has_prev, *refs):
    q_ref, k_ref, v_ref, z_ref, lg_ref = refs[:5]
    pos = 5
    if has_pos:
        cos_ref, sin_ref = refs[pos:pos + 2]
        pos += 2
    if has_init:
        r0_ref = refs[pos]
        pos += 1
    pos += int(has_prev)
    y_ref = refs[pos]
    pos += 1
    emit_state = slot_j is not None
    if emit_state:
        rf_ref = refs[pos]
        pos += 1
    of_scr, ob_scr, q_scr, k_scr = refs[pos:pos + 4]

    ln = ML_CHUNK
    nch = t_len // ln
    dk = q_ref.shape[1] // hpb
    dv = v_ref.shape[1] // hpb
    nh = pl.num_programs(1) * hpb
    h_base = pl.program_id(1) * hpb
    row, col = _chunk_masks(ln)
    scale = dk ** -0.5
    o_scrs = (of_scr, ob_scr)

    dist = jnp.abs(row - col).astype(F32)
    iota_col = lax.broadcasted_iota(jnp.int32, (ln, 1), 0).astype(F32)
    consts = []
    for hh in range(hpb):
        sl = slice(hh * dk, (hh + 1) * dk)
        q = q_ref[:, sl]
        k = k_ref[:, sl]
        if has_pos:
            q = q * cos_ref[...] + _rope_swap(q) * sin_ref[...]
            k = k * cos_ref[...] + _rope_swap(k) * sin_ref[...]
        q_scr[:, sl] = q
        k_scr[:, sl] = k * scale
        for d in range(2):
            incl = _dir_masks(row, col, d)[0]
            lg = _log_sigmoid(jnp.zeros((1, 1), F32) + lg_ref[d * nh + h_base + hh])
            order = iota_col if d == 0 else (ln - 1.0) - iota_col
            dmask = jnp.where(incl, jnp.exp(dist * lg), 0.0)
            consts.append((dmask, jnp.exp((order + 1.0) * lg), jnp.exp((ln - 1.0 - order) * lg),
                           jnp.exp(float(ln) * lg)))

    def load_qkv(hh, n):
        r0 = pl.multiple_of(n * ln, ln)
        return (r0, q_scr[pl.ds(r0, ln), hh * dk:(hh + 1) * dk],
                k_scr[pl.ds(r0, ln), hh * dk:(hh + 1) * dk],
                v_ref[pl.ds(r0, ln), hh * dv:(hh + 1) * dv])

    def step_split(ns, states):
        count = 2 * hpb
        loads = [load_qkv(c // 2, ns[c % 2]) for c in range(count)]
        ss = [_bdot_nt(ld[1], ld[2]) * consts[c][0] for c, ld in enumerate(loads)]
        svs = [_bdot(sm, ld[3]) for sm, ld in zip(ss, loads)]
        qrs = [_bdot(ld[1] * consts[c][1], states[c]) for c, ld in enumerate(loads)]
        kvs = [_bdot_tn(ld[2] * consts[c][2], ld[3]) for c, ld in enumerate(loads)]
        for c in range(count):
            hh, d = c // 2, c % 2
            o_scrs[d][pl.ds(loads[c][0], ln), hh * dv:(hh + 1) * dv] = svs[c] + qrs[c]
        return [consts[c][3] * states[c] + kvs[c] for c in range(count)]

    def step_shared(states):
        count = 2 * hpb
        loads = [load_qkv(hh, 0) for hh in range(hpb)]
        qks = [_bdot_nt(ld[1], ld[2]) for ld in loads]
        svs = [_bdot(jnp.concatenate([qks[hh] * consts[hh * 2 + d][0] for d in range(2)], axis=0),
                     loads[hh][3]) for hh in range(hpb)]
        kvs = [_bdot_tn(jnp.concatenate([loads[hh][2] * consts[hh * 2 + d][2] for d in range(2)],
                                        axis=1), loads[hh][3]) for hh in range(hpb)]
        qrs = [None if states[c] is None else _bdot(loads[c // 2][1] * consts[c][1], states[c])
               for c in range(count)]
        out = []
        for c in range(count):
            hh, d = c // 2, c % 2
            o = svs[hh][d * ln:(d + 1) * ln]
            r_new = kvs[hh][d * dk:(d + 1) * dk]
            if states[c] is not None:
                o = o + qrs[c]
                r_new = r_new + consts[c][3] * states[c]
            o_scrs[d][pl.ds(loads[hh][0], ln), hh * dv:(hh + 1) * dv] = o
            out.append(r_new)
        return out

    def body(i, carry):
        return tuple(step_split((i, nch - 1 - i), carry))

    if has_init:
        init = [r0_ref[0, d, hh] for hh in range(hpb) for d in range(2)]
    else:
        init = [None] * (2 * hpb)
    if nch == 1:
        fin = step_shared(init)
    else:
        fin = _chunk_loop(nch, body, tuple(jnp.zeros((dk, dv), F32) if c is None else c
                                           for c in init))
    for hh in range(hpb):
        if emit_state:
            _write_state_slab(rf_ref, slot_j, hh, fin[hh * 2:hh * 2 + 2])
        sl = slice(hh * dv, (hh + 1) * dv)
        y_ref[:, sl] = (_rms_rows(of_scr[:, sl] + ob_scr[:, sl])
                        * _silu(z_ref[:, sl])).astype(y_ref.dtype)


def _retention(main, cols, decay_logit, nb, t_len, rope, init, slot, hpb):
    m = main.shape[0]
    nh = RET_HEADS
    dk = (cols['k'] - cols['q']) // nh
    dv = 2 * dk

    def colspec(width, off):
        return pl.BlockSpec((t_len, hpb * width), lambda b, h: (b, off // (hpb * width) + h))

    smem = pl.BlockSpec(memory_space=pltpu.SMEM)
    in_specs = [colspec(dk, cols['q']), colspec(dk, cols['k']), colspec(dv, cols['v']),
                colspec(dv, cols['z']), smem]
    args = [main, main, main, main, decay_logit.reshape(-1)]
    if rope is not None:
        in_specs += [pl.BlockSpec((t_len, dk), lambda b, h: (0, 0))] * 2
        args += list(rope)
    if init is not None:
        in_specs.append(pl.BlockSpec((1, 2, hpb, dk, dv), lambda b, h: (b, 0, h, 0, 0)))
        args.append(init)
    out_shape = [jax.ShapeDtypeStruct((m, nh * dv), BF16)]
    out_specs = [pl.BlockSpec((t_len, hpb * dv), lambda b, h: (b, h))]
    aliases = {}
    if slot is not None:
        slab_shape, slab_spec = _state_slab_out(slot, nb, nh, hpb, dk, dv)
        out_shape.append(slab_shape)
        out_specs.append(slab_spec)
        if slot[2] is not None:
            aliases = {len(args): 1}
            in_specs.append(pl.BlockSpec(memory_space=pl.ANY))
            args.append(slot[2])
    return pl.pallas_call(
        functools.partial(_retention_kernel, t_len, hpb, rope is not None, init is not None,
                          None if slot is None else slot[0], bool(aliases)),
        name="retention",
        grid=(nb, nh // hpb),
        in_specs=in_specs,
        out_specs=out_specs,
        out_shape=out_shape,
        input_output_aliases=aliases,
        scratch_shapes=[pltpu.VMEM((t_len, hpb * dv), F32), pltpu.VMEM((t_len, hpb * dv), F32),
                        pltpu.VMEM((t_len, hpb * dk), F32), pltpu.VMEM((t_len, hpb * dk), F32)],
        compiler_params=_cparams(("parallel", "parallel")),
    )(*args)


def _rope_tables(t_len, dk):
    quarter = dk // 4
    t_idx = jnp.arange(t_len)
    row = (t_idx // GRID_W).astype(F32)
    colp = (t_idx % GRID_W).astype(F32)
    freq = ROPE_BASE ** (-jnp.arange(quarter, dtype=F32) / quarter)
    ar = row[:, None] * freq[None]
    ac = colp[:, None] * freq[None]
    cos = jnp.concatenate([jnp.cos(ar), jnp.cos(ar), jnp.cos(ac), jnp.cos(ac)], axis=1)
    sin = jnp.concatenate([-jnp.sin(ar), jnp.sin(ar), -jnp.sin(ac), jnp.sin(ac)], axis=1)
    return cos, sin


def _l2n(x):
    return x * lax.rsqrt(jnp.sum(x * x, axis=-1, keepdims=True) + EPS)


_INV_BASE = 8


def _unit_triangular_inverses(neg_as, row, col):
    ln = neg_as[0].shape[0]
    count = len(neg_as)

    def same_block(size):
        return (row // size) == (col // size)

    base = same_block(_INV_BASE)
    ps = [jnp.where(base, a, 0.0) for a in neg_as]
    invs = [jnp.where(row == col, 1.0, 0.0) + p for p in ps]
    n_sq = int(math.log2(_INV_BASE)) - 1
    ps = [_bdot(p, p) for p in ps]
    for i in range(n_sq):
        if i == n_sq - 1:
            invs = [invs[c] + _bdot(invs[c], ps[c]) for c in range(count)]
        else:
            boths = [_bdot(jnp.concatenate([invs[c], ps[c]], axis=0), ps[c]) for c in range(count)]
            invs = [invs[c] + boths[c][:ln] for c in range(count)]
            ps = [boths[c][ln:] for c in range(count)]
    size = _INV_BASE
    while size < ln:
        sel = same_block(2 * size) & jnp.logical_not(same_block(size))
        mids = [_bdot(invs[c], jnp.where(sel, neg_as[c], 0.0)) for c in range(count)]
        invs = [invs[c] + _bdot(mids[c], invs[c]) for c in range(count)]
        size *= 2
    return invs


def _deltanet_kernel(t_len, hpb, has_init, slot_j, has_prev, *refs):
    (q_ref, k_ref, v_ref, z_ref, a0_ref, a1_ref, b0_ref, b1_ref, cw_ref,
     alog_ref, dtb_ref, ng_ref) = refs[:12]
    pos = 12
    if has_init:
        s0_ref = refs[pos]
        pos += 1
    pos += int(has_prev)
    y_ref = refs[pos]
    pos += 1
    emit_state = slot_j is not None
    if emit_state:
        sf_ref = refs[pos]
        pos += 1
    of_scr, ob_scr, q_scr, k_scr, v_scr = refs[pos:pos + 5]

    ln = DN_CHUNK
    nch = t_len // ln
    dk = q_ref.shape[1] // hpb
    dv = v_ref.shape[1] // hpb
    nh = pl.num_programs(1) * hpb
    h_base = pl.program_id(1) * hpb
    row, col = _chunk_masks(ln)
    eye = row == col
    o_scrs = (of_scr, ob_scr)
    a_refs = (a0_ref, a1_ref)
    b_refs = (b0_ref, b1_ref)
    masks = [_dir_masks(row, col, d) for d in range(2)]

    trow = lax.broadcasted_iota(jnp.int32, (t_len, 1), 0)

    def conv_silu(x, w):
        prev = jnp.where(trow == 0, 0.0, pltpu.roll(x, 1, 0))
        nxt = jnp.where(trow == t_len - 1, 0.0, pltpu.roll(x, t_len - 1, 0))
        return _silu(prev * w[0:1, :] + x * w[1:2, :] + nxt * w[2:3, :])

    for hh in range(hpb):
        sl = slice(hh * dk, (hh + 1) * dk)
        q_scr[:, sl] = _l2n(conv_silu(q_ref[:, sl], cw_ref[hh, 0])) * (dk ** -0.5)
        k_scr[:, sl] = _l2n(conv_silu(k_ref[:, sl], cw_ref[hh, 1]))
        v_scr[:, sl] = conv_silu(v_ref[:, sl], cw_ref[hh, 2])

    def load_qkv(hh, n):
        r0 = pl.multiple_of(n * ln, ln)
        return (r0, q_scr[pl.ds(r0, ln), hh * dk:(hh + 1) * dk],
                k_scr[pl.ds(r0, ln), hh * dk:(hh + 1) * dk],
                v_scr[pl.ds(r0, ln), hh * dv:(hh + 1) * dv])

    def gate_terms(hh, d, n):
        incl, incl_t, _ = masks[d]
        a_row = a_refs[d][hh, 0, pl.ds(n, 1), :]
        beta_row = _sigmoid(b_refs[d][hh, 0, pl.ds(n, 1), :])
        neg_rate = -jnp.exp(jnp.zeros((1, 1), F32) + alog_ref[d * nh + h_base + hh])
        g_row = neg_rate * _softplus(a_row + dtb_ref[d * nh + h_base + hh])
        gc_col, gc_row = _cumsum_forms(g_row, eye, incl, incl_t)
        g_last = jnp.sum(g_row, axis=1, keepdims=True)
        return dict(beta=_to_col(eye, beta_row), gc_col=gc_col, g_last=g_last,
                    decay=jnp.exp(jnp.where(incl, gc_col - gc_row, -jnp.inf)))

    def neg_a_of(d, g, kbk):
        return jnp.where(masks[d][2], -(kbk * g['decay']), 0.0)

    def chunks_all(ns, states):
        items = []
        for hh in range(hpb):
            if nch == 1:
                r0, q, k, v = load_qkv(hh, 0)
                gs = [gate_terms(hh, d, 0) for d in range(2)]
                both = _bdot_nt(jnp.concatenate([k * gs[0]['beta'], k * gs[1]['beta'], q], axis=0), k)
                for d in range(2):
                    items.append((hh, d, r0, gs[d], both[d * ln:(d + 1) * ln], both[2 * ln:], q, k, v))
            else:
                for d in range(2):
                    r0, q, k, v = load_qkv(hh, ns[d])
                    g = gate_terms(hh, d, ns[d])
                    both = _bdot_nt(jnp.concatenate([k * g['beta'], q], axis=0), k)
                    items.append((hh, d, r0, g, both[:ln], both[ln:], q, k, v))
        invs = _unit_triangular_inverses([neg_a_of(it[1], it[3], it[4]) for it in items], row, col)
        rs = [_bdot(inv, jnp.concatenate([it[8] * it[3]['beta'],
                                          it[7] * it[3]['beta'] * jnp.exp(it[3]['gc_col'])], axis=1))
              for inv, it in zip(invs, items)]
        inter = [None if st is None else
                 _bdot(jnp.concatenate([r[:, dv:], it[6] * jnp.exp(it[3]['gc_col'])], axis=0), st)
                 for r, it, st in zip(rs, items, states)]
        v_news = [r[:, :dv] if x is None else r[:, :dv] - x[:ln] for r, x in zip(rs, inter)]
        boths = [_bdot(jnp.concatenate(
            [it[5] * it[3]['decay'], (it[7] * jnp.exp(it[3]['g_last'] - it[3]['gc_col'])).T], axis=0), vn)
            for it, vn in zip(items, v_news)]
        out = []
        for it, x, both, st in zip(items, inter, boths, states):
            hh, d, r0, g = it[:4]
            o_scrs[d][pl.ds(r0, ln), hh * dv:(hh + 1) * dv] = (
                both[:ln] if x is None else both[:ln] + x[ln:])
            out.append(both[ln:] if st is None else both[ln:] + st * jnp.exp(g['g_last']))
        return out

    def body(i, carry):
        return tuple(chunks_all((i, nch - 1 - i), carry))

    if has_init:
        init = [s0_ref[0, d, hh] for hh in range(hpb) for d in range(2)]
    else:
        init = [None] * (2 * hpb)
    if nch == 1:
        fin = chunks_all((0, 0), init)
    else:
        fin = _chunk_loop(nch, body, tuple(jnp.zeros((dk, dv), F32) if c is None else c
                                           for c in init))
    for hh in range(hpb):
        if emit_state:
            _write_state_slab(sf_ref, slot_j, hh, fin[hh * 2:hh * 2 + 2])
        sl = slice(hh * dv, (hh + 1) * dv)
        o = _rms_rows(of_scr[:, sl] + ob_scr[:, sl]) * ng_ref[...]
        y_ref[:, sl] = (o * _silu(z_ref[:, sl])).astype(y_ref.dtype)


def _deltanet(main, zarr, gr, cols, conv_w, a_log, dt_bias, norm_g, nb, t_len, init, slot, hpb):
    m = main.shape[0]
    nh = DN_HEADS
    dk = norm_g.shape[1]
    dv = dk
    nch = t_len // DN_CHUNK
    gr3 = gr.reshape(gr.shape[0], nb, nch, DN_CHUNK)
    cw = conv_w.reshape(conv_w.shape[0], 3, nh, dk).transpose(2, 1, 0, 3)

    def colspec(off):
        return pl.BlockSpec((t_len, hpb * dk), lambda b, h: (b, off // (hpb * dk) + h))

    def gspec(base):
        return pl.BlockSpec((hpb, 1, nch, DN_CHUNK), lambda b, h: (base // hpb + h, b, 0, 0))

    smem = pl.BlockSpec(memory_space=pltpu.SMEM)
    in_specs = [colspec(cols['q']), colspec(cols['k']), colspec(cols['v']), colspec(0),
                gspec(0), gspec(nh), gspec(2 * nh), gspec(3 * nh),
                pl.BlockSpec((hpb, 3, 3, dk), lambda b, h: (h, 0, 0, 0)), smem, smem,
                pl.BlockSpec((1, dv), lambda b, h: (0, 0))]
    args = [main, main, main, zarr, gr3, gr3, gr3, gr3, cw,
            a_log.reshape(-1), dt_bias.reshape(-1), norm_g]
    if init is not None:
        in_specs.append(pl.BlockSpec((1, 2, hpb, dk, dv), lambda b, h: (b, 0, h, 0, 0)))
        args.append(init)
    out_shape = [jax.ShapeDtypeStruct((m, nh * dv), BF16)]
    out_specs = [pl.BlockSpec((t_len, hpb * dv), lambda b, h: (b, h))]
    aliases = {}
    if slot is not None:
        slab_shape, slab_spec = _state_slab_out(slot, nb, nh, hpb, dk, dv)
        out_shape.append(slab_shape)
        out_specs.append(slab_spec)
        if slot[2] is not None:
            aliases = {len(args): 1}
            in_specs.append(pl.BlockSpec(memory_space=pl.ANY))
            args.append(slot[2])
    return pl.pallas_call(
        functools.partial(_deltanet_kernel, t_len, hpb, init is not None,
                          None if slot is None else slot[0], bool(aliases)),
        name="deltanet",
        grid=(nb, nh // hpb),
        in_specs=in_specs,
        out_specs=out_specs,
        out_shape=out_shape,
        input_output_aliases=aliases,
        scratch_shapes=[pltpu.VMEM((t_len, hpb * dv), F32), pltpu.VMEM((t_len, hpb * dv), F32),
                        pltpu.VMEM((t_len, hpb * dk), F32), pltpu.VMEM((t_len, hpb * dk), F32),
                        pltpu.VMEM((t_len, hpb * dv), F32)],
        compiler_params=_cparams(("parallel", "parallel")),
    )(*args)


def _pack_weights(w, gate_lo, gate_hi):
    pad = (-w.shape[1]) % LANES
    return (jnp.pad(w.astype(BF16), ((0, 0), (0, pad))), w[:, gate_lo:gate_hi].T.astype(BF16))


def kernel(x_prompt, x_sample, state_s5, state_ml_c, state_ml_n, state_ml_m, state_ret, state_dn,
           c, c_ctx, mod_w, mod_b, norm_pre, norm_post,
           ab_w_in, ab_w_out, s5_lambda_re, s5_lambda_im, s5_log_step, s5_b_re, s5_b_im,
           s5_c_re, s5_c_im, s5_d, s5_w_glu, ml_i_bias, ml_f_bias, ml_norm,
           cd_w_in, cd_w_out, ret_decay_logit, dn_conv, dn_a_log, dn_dt_bias, dn_norm):
    bp, tp, d = x_prompt.shape
    bs, ts, _ = x_sample.shape
    depth = mod_w.shape[0]
    br = s5_d.shape[1]
    g_s5 = s5_lambda_re.shape[2]
    ml_dv = br // ML_HEADS
    ml_dk = ml_dv // 2
    ret_dv = br // RET_HEADS
    ret_dk = ret_dv // 2
    assert tp == SEG and ts % SEG == 0 and (bs * ts // SEG) == SUBLANES and bp % SUBLANES == 0
    nseg = ts // SEG

    cond = jnp.zeros((SUBLANES, d), F32).at[0].set(c_ctx).at[1:1 + bs].set(c)
    mod = _modulation(cond, mod_w, mod_b)
    mod = mod.reshape(depth, SUBLANES, 3, d)
    mod = mod.at[:, :, 1].add(1.0)
    mod = mod[:, :, jnp.array([1, 0, 2])]

    qk_ml = ML_HEADS * ml_dk
    ab_main = 3 * br + 2 * qk_ml
    ab_cols = {'za': 0, 'q': br, 'k': br + qk_ml, 'v': br + 2 * qk_ml, 'o': 2 * br + 2 * qk_ml}
    ab_gates = (br + ab_main, br + ab_main + 4 * ML_HEADS)
    qk_ret = RET_HEADS * ret_dk
    cd_main = 2 * qk_ret + 2 * br + 3 * br
    ret_cols = {'q': 0, 'k': qk_ret, 'v': 2 * qk_ret, 'z': 2 * qk_ret + br}
    dn_base = 2 * qk_ret + 2 * br
    dn_cols = {'q': dn_base, 'k': dn_base + br, 'v': dn_base + 2 * br}
    cd_gates = (cd_main, cd_main + 4 * DN_HEADS)

    rope = _rope_tables(ts, ret_dk)

    xp = x_prompt.reshape(bp * tp, d)
    xs = x_sample.reshape(bs * ts, d)
    new_s5, new_mn, new_mm = [], [], []
    slab_mc = slab_ret = slab_dn = None
    n_slots = depth // 2
    for l in range(depth):
        j = l // 2
        mod_p = mod[l, 0:1]
        mod_s = mod[l, 1:1 + bs]
        g_pre = norm_pre[l][None]
        g_post = norm_post[l][None]
        if l % 2 == 0:
            w_main, w_gt = _pack_weights(ab_w_in[j], *ab_gates)
            bw, cw, acoef = _s5_weights(s5_lambda_re[j], s5_lambda_im[j], s5_log_step[j],
                                        s5_b_re[j], s5_b_im[j], s5_c_re[j], s5_c_im[j])
            w_glu = s5_w_glu[j].astype(BF16)
            w_out = ab_w_out[j].astype(BF16)
            outs = []
            for (x2, nb, t_len, md, is_sample) in ((xp, bp, tp, mod_p, False), (xs, bs, ts, mod_s, True)):
                rpm = x2.shape[0] // md.shape[0]
                u_tm, main, zb, gr = _inproj(x2, md[:, 0:2], rpm, g_pre, w_main, w_gt, br, ab_main,
                                             ab_gates[1] - ab_gates[0], br)
                nrow = x2.shape[0] // SEG
                if is_sample:
                    st = state_s5[:, j].astype(F32)
                    x0 = jnp.zeros((bs, nseg, 2, 2, g_s5, S5_P), F32)
                    x0 = x0.at[:, 0, 0].set(st[:, 0]).at[:, nseg - 1, 1].set(st[:, 1])
                    x0 = _s5_state_to_rows(x0.reshape(bs * nseg, 2, 2, g_s5, S5_P))
                else:
                    x0 = jnp.zeros((nrow, 4 * g_s5 * S5_P), F32)
                if is_sample:
                    y_tm, xf = _s5_scan(u_tm.reshape(SEG, nrow, br), bw, cw, acoef, x0, nseg)
                else:
                    y_tm, xf = _s5_scan_pair(u_tm.reshape(SEG, nrow, br), bw, cw, acoef, x0)
                ya = _s5_glu(y_tm.reshape(SEG, nrow * br), u_tm, main, s5_d[j][None], w_glu)
                init = None
                if is_sample:
                    init = (state_ml_c[:, j].astype(F32), state_ml_n[:, j].astype(F32),
                            state_ml_m[:, j].astype(F32))
                res = _mlstm(main, zb, gr, ab_cols, ml_i_bias[j], ml_f_bias[j], ml_norm[j][None],
                             nb, t_len, init, None if is_sample else (j, n_slots, slab_mc),
                             2 if is_sample else ML_HEADS)
                yb = res[0]
                if not is_sample:
                    new_s5.append(_s5_rows_to_state(xf, g_s5, S5_P))
                    slab_mc = res[1]
                    new_mn.append(res[2][:, :, :, 0, :])
                    new_mm.append(res[3][:, :, :, 0, 0])
                outs.append(_outproj(ya, yb, w_out, x2, md, rpm, g_post))
            xp, xs = outs
        else:
            w_main, w_gt = _pack_weights(cd_w_in[j], *cd_gates)
            w_out = cd_w_out[j].astype(BF16)
            outs = []
            for (x2, nb, t_len, md, is_sample) in ((xp, bp, tp, mod_p, False), (xs, bs, ts, mod_s, True)):
                rpm = x2.shape[0] // md.shape[0]
                main, zd, gr = _inproj(x2, md[:, 0:2], rpm, g_pre, w_main, w_gt, 0, cd_main,
                                       cd_gates[1] - cd_gates[0], br)
                res_c = _retention(main, ret_cols, ret_decay_logit[j], nb, t_len,
                                   rope if is_sample else None,
                                   state_ret[:, j].astype(F32) if is_sample else None,
                                   None if is_sample else (j, n_slots, slab_ret),
                                   2 if is_sample else RET_HEADS)
                res_d = _deltanet(main, zd, gr, dn_cols, dn_conv[j], dn_a_log[j], dn_dt_bias[j],
                                  dn_norm[j][None], nb, t_len,
                                  state_dn[:, j].astype(F32) if is_sample else None,
                                  None if is_sample else (j, n_slots, slab_dn),
                                  4 if is_sample else DN_HEADS)
                if not is_sample:
                    slab_ret = res_c[1]
                    slab_dn = res_d[1]
                outs.append(_outproj(res_c[0], res_d[0], w_out, x2, md, rpm, g_post))
            xp, xs = outs
    return (xp.reshape(bp, tp, d), xs.reshape(bs, ts, d), jnp.stack(new_s5, 1), slab_mc,
            jnp.stack(new_mn, 1), jnp.stack(new_mm, 1), slab_ret, slab_dn)
```

```python
import functools
import math

import jax
import jax.numpy as jnp
from jax import lax
from jax.experimental import pallas as pl
from jax.experimental.pallas import tpu as pltpu

F32 = jnp.float32
BF16 = jnp.bfloat16

EPS = 1e-6
GRID_W = 64
ROPE_BASE = 10000.0
S5_GROUP = 16
S5_P = 64
ML_HEADS = 4
RET_HEADS = 4
DN_HEADS = 8
LANES = 128
SUBLANES = 8
SEG = 256
ML_CHUNK = 256
DN_CHUNK = 256
VMEM_LIMIT = 56 * 1024 * 1024


def _cparams(sem):
    return pltpu.CompilerParams(dimension_semantics=sem, vmem_limit_bytes=VMEM_LIMIT)


def _bdot(a, b):
    return jnp.dot(a.astype(BF16), b.astype(BF16), preferred_element_type=F32)


def _bdot_nt(a, b):
    return lax.dot_general(a.astype(BF16), b.astype(BF16), (((1,), (1,)), ((), ())),
                           preferred_element_type=F32)


def _bdot_tn(a, b):
    return lax.dot_general(a.astype(BF16), b.astype(BF16), (((0,), (0,)), ((), ())),
                           preferred_element_type=F32)


def _sigmoid(x):
    return 1.0 / (1.0 + jnp.exp(-x))


def _silu(x):
    return x * _sigmoid(x)


def _softplus(x):
    return jnp.maximum(x, 0.0) + jnp.log(1.0 + jnp.exp(-jnp.abs(x)))


def _log_sigmoid(x):
    return -_softplus(-x)


def _gelu_tanh(x):
    return 0.5 * x * (1.0 + jnp.tanh(math.sqrt(2.0 / math.pi) * (x + 0.044715 * (x * x * x))))


def _rms_rows(x):
    return x * lax.rsqrt(jnp.mean(x * x, axis=-1, keepdims=True) + EPS)


def _mod_kernel(cond_ref, w_ref, b_ref, o_ref):
    o_ref[0] = _bdot(_silu(cond_ref[...]), w_ref[0]) + b_ref[0]


def _modulation(cond, mod_w, mod_b):
    depth, d, n3 = mod_w.shape
    tn = 1024
    return pl.pallas_call(
        _mod_kernel, name="modulation",
        grid=(depth, n3 // tn),
        in_specs=[pl.BlockSpec((SUBLANES, d), lambda l, n: (0, 0)),
                  pl.BlockSpec((1, d, tn), lambda l, n: (l, 0, n)),
                  pl.BlockSpec((1, 1, tn), lambda l, n: (l, 0, n))],
        out_specs=pl.BlockSpec((1, SUBLANES, tn), lambda l, n: (l, 0, n)),
        out_shape=jax.ShapeDtypeStruct((depth, SUBLANES, n3), F32),
        compiler_params=_cparams(("parallel", "parallel")),
    )(cond, mod_w, mod_b.reshape(depth, 1, n3))


def _inproj_kernel(n_tm, z_off, x_ref, mod_ref, g_ref, w_ref, wgt_ref, *out_refs):
    x = x_ref[...]
    h = _rms_rows(x) * g_ref[...]
    h = (h * mod_ref[0, 0:1, :] + mod_ref[0, 1:2, :]).astype(BF16)
    tm_ref = out_refs[0] if n_tm else None
    main_ref, z_ref, gr_ref = out_refs[-3:]
    step = 1024
    n_main = main_ref.shape[1]
    for n0 in range(0, n_tm, step):
        tm_ref[:, n0:n0 + step] = jnp.dot(h, w_ref[:, n0:n0 + step], preferred_element_type=F32)
    for n0 in range(0, n_main, step):
        main_ref[:, n0:n0 + step] = jnp.dot(h, w_ref[:, n_tm + n0:n_tm + n0 + step],
                                            preferred_element_type=F32)
    tail = jnp.dot(h, w_ref[:, n_tm + n_main:], preferred_element_type=F32)
    z_ref[...] = tail[:, z_off:z_off + z_ref.shape[1]]
    gr_ref[...] = lax.dot_general(wgt_ref[...], h, (((1,), (1,)), ((), ())),
                                  preferred_element_type=F32)


def _inproj(x2, mod, rows_per_mod, gain, w, wgt, n_tm, n_main, z_off, n_z):
    m, d = x2.shape
    n = w.shape[1]
    ng = wgt.shape[0]
    nt = m // SEG
    tiles_per_mod = rows_per_mod // SEG
    out_shape, out_specs = [], []
    if n_tm:
        out_shape.append(jax.ShapeDtypeStruct((SEG, nt * n_tm), F32))
        out_specs.append(pl.BlockSpec((SEG, n_tm), lambda i: (0, i)))
    out_shape += [jax.ShapeDtypeStruct((m, n_main), F32), jax.ShapeDtypeStruct((m, n_z), F32),
                  jax.ShapeDtypeStruct((ng, m), F32)]
    out_specs += [pl.BlockSpec((SEG, n_main), lambda i: (i, 0)),
                  pl.BlockSpec((SEG, n_z), lambda i: (i, 0)),
                  pl.BlockSpec((ng, SEG), lambda i: (0, i))]
    return pl.pallas_call(
        functools.partial(_inproj_kernel, n_tm, z_off), name="inproj",
        grid=(nt,),
        in_specs=[pl.BlockSpec((SEG, d), lambda i: (i, 0)),
                  pl.BlockSpec((1, 2, d), lambda i: (i // tiles_per_mod, 0, 0)),
                  pl.BlockSpec((1, d), lambda i: (0, 0)),
                  pl.BlockSpec((d, n), lambda i: (0, 0), pipeline_mode=pl.Buffered(1)),
                  pl.BlockSpec((ng, d), lambda i: (0, 0))],
        out_specs=out_specs,
        out_shape=out_shape,
        compiler_params=_cparams(("parallel",)),
    )(x2, mod, gain, w, wgt)


def _outproj_kernel(ya_ref, yb_ref, w_ref, x_ref, mod_ref, g_ref, o_ref):
    half = ya_ref.shape[1]
    acc = (jnp.dot(ya_ref[...], w_ref[0:half, :], preferred_element_type=F32)
           + jnp.dot(yb_ref[...], w_ref[half:2 * half, :], preferred_element_type=F32))
    o_ref[...] = x_ref[...] + mod_ref[0, 2:3, :] * (_rms_rows(acc) * g_ref[...])


def _outproj(ya, yb, w, x2, mod, rows_per_mod, gain):
    m, d = x2.shape
    br = ya.shape[1]
    tm = 2 * SEG
    tiles_per_mod = rows_per_mod // tm
    return pl.pallas_call(
        _outproj_kernel, name="outproj",
        grid=(m // tm,),
        in_specs=[pl.BlockSpec((tm, br), lambda i: (i, 0)),
                  pl.BlockSpec((tm, br), lambda i: (i, 0)),
                  pl.BlockSpec((2 * br, d), lambda i: (0, 0), pipeline_mode=pl.Buffered(1)),
                  pl.BlockSpec((tm, d), lambda i: (i, 0)),
                  pl.BlockSpec((1, 3, d), lambda i: (i // tiles_per_mod, 0, 0)),
                  pl.BlockSpec((1, d), lambda i: (0, 0))],
        out_specs=pl.BlockSpec((tm, d), lambda i: (i, 0)),
        out_shape=jax.ShapeDtypeStruct((m, d), F32),
        compiler_params=_cparams(("parallel",)),
    )(ya, yb, w, x2, mod, gain)


def _outproj_s5_kernel(y_ref, u_ref, z_ref, d_ref, wg_ref, yb_ref, w_ref, x_ref, mod_ref, g_ref,
                       o_ref):
    half = yb_ref.shape[1]
    y = y_ref[...] + d_ref[...] * u_ref[...]
    g = _gelu_tanh(y)
    ya = (g * _sigmoid(_bdot(g, wg_ref[...])) * _silu(z_ref[...])).astype(BF16)
    acc = (jnp.dot(ya, w_ref[0:half, :], preferred_element_type=F32)
           + jnp.dot(yb_ref[...], w_ref[half:2 * half, :], preferred_element_type=F32))
    o_ref[...] = x_ref[...] + mod_ref[0, 2:3, :] * (_rms_rows(acc) * g_ref[...])


def _outproj_s5(y_tm2, u_tm2, main, d_skip, w_glu, yb, w, x2, mod, rows_per_mod, gain):
    m, d = x2.shape
    br = yb.shape[1]
    tiles_per_mod = rows_per_mod // SEG
    return pl.pallas_call(
        _outproj_s5_kernel, name="outproj_s5",
        grid=(m // SEG,),
        in_specs=[pl.BlockSpec((SEG, br), lambda i: (0, i)),
                  pl.BlockSpec((SEG, br), lambda i: (0, i)),
                  pl.BlockSpec((SEG, br), lambda i: (i, 0)),
                  pl.BlockSpec((1, br), lambda i: (0, 0)),
                  pl.BlockSpec((br, br), lambda i: (0, 0), pipeline_mode=pl.Buffered(1)),
                  pl.BlockSpec((SEG, br), lambda i: (i, 0)),
                  pl.BlockSpec((2 * br, d), lambda i: (0, 0), pipeline_mode=pl.Buffered(1)),
                  pl.BlockSpec((SEG, d), lambda i: (i, 0)),
                  pl.BlockSpec((1, 3, d), lambda i: (i // tiles_per_mod, 0, 0)),
                  pl.BlockSpec((1, d), lambda i: (0, 0))],
        out_specs=pl.BlockSpec((SEG, d), lambda i: (i, 0)),
        out_shape=jax.ShapeDtypeStruct((m, d), F32),
        compiler_params=_cparams(("parallel",)),
    )(y_tm2, u_tm2, main, d_skip, w_glu, yb, w, x2, mod, gain)


def _cmul(ar, ai, xr, xi):
    return ar * xr - ai * xi, ar * xi + ai * xr


def _s5_kernel(t_len, nseg, u_ref, bw_ref, cw_ref, a_ref, x0_ref, y_ref, xf_ref, bu_scr):
    hp = xf_ref.shape[1] // 4
    u = u_ref[...].reshape(t_len * SUBLANES, LANES)
    bu_scr[...] = _bdot(u, bw_ref[0])
    a = a_ref[0]
    coef = [jnp.broadcast_to(a[r:r + 1, :], (SUBLANES, hp)) for r in range(4)]

    def scan(init, store):
        def step(t, carry):
            xfr, xfi, xbr, xbi = carry
            rf = pl.multiple_of(t * SUBLANES, SUBLANES)
            rb = pl.multiple_of((t_len - 1 - t) * SUBLANES, SUBLANES)
            pr, pi = _cmul(coef[0], coef[1], xfr, xfi)
            nfr = pr + bu_scr[pl.ds(rf, SUBLANES), 0:hp]
            nfi = pi + bu_scr[pl.ds(rf, SUBLANES), hp:2 * hp]
            pr, pi = _cmul(coef[2], coef[3], xbr, xbi)
            nbr = pr + bu_scr[pl.ds(rb, SUBLANES), 2 * hp:3 * hp]
            nbi = pi + bu_scr[pl.ds(rb, SUBLANES), 3 * hp:4 * hp]
            if store:
                bu_scr[pl.ds(rf, SUBLANES), 0:hp] = nfr
                bu_scr[pl.ds(rf, SUBLANES), hp:2 * hp] = nfi
                bu_scr[pl.ds(rb, SUBLANES), 2 * hp:3 * hp] = nbr
                bu_scr[pl.ds(rb, SUBLANES), 3 * hp:4 * hp] = nbi
            return nfr, nfi, nbr, nbi
        return lax.fori_loop(0, t_len, step, init, unroll=4)

    x0 = tuple(x0_ref[:, r * hp:(r + 1) * hp] for r in range(4))
    if nseg > 1:
        zero = jnp.zeros((SUBLANES, hp), F32)
        ffr, ffi, fbr, fbi = scan((zero, zero, zero, zero), False)
        pfr, pfi, pbr, pbi = coef
        for _ in range(int(math.log2(t_len))):
            pfr, pfi = _cmul(pfr, pfi, pfr, pfi)
            pbr, pbi = _cmul(pbr, pbi, pbr, pbi)
        seg = lax.broadcasted_iota(jnp.int32, (SUBLANES, hp), 0) % nseg
        sfr, sfi, sbr, sbi = x0
        for _ in range(nseg - 1):
            tr, ti = _cmul(pfr, pfi, sfr, sfi)
            sfr = x0[0] + jnp.where(seg >= 1, pltpu.roll(tr + ffr, 1, 0), 0.0)
            sfi = x0[1] + jnp.where(seg >= 1, pltpu.roll(ti + ffi, 1, 0), 0.0)
            tr, ti = _cmul(pbr, pbi, sbr, sbi)
            sbr = x0[2] + jnp.where(seg <= nseg - 2, pltpu.roll(tr + fbr, SUBLANES - 1, 0), 0.0)
            sbi = x0[3] + jnp.where(seg <= nseg - 2, pltpu.roll(ti + fbi, SUBLANES - 1, 0), 0.0)
        x0 = (sfr, sfi, sbr, sbi)
    fin = scan(x0, True)
    for r in range(4):
        xf_ref[:, r * hp:(r + 1) * hp] = fin[r]
    y_t = _bdot_nt(cw_ref[0], bu_scr[...])
    y_ref[...] = y_t.T.reshape(t_len, SUBLANES, LANES)


def _s5_scan(u_tm, bw, cw, acoef, x0, nseg):
    t_len, nb, br = u_tm.shape
    nj = br // LANES
    sw = bw.shape[2]
    assert t_len & (t_len - 1) == 0
    return pl.pallas_call(
        functools.partial(_s5_kernel, t_len, nseg), name="s5_scan",
        grid=(nb // SUBLANES, nj),
        in_specs=[pl.BlockSpec((t_len, SUBLANES, LANES), lambda i, j: (0, i, j)),
                  pl.BlockSpec((1, LANES, sw), lambda i, j: (j, 0, 0)),
                  pl.BlockSpec((1, LANES, sw), lambda i, j: (j, 0, 0)),
                  pl.BlockSpec((1, 4, sw // 4), lambda i, j: (j, 0, 0)),
                  pl.BlockSpec((SUBLANES, sw), lambda i, j: (i, j))],
        out_specs=[pl.BlockSpec((t_len, SUBLANES, LANES), lambda i, j: (0, i, j)),
                   pl.BlockSpec((SUBLANES, sw), lambda i, j: (i, j))],
        out_shape=[jax.ShapeDtypeStruct((t_len, nb, br), F32),
                   jax.ShapeDtypeStruct((nb, nj * sw), F32)],
        scratch_shapes=[pltpu.VMEM((t_len * SUBLANES, sw), F32)],
        compiler_params=_cparams(("parallel", "parallel")),
    )(u_tm, bw, cw, acoef, x0)


S5_PAIR_CHUNKS = 8


def _s5_pair_kernel(t_len, u_ref, bw_ref, cw_ref, a_ref, x0_ref, y_ref, xf_ref, bu_a, bu_b):
    hp = bu_a.shape[1] // 4
    rows = t_len * SUBLANES
    crow = rows // S5_PAIR_CHUNKS
    csteps = t_len // S5_PAIR_CHUNKS
    scrs = (bu_a, bu_b)
    u2 = u_ref[...].reshape(rows, 2 * LANES)

    def bu_chunk(blk, c):
        lhs = u2[c * crow:(c + 1) * crow, blk * LANES:(blk + 1) * LANES]
        scrs[blk][c * crow:(c + 1) * crow, :] = _bdot(lhs, bw_ref[blk])

    def out_chunk(blk, c):
        y_t = _bdot_nt(cw_ref[blk], scrs[blk][c * crow:(c + 1) * crow, :])
        y_ref[c * csteps:(c + 1) * csteps, :, blk * LANES:(blk + 1) * LANES] = (
            y_t.T.reshape(csteps, SUBLANES, LANES))

    def scan_steps(blk, coef, t0, t1, carry):
        scr = scrs[blk]
        xfr, xfi, xbr, xbi = carry
        for t in range(t0, t1):
            rf, rb = t * SUBLANES, (t_len - 1 - t) * SUBLANES
            pr, pi = _cmul(coef[0], coef[1], xfr, xfi)
            xfr = pr + scr[rf:rf + SUBLANES, 0:hp]
            xfi = pi + scr[rf:rf + SUBLANES, hp:2 * hp]
            pr, pi = _cmul(coef[2], coef[3], xbr, xbi)
            xbr = pr + scr[rb:rb + SUBLANES, 2 * hp:3 * hp]
            xbi = pi + scr[rb:rb + SUBLANES, 3 * hp:4 * hp]
            scr[rf:rf + SUBLANES, 0:hp] = xfr
            scr[rf:rf + SUBLANES, hp:2 * hp] = xfi
            scr[rb:rb + SUBLANES, 2 * hp:3 * hp] = xbr
            scr[rb:rb + SUBLANES, 3 * hp:4 * hp] = xbi
        return xfr, xfi, xbr, xbi

    def scan_with(blk, other_work):
        a = a_ref[blk]
        coef = [jnp.broadcast_to(a[r:r + 1, :], (SUBLANES, hp)) for r in range(4)]
        base = blk * 4 * hp
        carry = tuple(x0_ref[:, base + r * hp:base + (r + 1) * hp] for r in range(4))
        for c in range(S5_PAIR_CHUNKS):
            other_work(c)
            carry = scan_steps(blk, coef, c * csteps, (c + 1) * csteps, carry)
        for r in range(4):
            xf_ref[:, base + r * hp:base + (r + 1) * hp] = carry[r]

    for c in range(S5_PAIR_CHUNKS):
        bu_chunk(0, c)
    scan_with(0, lambda c: bu_chunk(1, c))
    scan_with(1, lambda c: out_chunk(0, c))
    for c in range(S5_PAIR_CHUNKS):
        out_chunk(1, c)


def _s5_scan_pair(u_tm, bw, cw, acoef, x0):
    t_len, nb, br = u_tm.shape
    nj = br // LANES
    sw = bw.shape[2]
    return pl.pallas_call(
        functools.partial(_s5_pair_kernel, t_len), name="s5_scan_pair",
        grid=(nb // SUBLANES, nj // 2),
        in_specs=[pl.BlockSpec((t_len, SUBLANES, 2 * LANES), lambda i, j: (0, i, j)),
                  pl.BlockSpec((2, LANES, sw), lambda i, j: (j, 0, 0)),
                  pl.BlockSpec((2, LANES, sw), lambda i, j: (j, 0, 0)),
                  pl.BlockSpec((2, 4, sw // 4), lambda i, j: (j, 0, 0)),
                  pl.BlockSpec((SUBLANES, 2 * sw), lambda i, j: (i, j))],
        out_specs=[pl.BlockSpec((t_len, SUBLANES, 2 * LANES), lambda i, j: (0, i, j)),
                   pl.BlockSpec((SUBLANES, 2 * sw), lambda i, j: (i, j))],
        out_shape=[jax.ShapeDtypeStruct((t_len, nb, br), F32),
                   jax.ShapeDtypeStruct((nb, nj * sw), F32)],
        scratch_shapes=[pltpu.VMEM((t_len * SUBLANES, sw), F32),
                        pltpu.VMEM((t_len * SUBLANES, sw), F32)],
        compiler_params=_cparams(("parallel", "parallel")),
    )(u_tm, bw, cw, acoef, x0)


def _s5_weights(lam_re, lam_im, log_step, b_re, b_im, c_re, c_im):
    lr, li = lam_re.astype(F32), lam_im.astype(F32)
    step = jnp.exp(log_step.astype(F32))[..., None]
    mag = jnp.exp(lr * step)
    a_r, a_i = mag * jnp.cos(li * step), mag * jnp.sin(li * step)
    den = lr * lr + li * li
    c_r = ((a_r - 1.0) * lr + a_i * li) / den
    c_i = (a_i * lr - (a_r - 1.0) * li) / den
    br_, bi_ = b_re.astype(F32), b_im.astype(F32)
    bbar_r = c_r[..., None] * br_ - c_i[..., None] * bi_
    bbar_i = c_r[..., None] * bi_ + c_i[..., None] * br_
    g = lr.shape[1]
    gl = LANES // S5_GROUP
    nj = g // gl
    eye = jnp.eye(gl, dtype=F32)

    def bw_part(x):
        x = x.reshape(nj, gl, S5_P, S5_GROUP)
        return jnp.einsum('ab,jbps->jasbp', eye, x).reshape(nj, LANES, gl * S5_P)

    def cw_part(x):
        x = x.reshape(nj, gl, S5_GROUP, S5_P)
        return jnp.einsum('ab,jbsp->jbpas', eye, x).reshape(nj, gl * S5_P, LANES)

    bw = jnp.concatenate([bw_part(bbar_r[0]), bw_part(bbar_i[0]),
                          bw_part(bbar_r[1]), bw_part(bbar_i[1])], axis=2)
    cw = jnp.concatenate([cw_part(c_re[0].astype(F32)), cw_part(-c_im[0].astype(F32)),
                          cw_part(c_re[1].astype(F32)), cw_part(-c_im[1].astype(F32))], axis=1)
    acoef = jnp.stack([a_r[0], a_i[0], a_r[1], a_i[1]], axis=0)
    acoef = acoef.reshape(4, nj, gl * S5_P).transpose(1, 0, 2)
    return bw.astype(BF16), cw.transpose(0, 2, 1).astype(BF16), acoef


def _s5_state_to_rows(st):
    nb, _, _, g, p = st.shape
    gl = LANES // S5_GROUP
    nj = g // gl
    return st.reshape(nb, 4, nj, gl * p).transpose(0, 2, 1, 3).reshape(nb, nj * 4 * gl * p)


def _s5_rows_to_state(rows, g, p):
    nb = rows.shape[0]
    gl = LANES // S5_GROUP
    nj = g // gl
    return rows.reshape(nb, nj, 4, gl * p).transpose(0, 2, 1, 3).reshape(nb, 2, 2, g, p)


def _chunk_masks(length):
    row = lax.broadcasted_iota(jnp.int32, (length, length), 0)
    col = lax.broadcasted_iota(jnp.int32, (length, length), 1)
    return row, col


def _to_col(eye, x_row):
    return jnp.sum(jnp.where(eye, x_row, 0.0), axis=1, keepdims=True)


def _cumsum_forms(x_row, eye, incl, incl_t):
    x_col = _to_col(eye, x_row)
    c_col = jnp.sum(jnp.where(incl, x_row, 0.0), axis=1, keepdims=True)
    c_row = jnp.sum(jnp.where(incl_t, x_col, 0.0), axis=0, keepdims=True)
    return c_col, c_row


def _dir_masks(row, col, d):
    if d == 0:
        return col <= row, row <= col, col < row
    return col >= row, row >= col, col > row


def _chunk_loop(nch, body, init):
    return lax.fori_loop(0, nch, body, init, unroll=nch <= 2)


def _state_slab_out(slot, nb, nh, hpb, dk, dv):
    j, n_slots, prev = slot
    shape = jax.ShapeDtypeStruct((nb, n_slots, 2, nh, dk, dv), F32)
    if prev is None:
        return shape, pl.BlockSpec((1, n_slots, 2, hpb, dk, dv), lambda b, h: (b, 0, 0, h, 0, 0))
    return shape, pl.BlockSpec((1, 1, 2, hpb, dk, dv), lambda b, h: (b, j, 0, h, 0, 0))


def _write_state_slab(ref, j, hh, per_dir):
    n_in = ref.shape[1]
    for sl in range(n_in):
        for d in range(2):
            keep = n_in == 1 or sl == j
            ref[0, sl, d, hh] = per_dir[d] if keep else jnp.zeros_like(per_dir[d])


def _mlstm_kernel(t_len, hpb, has_init, slot_j, has_prev, *refs):
    (q_ref, k_ref, v_ref, o_ref, z_ref, ig0_ref, ig1_ref, fg0_ref, fg1_ref,
     ib_ref, fb_ref, ng_ref) = refs[:12]
    pos = 12
    if has_init:
        c0_ref, n0_ref, m0_ref = refs[pos:pos + 3]
        pos += 3
    pos += int(has_prev)
    y_ref = refs[pos]
    pos += 1
    emit_state = slot_j is not None
    if emit_state:
        cf_ref, nf_ref, mf_ref = refs[pos:pos + 3]
        pos += 3
    hf_scr, hb_scr = refs[pos:pos + 2]

    ln = ML_CHUNK
    nch = t_len // ln
    dk = q_ref.shape[1] // hpb
    dv = v_ref.shape[1] // hpb
    b_id = pl.program_id(0)
    nh = pl.num_programs(1) * hpb
    h_base = pl.program_id(1) * hpb
    row, col = _chunk_masks(ln)
    eye = row == col
    scale = dk ** -0.5
    ig_refs = (ig0_ref, ig1_ref)
    fg_refs = (fg0_ref, fg1_ref)
    h_scrs = (hf_scr, hb_scr)
    masks = [_dir_masks(row, col, d) for d in range(2)]

    def gate_terms(hh, d, n, m_st):
        incl, incl_t, _ = masks[d]
        i_row = ig_refs[d][hh, 0, pl.ds(n, 1), :] + ib_ref[d * nh + h_base + hh]
        lf_row = _log_sigmoid(fg_refs[d][hh, 0, pl.ds(n, 1), :] + fb_ref[d * nh + h_base + hh])
        i_col = _to_col(eye, i_row)
        b_col, b_row = _cumsum_forms(lf_row, eye, incl, incl_t)
        c_row = i_row - b_row
        c_col = i_col - b_col
        cm = jnp.where(incl, c_row, -jnp.inf)
        mu = jnp.maximum(m_st, jnp.max(cm, axis=1, keepdims=True))
        mu_last = jnp.maximum(m_st, jnp.max(c_row, axis=1, keepdims=True))
        b_last = jnp.sum(lf_row, axis=1, keepdims=True)
        return dict(e=jnp.exp(cm - mu), inter=jnp.exp(m_st - mu), floor=jnp.exp(-b_col - mu),
                    w=jnp.exp(c_col - mu_last), dec=jnp.exp(m_st - mu_last), m_new=b_last + mu_last)

    def load_qkv(hh, n):
        r0 = pl.multiple_of(n * ln, ln)
        return (r0, q_ref[pl.ds(r0, ln), hh * dk:(hh + 1) * dk],
                k_ref[pl.ds(r0, ln), hh * dk:(hh + 1) * dk] * scale,
                v_ref[pl.ds(r0, ln), hh * dv:(hh + 1) * dv])

    def finish(hh, d, r0, g, s, sv, q, qc, carry):
        num, den = sv, jnp.sum(s, axis=1, keepdims=True)
        if carry is not None:
            num = num + g['inter'] * qc
            den = den + g['inter'] * jnp.sum(q * carry[1], axis=1, keepdims=True)
        h_scrs[d][pl.ds(r0, ln), hh * dv:(hh + 1) * dv] = num / jnp.maximum(jnp.abs(den), g['floor'])

    def new_state(g, kw, kv, carry):
        n_add = jnp.sum(kw, axis=0, keepdims=True)
        if carry is None:
            return kv, n_add, g['m_new']
        return g['dec'] * carry[0] + kv, g['dec'] * carry[1] + n_add, g['m_new']

    def state_products(qs, carries):
        return [None if c is None else _bdot(q, c[0]) for q, c in zip(qs, carries)]

    def step_split(ns, carries):
        items = [(hh, d) for hh in range(hpb) for d in range(2)]
        loads = [load_qkv(hh, ns[d]) for hh, d in items]
        gs = [gate_terms(hh, d, ns[d], carries[c][2]) for c, (hh, d) in enumerate(items)]
        ss = [_bdot_nt(ld[1], ld[2]) * g['e'] for ld, g in zip(loads, gs)]
        svs = [_bdot(sm, ld[3]) for sm, ld in zip(ss, loads)]
        qcs = state_products([ld[1] for ld in loads], carries)
        kws = [ld[2] * g['w'] for ld, g in zip(loads, gs)]
        kvs = [_bdot_tn(kw, ld[3]) for kw, ld in zip(kws, loads)]
        out = []
        for c, (hh, d) in enumerate(items):
            finish(hh, d, loads[c][0], gs[c], ss[c], svs[c], loads[c][1], qcs[c], carries[c])
            out.append(new_state(gs[c], kws[c], kvs[c], carries[c]))
        return out

    def step_shared(carries):
        loads = [load_qkv(hh, 0) for hh in range(hpb)]
        zero_m = jnp.zeros((1, 1), F32)
        gs = [gate_terms(c // 2, c % 2, 0, zero_m if carries[c] is None else carries[c][2])
              for c in range(2 * hpb)]
        qks = [_bdot_nt(ld[1], ld[2]) for ld in loads]
        ss = [qks[c // 2] * gs[c]['e'] for c in range(2 * hpb)]
        svs = [_bdot(jnp.concatenate(ss[2 * hh:2 * hh + 2], axis=0), loads[hh][3])
               for hh in range(hpb)]
        qcs = state_products([loads[c // 2][1] for c in range(2 * hpb)], carries)
        kws = [loads[c // 2][2] * gs[c]['w'] for c in range(2 * hpb)]
        kvs = [_bdot_tn(jnp.concatenate(kws[2 * hh:2 * hh + 2], axis=1), loads[hh][3])
               for hh in range(hpb)]
        out = []
        for c in range(2 * hpb):
            hh, d = c // 2, c % 2
            finish(hh, d, loads[hh][0], gs[c], ss[c], svs[hh][d * ln:(d + 1) * ln], loads[hh][1],
                   qcs[c], carries[c])
            out.append(new_state(gs[c], kws[c], kvs[hh][d * dk:(d + 1) * dk], carries[c]))
        return out

    def body(i, carry):
        return tuple(step_split((i, nch - 1 - i), carry))

    init = []
    for hh in range(hpb):
        for d in range(2):
            if has_init:
                m0 = jnp.zeros((1, 1), F32) + m0_ref[(b_id * 2 + d) * nh + h_base + hh]
                init.append((c0_ref[0, d, hh], n0_ref[0, d, hh], m0))
            else:
                init.append(None)
    if nch == 1:
        fin = step_shared(init)
    else:
        zero_state = (jnp.zeros((dk, dv), F32), jnp.zeros((1, dk), F32), jnp.zeros((1, 1), F32))
        fin = _chunk_loop(nch, body, tuple(zero_state if c is None else c for c in init))
    if emit_state:
        for hh in range(hpb):
            _write_state_slab(cf_ref, slot_j, hh, [fin[hh * 2 + d][0] for d in range(2)])
            for d in range(2):
                nf_ref[0, d, hh] = fin[hh * 2 + d][1]
                mf_ref[0, d, hh] = jnp.broadcast_to(fin[hh * 2 + d][2], (1, LANES))
    for hh in range(hpb):
        sl = slice(hh * dv, (hh + 1) * dv)
        h = _rms_rows(hf_scr[:, sl] + hb_scr[:, sl]) * ng_ref[:, sl]
        y_ref[:, sl] = (_sigmoid(o_ref[:, sl]) * h * _silu(z_ref[:, sl])).astype(y_ref.dtype)


def _mlstm(main, zarr, gr, cols, i_bias, f_bias, norm_g, nb, t_len, init, slot, hpb):
    m = main.shape[0]
    nh = ML_HEADS
    dv = norm_g.shape[1] // nh
    dk = dv // 2
    nch = t_len // ML_CHUNK
    gr3 = gr.reshape(gr.shape[0], nb, nch, ML_CHUNK)

    def colspec(width, off):
        return pl.BlockSpec((t_len, hpb * width), lambda b, h: (b, off // (hpb * width) + h))

    def gspec(base):
        return pl.BlockSpec((hpb, 1, nch, ML_CHUNK), lambda b, h: (base // hpb + h, b, 0, 0))

    smem = pl.BlockSpec(memory_space=pltpu.SMEM)
    in_specs = [colspec(dk, cols['q']), colspec(dk, cols['k']), colspec(dv, cols['v']),
                colspec(dv, cols['o']), colspec(dv, 0),
                gspec(0), gspec(nh), gspec(2 * nh), gspec(3 * nh), smem, smem,
                pl.BlockSpec((1, hpb * dv), lambda b, h: (0, h))]
    args = [main, main, main, main, zarr, gr3, gr3, gr3, gr3,
            i_bias.reshape(-1), f_bias.reshape(-1), norm_g]
    if init is not None:
        c0, n0, m0 = init
        in_specs += [pl.BlockSpec((1, 2, hpb, dk, dv), lambda b, h: (b, 0, h, 0, 0)),
                     pl.BlockSpec((1, 2, hpb, 1, dk), lambda b, h: (b, 0, h, 0, 0)), smem]
        args += [c0, n0.reshape(nb, 2, nh, 1, dk), m0.reshape(-1)]
    out_shape = [jax.ShapeDtypeStruct((m, nh * dv), BF16)]
    out_specs = [pl.BlockSpec((t_len, hpb * dv), lambda b, h: (b, h))]
    aliases = {}
    if slot is not None:
        slab_shape, slab_spec = _state_slab_out(slot, nb, nh, hpb, dk, dv)
        out_shape += [slab_shape, jax.ShapeDtypeStruct((nb, 2, nh, 1, dk), F32),
                      jax.ShapeDtypeStruct((nb, 2, nh, 1, LANES), F32)]
        out_specs += [slab_spec, pl.BlockSpec((1, 2, hpb, 1, dk), lambda b, h: (b, 0, h, 0, 0)),
                      pl.BlockSpec((1, 2, hpb, 1, LANES), lambda b, h: (b, 0, h, 0, 0))]
        if slot[2] is not None:
            aliases = {len(args): 1}
            in_specs.append(pl.BlockSpec(memory_space=pl.ANY))
            args.append(slot[2])
    return pl.pallas_call(
        functools.partial(_mlstm_kernel, t_len, hpb, init is not None,
                          None if slot is None else slot[0], bool(aliases)), name="mlstm",
        grid=(nb, nh // hpb),
        in_specs=in_specs,
        out_specs=out_specs,
        out_shape=out_shape,
        input_output_aliases=aliases,
        scratch_shapes=[pltpu.VMEM((t_len, hpb * dv), F32), pltpu.VMEM((t_len, hpb * dv), F32)],
        compiler_params=_cparams(("parallel", "parallel")),
    )(*args)


def _rope_swap(x):
    quarter = x.shape[1] // 4
    lane = lax.broadcasted_iota(jnp.int32, x.shape, 1)
    up = pltpu.roll(x, x.shape[1] - quarter, 1)
    down = pltpu.roll(x, quarter, 1)
    return jnp.where((lane // quarter) % 2 == 0, up, down)


def _retention_kernel(t_len, hpb, has_pos, has_init, slot_j, has_prev, *refs):
    q_ref, k_ref, v_ref, z_ref, lg_ref = refs[:5]
    pos = 5
    if has_pos:
        cos_ref, sin_ref = refs[pos:pos + 2]
        pos += 2
    if has_init:
        r0_ref = refs[pos]
        pos += 1
    pos += int(has_prev)
    y_ref = refs[pos]
    pos += 1
    emit_state = slot_j is not None
    if emit_state:
        rf_ref = refs[pos]
        pos += 1
    of_scr, ob_scr, q_scr, k_scr = refs[pos:pos + 4]

    ln = ML_CHUNK
    nch = t_len // ln
    dk = q_ref.shape[1] // hpb
    dv = v_ref.shape[1] // hpb
    nh = pl.num_programs(1) * hpb
    h_base = pl.program_id(1) * hpb
    row, col = _chunk_masks(ln)
    scale = dk ** -0.5
    o_scrs = (of_scr, ob_scr)

    dist = jnp.abs(row - col).astype(F32)
    iota_col = lax.broadcasted_iota(jnp.int32, (ln, 1), 0).astype(F32)
    consts = []
    for hh in range(hpb):
        sl = slice(hh * dk, (hh + 1) * dk)
        q = q_ref[:, sl]
        k = k_ref[:, sl]
        if has_pos:
            q = q * cos_ref[...] + _rope_swap(q) * sin_ref[...]
            k = k * cos_ref[...] + _rope_swap(k) * sin_ref[...]
        q_scr[:, sl] = q
        k_scr[:, sl] = k * scale
        for d in range(2):
            incl = _dir_masks(row, col, d)[0]
            lg = _log_sigmoid(jnp.zeros((1, 1), F32) + lg_ref[d * nh + h_base + hh])
            order = iota_col if d == 0 else (ln - 1.0) - iota_col
            dmask = jnp.where(incl, jnp.exp(dist * lg), 0.0)
            consts.append((dmask, jnp.exp((order + 1.0) * lg), jnp.exp((ln - 1.0 - order) * lg),
                           jnp.exp(float(ln) * lg)))

    def load_qkv(hh, n):
        r0 = pl.multiple_of(n * ln, ln)
        return (r0, q_scr[pl.ds(r0, ln), hh * dk:(hh + 1) * dk],
                k_scr[pl.ds(r0, ln), hh * dk:(hh + 1) * dk],
                v_ref[pl.ds(r0, ln), hh * dv:(hh + 1) * dv])

    def step_split(ns, states):
        count = 2 * hpb
        loads = [load_qkv(c // 2, ns[c % 2]) for c in range(count)]
        ss = [_bdot_nt(ld[1], ld[2]) * consts[c][0] for c, ld in enumerate(loads)]
        svs = [_bdot(sm, ld[3]) for sm, ld in zip(ss, loads)]
        qrs = [_bdot(ld[1] * consts[c][1], states[c]) for c, ld in enumerate(loads)]
        kvs = [_bdot_tn(ld[2] * consts[c][2], ld[3]) for c, ld in enumerate(loads)]
        for c in range(count):
            hh, d = c // 2, c % 2
            o_scrs[d][pl.ds(loads[c][0], ln), hh * dv:(hh + 1) * dv] = svs[c] + qrs[c]
        return [consts[c][3] * states[c] + kvs[c] for c in range(count)]

    def step_shared(states):
        count = 2 * hpb
        loads = [load_qkv(hh, 0) for hh in range(hpb)]
        qks = [_bdot_nt(ld[1], ld[2]) for ld in loads]
        svs = [_bdot(jnp.concatenate([qks[hh] * consts[hh * 2 + d][0] for d in range(2)], axis=0),
                     loads[hh][3]) for hh in range(hpb)]
        kvs = [_bdot_tn(jnp.concatenate([loads[hh][2] * consts[hh * 2 + d][2] for d in range(2)],
                                        axis=1), loads[hh][3]) for hh in range(hpb)]
        qrs = [None if states[c] is None else _bdot(loads[c // 2][1] * consts[c][1], states[c])
               for c in range(count)]
        out = []
        for c in range(count):
            hh, d = c // 2, c % 2
            o = svs[hh][d * ln:(d + 1) * ln]
            r_new = kvs[hh][d * dk:(d + 1) * dk]
            if states[c] is not None:
                o = o + qrs[c]
                r_new = r_new + consts[c][3] * states[c]
            o_scrs[d][pl.ds(loads[hh][0], ln), hh * dv:(hh + 1) * dv] = o
            out.append(r_new)
        return out

    def body(i, carry):
        return tuple(step_split((i, nch - 1 - i), carry))

    if has_init:
        init = [r0_ref[0, d, hh] for hh in range(hpb) for d in range(2)]
    else:
        init = [None] * (2 * hpb)
    if nch == 1:
        fin = step_shared(init)
    else:
        fin = _chunk_loop(nch, body, tuple(jnp.zeros((dk, dv), F32) if c is None else c
                                           for c in init))
    for hh in range(hpb):
        if emit_state:
            _write_state_slab(rf_ref, slot_j, hh, fin[hh * 2:hh * 2 + 2])
        sl = slice(hh * dv, (hh + 1) * dv)
        y_ref[:, sl] = (_rms_rows(of_scr[:, sl] + ob_scr[:, sl])
                        * _silu(z_ref[:, sl])).astype(y_ref.dtype)


def _retention(main, cols, decay_logit, nb, t_len, rope, init, slot, hpb):
    m = main.shape[0]
    nh = RET_HEADS
    dk = (cols['k'] - cols['q']) // nh
    dv = 2 * dk

    def colspec(width, off):
        return pl.BlockSpec((t_len, hpb * width), lambda b, h: (b, off // (hpb * width) + h))

    smem = pl.BlockSpec(memory_space=pltpu.SMEM)
    in_specs = [colspec(dk, cols['q']), colspec(dk, cols['k']), colspec(dv, cols['v']),
                colspec(dv, cols['z']), smem]
    args = [main, main, main, main, decay_logit.reshape(-1)]
    if rope is not None:
        in_specs += [pl.BlockSpec((t_len, dk), lambda b, h: (0, 0))] * 2
        args += list(rope)
    if init is not None:
        in_specs.append(pl.BlockSpec((1, 2, hpb, dk, dv), lambda b, h: (b, 0, h, 0, 0)))
        args.append(init)
    out_shape = [jax.ShapeDtypeStruct((m, nh * dv), BF16)]
    out_specs = [pl.BlockSpec((t_len, hpb * dv), lambda b, h: (b, h))]
    aliases = {}
    if slot is not None:
        slab_shape, slab_spec = _state_slab_out(slot, nb, nh, hpb, dk, dv)
        out_shape.append(slab_shape)
        out_specs.append(slab_spec)
        if slot[2] is not None:
            aliases = {len(args): 1}
            in_specs.append(pl.BlockSpec(memory_space=pl.ANY))
            args.append(slot[2])
    return pl.pallas_call(
        functools.partial(_retention_kernel, t_len, hpb, rope is not None, init is not None,
                          None if slot is None else slot[0], bool(aliases)),
        name="retention",
        grid=(nb, nh // hpb),
        in_specs=in_specs,
        out_specs=out_specs,
        out_shape=out_shape,
        input_output_aliases=aliases,
        scratch_shapes=[pltpu.VMEM((t_len, hpb * dv), F32), pltpu.VMEM((t_len, hpb * dv), F32),
                        pltpu.VMEM((t_len, hpb * dk), F32), pltpu.VMEM((t_len, hpb * dk), F32)],
        compiler_params=_cparams(("parallel", "parallel")),
    )(*args)


def _rope_tables(t_len, dk):
    quarter = dk // 4
    t_idx = jnp.arange(t_len)
    row = (t_idx // GRID_W).astype(F32)
    colp = (t_idx % GRID_W).astype(F32)
    freq = ROPE_BASE ** (-jnp.arange(quarter, dtype=F32) / quarter)
    ar = row[:, None] * freq[None]
    ac = colp[:, None] * freq[None]
    cos = jnp.concatenate([jnp.cos(ar), jnp.cos(ar), jnp.cos(ac), jnp.cos(ac)], axis=1)
    sin = jnp.concatenate([-jnp.sin(ar), jnp.sin(ar), -jnp.sin(ac), jnp.sin(ac)], axis=1)
    return cos, sin


def _l2n(x):
    return x * lax.rsqrt(jnp.sum(x * x, axis=-1, keepdims=True) + EPS)


_INV_BASE = 8


def _unit_triangular_inverses(neg_as, row, col):
    ln = neg_as[0].shape[0]
    count = len(neg_as)

    def same_block(size):
        return (row // size) == (col // size)

    base = same_block(_INV_BASE)
    ps = [jnp.where(base, a, 0.0) for a in neg_as]
    invs = [jnp.where(row == col, 1.0, 0.0) + p for p in ps]
    n_sq = int(math.log2(_INV_BASE)) - 1
    ps = [_bdot(p, p) for p in ps]
    for i in range(n_sq):
        if i == n_sq - 1:
            invs = [invs[c] + _bdot(invs[c], ps[c]) for c in range(count)]
        else:
            boths = [_bdot(jnp.concatenate([invs[c], ps[c]], axis=0), ps[c]) for c in range(count)]
            invs = [invs[c] + boths[c][:ln] for c in range(count)]
            ps = [boths[c][ln:] for c in range(count)]
    size = _INV_BASE
    while size < ln:
        sel = same_block(2 * size) & jnp.logical_not(same_block(size))
        mids = [_bdot(invs[c], jnp.where(sel, neg_as[c], 0.0)) for c in range(count)]
        invs = [invs[c] + _bdot(mids[c], invs[c]) for c in range(count)]
        size *= 2
    return invs


def _deltanet_kernel(t_len, hpb, has_init, slot_j, has_prev, *refs):
    (q_ref, k_ref, v_ref, z_ref, a0_ref, a1_ref, b0_ref, b1_ref, cw_ref,
     alog_ref, dtb_ref, ng_ref) = refs[:12]
    pos = 12
    if has_init:
        s0_ref = refs[pos]
        pos += 1
    pos += int(has_prev)
    y_ref = refs[pos]
    pos += 1
    emit_state = slot_j is not None
    if emit_state:
        sf_ref = refs[pos]
        pos += 1
    of_scr, ob_scr, q_scr, k_scr, v_scr = refs[pos:pos + 5]

    ln = DN_CHUNK
    nch = t_len // ln
    dk = q_ref.shape[1] // hpb
    dv = v_ref.shape[1] // hpb
    nh = pl.num_programs(1) * hpb
    h_base = pl.program_id(1) * hpb
    row, col = _chunk_masks(ln)
    eye = row == col
    o_scrs = (of_scr, ob_scr)
    a_refs = (a0_ref, a1_ref)
    b_refs = (b0_ref, b1_ref)
    masks = [_dir_masks(row, col, d) for d in range(2)]

    trow = lax.broadcasted_iota(jnp.int32, (t_len, 1), 0)

    def conv_silu(x, w):
        prev = jnp.where(trow == 0, 0.0, pltpu.roll(x, 1, 0))
        nxt = jnp.where(trow == t_len - 1, 0.0, pltpu.roll(x, t_len - 1, 0))
        return _silu(prev * w[0:1, :] + x * w[1:2, :] + nxt * w[2:3, :])

    for hh in range(hpb):
        sl = slice(hh * dk, (hh + 1) * dk)
        q_scr[:, sl] = _l2n(conv_silu(q_ref[:, sl], cw_ref[hh, 0])) * (dk ** -0.5)
        k_scr[:, sl] = _l2n(conv_silu(k_ref[:, sl], cw_ref[hh, 1]))
        v_scr[:, sl] = conv_silu(v_ref[:, sl], cw_ref[hh, 2])

    def load_qkv(hh, n):
        r0 = pl.multiple_of(n * ln, ln)
        return (r0, q_scr[pl.ds(r0, ln), hh * dk:(hh + 1) * dk],
                k_scr[pl.ds(r0, ln), hh * dk:(hh + 1) * dk],
                v_scr[pl.ds(r0, ln), hh * dv:(hh + 1) * dv])

    def gate_terms(hh, d, n):
        incl, incl_t, _ = masks[d]
        a_row = a_refs[d][hh, 0, pl.ds(n, 1), :]
        beta_row = _sigmoid(b_refs[d][hh, 0, pl.ds(n, 1), :])
        neg_rate = -jnp.exp(jnp.zeros((1, 1), F32) + alog_ref[d * nh + h_base + hh])
        g_row = neg_rate * _softplus(a_row + dtb_ref[d * nh + h_base + hh])
        gc_col, gc_row = _cumsum_forms(g_row, eye, incl, incl_t)
        g_last = jnp.sum(g_row, axis=1, keepdims=True)
        return dict(beta=_to_col(eye, beta_row), gc_col=gc_col, g_last=g_last,
                    decay=jnp.exp(jnp.where(incl, gc_col - gc_row, -jnp.inf)))

    def neg_a_of(d, g, kbk):
        return jnp.where(masks[d][2], -(kbk * g['decay']), 0.0)

    def chunks_all(ns, states):
        items = []
        for hh in range(hpb):
            if nch == 1:
                r0, q, k, v = load_qkv(hh, 0)
                gs = [gate_terms(hh, d, 0) for d in range(2)]
                both = _bdot_nt(jnp.concatenate([k * gs[0]['beta'], k * gs[1]['beta'], q], axis=0), k)
                for d in range(2):
                    items.append((hh, d, r0, gs[d], both[d * ln:(d + 1) * ln], both[2 * ln:], q, k, v))
            else:
                for d in range(2):
                    r0, q, k, v = load_qkv(hh, ns[d])
                    g = gate_terms(hh, d, ns[d])
                    both = _bdot_nt(jnp.concatenate([k * g['beta'], q], axis=0), k)
                    items.append((hh, d, r0, g, both[:ln], both[ln:], q, k, v))
        invs = _unit_triangular_inverses([neg_a_of(it[1], it[3], it[4]) for it in items], row, col)
        rs = [_bdot(inv, jnp.concatenate([it[8] * it[3]['beta'],
                                          it[7] * it[3]['beta'] * jnp.exp(it[3]['gc_col'])], axis=1))
              for inv, it in zip(invs, items)]
        inter = [None if st is None else
                 _bdot(jnp.concatenate([r[:, dv:], it[6] * jnp.exp(it[3]['gc_col'])], axis=0), st)
                 for r, it, st in zip(rs, items, states)]
        v_news = [r[:, :dv] if x is None else r[:, :dv] - x[:ln] for r, x in zip(rs, inter)]
        boths = [_bdot(jnp.concatenate(
            [it[5] * it[3]['decay'], (it[7] * jnp.exp(it[3]['g_last'] - it[3]['gc_col'])).T], axis=0), vn)
            for it, vn in zip(items, v_news)]
        out = []
        for it, x, both, st in zip(items, inter, boths, states):
            hh, d, r0, g = it[:4]
            o_scrs[d][pl.ds(r0, ln), hh * dv:(hh + 1) * dv] = (
                both[:ln] if x is None else both[:ln] + x[ln:])
            out.append(both[ln:] if st is None else both[ln:] + st * jnp.exp(g['g_last']))
        return out

    def body(i, carry):
        return tuple(chunks_all((i, nch - 1 - i), carry))

    if has_init:
        init = [s0_ref[0, d, hh] for hh in range(hpb) for d in range(2)]
    else:
        init = [None] * (2 * hpb)
    if nch == 1:
        fin = chunks_all((0, 0), init)
    else:
        fin = _chunk_loop(nch, body, tuple(jnp.zeros((dk, dv), F32) if c is None else c
                                           for c in init))
    for hh in range(hpb):
        if emit_state:
            _write_state_slab(sf_ref, slot_j, hh, fin[hh * 2:hh * 2 + 2])
        sl = slice(hh * dv, (hh + 1) * dv)
        o = _rms_rows(of_scr[:, sl] + ob_scr[:, sl]) * ng_ref[...]
        y_ref[:, sl] = (o * _silu(z_ref[:, sl])).astype(y_ref.dtype)


def _deltanet(main, zarr, gr, cols, conv_w, a_log, dt_bias, norm_g, nb, t_len, init, slot, hpb):
    m = main.shape[0]
    nh = DN_HEADS
    dk = norm_g.shape[1]
    dv = dk
    nch = t_len // DN_CHUNK
    gr3 = gr.reshape(gr.shape[0], nb, nch, DN_CHUNK)
    cw = conv_w.reshape(conv_w.shape[0], 3, nh, dk).transpose(2, 1, 0, 3)

    def colspec(off):
        return pl.BlockSpec((t_len, hpb * dk), lambda b, h: (b, off // (hpb * dk) + h))

    def gspec(base):
        return pl.BlockSpec((hpb, 1, nch, DN_CHUNK), lambda b, h: (base // hpb + h, b, 0, 0))

    smem = pl.BlockSpec(memory_space=pltpu.SMEM)
    in_specs = [colspec(cols['q']), colspec(cols['k']), colspec(cols['v']), colspec(0),
                gspec(0), gspec(nh), gspec(2 * nh), gspec(3 * nh),
                pl.BlockSpec((hpb, 3, 3, dk), lambda b, h: (h, 0, 0, 0)), smem, smem,
                pl.BlockSpec((1, dv), lambda b, h: (0, 0))]
    args = [main, main, main, zarr, gr3, gr3, gr3, gr3, cw,
            a_log.reshape(-1), dt_bias.reshape(-1), norm_g]
    if init is not None:
        in_specs.append(pl.BlockSpec((1, 2, hpb, dk, dv), lambda b, h: (b, 0, h, 0, 0)))
        args.append(init)
    out_shape = [jax.ShapeDtypeStruct((m, nh * dv), BF16)]
    out_specs = [pl.BlockSpec((t_len, hpb * dv), lambda b, h: (b, h))]
    aliases = {}
    if slot is not None:
        slab_shape, slab_spec = _state_slab_out(slot, nb, nh, hpb, dk, dv)
        out_shape.append(slab_shape)
        out_specs.append(slab_spec)
        if slot[2] is not None:
            aliases = {len(args): 1}
            in_specs.append(pl.BlockSpec(memory_space=pl.ANY))
            args.append(slot[2])
    return pl.pallas_call(
        functools.partial(_deltanet_kernel, t_len, hpb, init is not None,
                          None if slot is None else slot[0], bool(aliases)),
        name="deltanet",
        grid=(nb, nh // hpb),
        in_specs=in_specs,
        out_specs=out_specs,
        out_shape=out_shape,
        input_output_aliases=aliases,
        scratch_shapes=[pltpu.VMEM((t_len, hpb * dv), F32), pltpu.VMEM((t_len, hpb * dv), F32),
                        pltpu.VMEM((t_len, hpb * dk), F32), pltpu.VMEM((t_len, hpb * dk), F32),
                        pltpu.VMEM((t_len, hpb * dv), F32)],
        compiler_params=_cparams(("parallel", "parallel")),
    )(*args)


def _pack_weights(w, gate_lo, gate_hi):
    pad = (-w.shape[1]) % LANES
    return (jnp.pad(w.astype(BF16), ((0, 0), (0, pad))), w[:, gate_lo:gate_hi].T.astype(BF16))


def kernel(x_prompt, x_sample, state_s5, state_ml_c, state_ml_n, state_ml_m, state_ret, state_dn,
           c, c_ctx, mod_w, mod_b, norm_pre, norm_post,
           ab_w_in, ab_w_out, s5_lambda_re, s5_lambda_im, s5_log_step, s5_b_re, s5_b_im,
           s5_c_re, s5_c_im, s5_d, s5_w_glu, ml_i_bias, ml_f_bias, ml_norm,
           cd_w_in, cd_w_out, ret_decay_logit, dn_conv, dn_a_log, dn_dt_bias, dn_norm):
    bp, tp, d = x_prompt.shape
    bs, ts, _ = x_sample.shape
    depth = mod_w.shape[0]
    br = s5_d.shape[1]
    g_s5 = s5_lambda_re.shape[2]
    ml_dv = br // ML_HEADS
    ml_dk = ml_dv // 2
    ret_dv = br // RET_HEADS
    ret_dk = ret_dv // 2
    assert tp == SEG and ts % SEG == 0 and (bs * ts // SEG) == SUBLANES and bp % SUBLANES == 0
    nseg = ts // SEG

    cond = jnp.zeros((SUBLANES, d), F32).at[0].set(c_ctx).at[1:1 + bs].set(c)
    mod = _modulation(cond, mod_w, mod_b)
    mod = mod.reshape(depth, SUBLANES, 3, d)
    mod = mod.at[:, :, 1].add(1.0)
    mod = mod[:, :, jnp.array([1, 0, 2])]

    qk_ml = ML_HEADS * ml_dk
    ab_main = 3 * br + 2 * qk_ml
    ab_cols = {'za': 0, 'q': br, 'k': br + qk_ml, 'v': br + 2 * qk_ml, 'o': 2 * br + 2 * qk_ml}
    ab_gates = (br + ab_main, br + ab_main + 4 * ML_HEADS)
    qk_ret = RET_HEADS * ret_dk
    cd_main = 2 * qk_ret + 2 * br + 3 * br
    ret_cols = {'q': 0, 'k': qk_ret, 'v': 2 * qk_ret, 'z': 2 * qk_ret + br}
    dn_base = 2 * qk_ret + 2 * br
    dn_cols = {'q': dn_base, 'k': dn_base + br, 'v': dn_base + 2 * br}
    cd_gates = (cd_main, cd_main + 4 * DN_HEADS)

    rope = _rope_tables(ts, ret_dk)

    xp = x_prompt.reshape(bp * tp, d)
    xs = x_sample.reshape(bs * ts, d)
    new_s5, new_mn, new_mm = [], [], []
    slab_mc = slab_ret = slab_dn = None
    n_slots = depth // 2
    for l in range(depth):
        j = l // 2
        mod_p = mod[l, 0:1]
        mod_s = mod[l, 1:1 + bs]
        g_pre = norm_pre[l][None]
        g_post = norm_post[l][None]
        if l % 2 == 0:
            w_main, w_gt = _pack_weights(ab_w_in[j], *ab_gates)
            bw, cw, acoef = _s5_weights(s5_lambda_re[j], s5_lambda_im[j], s5_log_step[j],
                                        s5_b_re[j], s5_b_im[j], s5_c_re[j], s5_c_im[j])
            w_glu = s5_w_glu[j].astype(BF16)
            w_out = ab_w_out[j].astype(BF16)
            outs = []
            for (x2, nb, t_len, md, is_sample) in ((xp, bp, tp, mod_p, False), (xs, bs, ts, mod_s, True)):
                rpm = x2.shape[0] // md.shape[0]
                u_tm, main, zb, gr = _inproj(x2, md[:, 0:2], rpm, g_pre, w_main, w_gt, br, ab_main,
                                             ab_gates[1] - ab_gates[0], br)
                nrow = x2.shape[0] // SEG
                if is_sample:
                    st = state_s5[:, j].astype(F32)
                    x0 = jnp.zeros((bs, nseg, 2, 2, g_s5, S5_P), F32)
                    x0 = x0.at[:, 0, 0].set(st[:, 0]).at[:, nseg - 1, 1].set(st[:, 1])
                    x0 = _s5_state_to_rows(x0.reshape(bs * nseg, 2, 2, g_s5, S5_P))
                else:
                    x0 = jnp.zeros((nrow, 4 * g_s5 * S5_P), F32)
                if is_sample:
                    y_tm, xf = _s5_scan(u_tm.reshape(SEG, nrow, br), bw, cw, acoef, x0, nseg)
                else:
                    y_tm, xf = _s5_scan_pair(u_tm.reshape(SEG, nrow, br), bw, cw, acoef, x0)
                init = None
                if is_sample:
                    init = (state_ml_c[:, j].astype(F32), state_ml_n[:, j].astype(F32),
                            state_ml_m[:, j].astype(F32))
                res = _mlstm(main, zb, gr, ab_cols, ml_i_bias[j], ml_f_bias[j], ml_norm[j][None],
                             nb, t_len, init, None if is_sample else (j, n_slots, slab_mc),
                             2 if is_sample else ML_HEADS)
                yb = res[0]
                if not is_sample:
                    new_s5.append(_s5_rows_to_state(xf, g_s5, S5_P))
                    slab_mc = res[1]
                    new_mn.append(res[2][:, :, :, 0, :])
                    new_mm.append(res[3][:, :, :, 0, 0])
                outs.append(_outproj_s5(y_tm.reshape(SEG, nrow * br), u_tm, main, s5_d[j][None], w_glu,
                                        yb, w_out, x2, md, rpm, g_post))
            xp, xs = outs
        else:
            w_main, w_gt = _pack_weights(cd_w_in[j], *cd_gates)
            w_out = cd_w_out[j].astype(BF16)
            outs = []
            for (x2, nb, t_len, md, is_sample) in ((xp, bp, tp, mod_p, False), (xs, bs, ts, mod_s, True)):
                rpm = x2.shape[0] // md.shape[0]
                main, zd, gr = _inproj(x2, md[:, 0:2], rpm, g_pre, w_main, w_gt, 0, cd_main,
                                       cd_gates[1] - cd_gates[0], br)
                res_c = _retention(main, ret_cols, ret_decay_logit[j], nb, t_len,
                                   rope if is_sample else None,
                                   state_ret[:, j].astype(F32) if is_sample else None,
                                   None if is_sample else (j, n_slots, slab_ret),
                                   2 if is_sample else RET_HEADS)
                res_d = _deltanet(main, zd, gr, dn_cols, dn_conv[j], dn_a_log[j], dn_dt_bias[j],
                                  dn_norm[j][None], nb, t_len,
                                  state_dn[:, j].astype(F32) if is_sample else None,
                                  None if is_sample else (j, n_slots, slab_dn),
                                  4 if is_sample else DN_HEADS)
                if not is_sample:
                    slab_ret = res_c[1]
                    slab_dn = res_d[1]
                outs.append(_outproj(res_c[0], res_d[0], w_out, x2, md, rpm, g_post))
            xp, xs = outs
    return (xp.reshape(bp, tp, d), xs.reshape(bs, ts, d), jnp.stack(new_s5, 1), slab_mc,
            jnp.stack(new_mn, 1), jnp.stack(new_mm, 1), slab_ret, slab_dn)
```
